```python
import jax
import jax.numpy as jnp
from jax import lax
import numpy as np

D_MODEL = 2048
BATCH = 32
SEQ = 256
DEPTH = 2
DEC_BATCH = 8
DEC_SEQ = 4096
PAST_LEN = 512

GRID_W = 64
HEAD_DIM = 64
A_HEADS = 12
A_KV_HEADS = 4
A_GROUP = A_HEADS // A_KV_HEADS
A_WIDTH = A_HEADS * HEAD_DIM
A_KV_WIDTH = A_KV_HEADS * HEAD_DIM
A_WINDOW = 128
A_BLOCK = 128
B_WIDTH = 512
HYENA_ORDER = 2
HYENA_BANDS = 16
HYENA_EMB = 1 + 2 * HYENA_BANDS
HYENA_HIDDEN = 64
SHORT_CONV = 3
C_HEADS = 12
C_WIDTH = C_HEADS * HEAD_DIM
NA_ROWS = 8
NA_COLS = 16
Q_BLOCK = 128
ROPE_BASE = 10000.0
EPS = 1e-6
NEG_INF = -1e30
N_BRANCH = 3
IN_SPLITS = (A_WIDTH, A_KV_WIDTH, A_KV_WIDTH, A_WIDTH, 3 * B_WIDTH, B_WIDTH, 3 * C_WIDTH, C_WIDTH, N_BRANCH * D_MODEL)
IN_WIDTH = sum(IN_SPLITS)
IN_OFFSETS = tuple(int(o) for o in np.cumsum(IN_SPLITS)[:-1])

kernel_name = "hybrid_diffusion_prefix_trunk_step"

F32 = jnp.float32


def rms_norm(x, w):
    xf = x.astype(F32)
    y = xf * lax.rsqrt(jnp.mean(xf * xf, axis=-1, keepdims=True) + EPS)
    return (y * w.astype(F32)).astype(x.dtype)


def axial_rope(L):
    t = jnp.arange(L)
    row = (t // GRID_W).astype(F32)
    col = (t % GRID_W).astype(F32)
    nf = HEAD_DIM // 4
    inv = jnp.power(ROPE_BASE, -jnp.arange(nf, dtype=F32) / nf)
    ang = jnp.concatenate([row[:, None] * inv[None], col[:, None] * inv[None]], axis=-1)
    return jnp.cos(ang), jnp.sin(ang)


def apply_rope(x, cos, sin):
    xf = x.astype(F32)
    half = HEAD_DIM // 2
    x1, x2 = xf[..., :half], xf[..., half:]
    c = cos[None, :, None, :]
    s = sin[None, :, None, :]
    return jnp.concatenate([x1 * c - x2 * s, x1 * s + x2 * c], axis=-1).astype(x.dtype)


def project(x, cond, p):
    mod = jnp.matmul(jax.nn.silu(cond), p["w_ada"]) + p["b_ada"]
    shift, scale, gate = jnp.split(mod, 3, axis=-1)
    h = rms_norm(x, p["norm_w"]) * (1 + scale) + shift
    u = jnp.matmul(h, p["w_in"])
    return jnp.split(u, list(IN_OFFSETS), axis=-1), gate


def context_attn(q, k, v, sink):
    Bn, S, Hk, G, dh = q.shape
    nb = S // Q_BLOCK
    scale = dh ** -0.5
    qb = jnp.moveaxis(q.reshape(Bn, nb, Q_BLOCK, Hk, G, dh), 1, 0)

    def block(qi):
        s = jnp.einsum('bqkgd,bskd->bkgqs', qi, k, preferred_element_type=F32) * scale
        if sink is not None:
            sk = jnp.broadcast_to(sink.astype(F32)[None, :, :, None, None], s.shape[:-1] + (1,))
            pr = jax.nn.softmax(jnp.concatenate([s, sk], axis=-1), axis=-1)[..., :S]
        else:
            pr = jax.nn.softmax(s, axis=-1)
        return jnp.einsum('bkgqs,bskd->bqkgd', pr.astype(v.dtype), v)

    o = lax.map(block, qb)
    return jnp.moveaxis(o, 0, 1).reshape(Bn, S, Hk * G * dh)


def window_attn_latent(q, k, v, ctx_k, ctx_v, sink):
    Bn, L, Hk, G, dh = q.shape
    P = ctx_k.shape[1]
    nb = L // A_BLOCK
    span = A_BLOCK + 2 * A_WINDOW
    scale = dh ** -0.5
    pad = ((0, 0), (A_WINDOW, A_WINDOW), (0, 0), (0, 0))
    kp = jnp.pad(k, pad)
    vp = jnp.pad(v, pad)
    qb = jnp.moveaxis(q.reshape(Bn, nb, A_BLOCK, Hk, G, dh), 1, 0)
    sink32 = sink.astype(F32)[None, :, :, None, None]

    def block(args):
        i, qi = args
        start = i * A_BLOCK
        kb = lax.dynamic_slice_in_dim(kp, start, span, axis=1)
        vb = lax.dynamic_slice_in_dim(vp, start, span, axis=1)
        qpos = start + jnp.arange(A_BLOCK)
        kpos = start - A_WINDOW + jnp.arange(span)
        valid = (jnp.abs(qpos[:, None] - kpos[None, :]) <= A_WINDOW) & (kpos >= 0)[None, :] & (kpos < L)[None, :]
        s_loc = jnp.einsum('bqkgd,bskd->bkgqs', qi, kb, preferred_element_type=F32) * scale
        s_loc = jnp.where(valid, s_loc, NEG_INF)
        s_ctx = jnp.einsum('bqkgd,bpkd->bkgqp', qi, ctx_k, preferred_element_type=F32) * scale
        sk = jnp.broadcast_to(sink32, s_loc.shape[:-1] + (1,))
        pr = jax.nn.softmax(jnp.concatenate([s_loc, s_ctx, sk], axis=-1), axis=-1).astype(v.dtype)
        return (jnp.einsum('bkgqs,bskd->bqkgd', pr[..., :span], vb)
                + jnp.einsum('bkgqp,bpkd->bqkgd', pr[..., span:span + P], ctx_v))

    o = lax.map(block, (jnp.arange(nb), qb))
    return jnp.moveaxis(o, 0, 1).reshape(Bn, L, Hk * G * dh)


def neighborhood_attn_latent(q, k, v, ctx_k, ctx_v, rpb):
    Bn, L, H, dh = q.shape
    rows = L // GRID_W
    kr = min(NA_ROWS, rows)
    n_nb = kr * NA_COLS
    scale = dh ** -0.5
    qg = jnp.moveaxis(q.reshape(Bn, rows, GRID_W, H, dh), 1, 0)
    kg = k.reshape(Bn, rows, GRID_W, H, dh)
    vg = v.reshape(Bn, rows, GRID_W, H, dh)
    col = jnp.arange(GRID_W)
    cstart = jnp.clip(col - NA_COLS // 2, 0, GRID_W - NA_COLS)
    col_idx = cstart[:, None] + jnp.arange(NA_COLS)[None, :]
    dcol = col_idx - col[:, None] + (NA_COLS - 1)
    rpb32 = rpb.astype(F32)

    def row_block(args):
        r, qr = args
        rstart = jnp.clip(r - NA_ROWS // 2, 0, rows - kr)
        kband = lax.dynamic_slice_in_dim(kg, rstart, kr, axis=1)
        vband = lax.dynamic_slice_in_dim(vg, rstart, kr, axis=1)
        kn = kband[:, :, col_idx]
        vn = vband[:, :, col_idx]
        drow = rstart + jnp.arange(kr) - r + (NA_ROWS - 1)
        bias = rpb32[:, drow[None, :, None], dcol[:, None, :]]
        s_nb = jnp.einsum('bwhd,brwjhd->bhwrj', qr, kn, preferred_element_type=F32) * scale + bias[None]
        s_nb = s_nb.reshape(Bn, H, GRID_W, n_nb)
        s_ctx = jnp.einsum('bwhd,bphd->bhwp', qr, ctx_k, preferred_element_type=F32) * scale
        pr = jax.nn.softmax(jnp.concatenate([s_nb, s_ctx], axis=-1), axis=-1).astype(v.dtype)
        p_nb = pr[..., :n_nb].reshape(Bn, H, GRID_W, kr, NA_COLS)
        return (jnp.einsum('bhwrj,brwjhd->bwhd', p_nb, vn)
                + jnp.einsum('bhwp,bphd->bwhd', pr[..., n_nb:], ctx_v))

    o = lax.map(row_block, (jnp.arange(rows), qg))
    return jnp.moveaxis(o, 0, 1).reshape(Bn, L, H * dh)


def hyena_spectrum(L, p):
    t = jnp.arange(L, dtype=F32) / L
    bands = 2.0 * jnp.pi * jnp.arange(1, HYENA_BANDS + 1, dtype=F32)
    ang = t[:, None] * bands[None, :]
    feats = jnp.concatenate([t[:, None], jnp.sin(ang), jnp.cos(ang)], axis=-1)
    freq = p["hy_freq"].astype(F32)
    z = jnp.sin(freq[0] * (feats @ p["hy_w1"].astype(F32) + p["hy_b1"].astype(F32)))
    z = jnp.sin(freq[1] * (z @ p["hy_w2"].astype(F32) + p["hy_b2"].astype(F32)))
    h = (z @ p["hy_w3"].astype(F32)).reshape(L, HYENA_ORDER, 2, B_WIDTH)
    h = h * jnp.exp(-jnp.abs(p["hy_decay"].astype(F32))[None] * t[:, None, None, None])
    fwd, bwd = h[:, :, 0], h[:, :, 1]
    two = jnp.concatenate([fwd, jnp.zeros((1, HYENA_ORDER, B_WIDTH), F32), bwd[:0:-1]], axis=0)
    return jnp.fft.rfft(two, axis=0)


def fft_conv(u, spec, skip):
    L = u.shape[1]
    uf = u.astype(F32)
    y = jnp.fft.irfft(jnp.fft.rfft(uf, n=2 * L, axis=1) * spec[None], n=2 * L, axis=1)[:, :L]
    return (y + uf * skip.astype(F32)).astype(u.dtype)


def short_conv(u, w, b):
    L = u.shape[1]
    half = SHORT_CONV // 2
    up = jnp.pad(u, ((0, 0), (half, SHORT_CONV - 1 - half), (0, 0)))
    y = b
    for j in range(SHORT_CONV):
        y = y + up[:, j:j + L] * w[j]
    return y


def hyena_mix(u, p):
    L = u.shape[1]
    spec = hyena_spectrum(L, p)
    u = short_conv(u, p["hy_conv_w"], p["hy_conv_b"])
    v, x1, x2 = jnp.split(u, 3, axis=-1)
    z = x1 * fft_conv(v, spec[:, 0], p["hy_skip"][0])
    return x2 * fft_conv(z, spec[:, 1], p["hy_skip"][1])


def merge_branches(ya, ag, yb, bg, yc, cg, mg, p):
    ga, gb, gc = jnp.split(mg, N_BRANCH, axis=-1)
    m = (jax.nn.sigmoid(ga) * jnp.matmul(ya * jax.nn.silu(ag), p["w_up_a"])
         + jax.nn.sigmoid(gb) * jnp.matmul(yb * jax.nn.silu(bg), p["w_up_b"])
         + jax.nn.sigmoid(gc) * jnp.matmul(yc * jax.nn.silu(cg), p["w_up_c"]))
    return jnp.matmul(m, p["w_out"])


def context_layer(x, cond, p):
    Bn, S, _ = x.shape
    (aq, ak, av, ag, bu, bg, cqkv, cg, mg), gate = project(x, cond, p)
    aq = aq.reshape(Bn, S, A_KV_HEADS, A_GROUP, HEAD_DIM)
    ak = ak.reshape(Bn, S, A_KV_HEADS, HEAD_DIM)
    av = av.reshape(Bn, S, A_KV_HEADS, HEAD_DIM)
    ya = context_attn(aq, ak, av, p["a_sink"].reshape(A_KV_HEADS, A_GROUP))
    yb = hyena_mix(bu, p)
    cq, ck, cv = jnp.split(cqkv, 3, axis=-1)
    cq = cq.reshape(Bn, S, C_HEADS, 1, HEAD_DIM)
    ck = ck.reshape(Bn, S, C_HEADS, HEAD_DIM)
    cv = cv.reshape(Bn, S, C_HEADS, HEAD_DIM)
    yc = context_attn(cq, ck, cv, None)
    out = merge_branches(ya, ag, yb, bg, yc, cg, mg, p)
    return x + gate * out, ak, av, ck, cv


def latent_layer(x, cond, ctx_ak, ctx_av, ctx_ck, ctx_cv, p):
    Bn, L, _ = x.shape
    (aq, ak, av, ag, bu, bg, cqkv, cg, mg), gate = project(x, cond, p)
    cos, sin = axial_rope(L)
    aq = apply_rope(aq.reshape(Bn, L, A_HEADS, HEAD_DIM), cos, sin).reshape(Bn, L, A_KV_HEADS, A_GROUP, HEAD_DIM)
    ak = apply_rope(ak.reshape(Bn, L, A_KV_HEADS, HEAD_DIM), cos, sin)
    av = av.reshape(Bn, L, A_KV_HEADS, HEAD_DIM)
    ya = window_attn_latent(aq, ak, av, ctx_ak, ctx_av, p["a_sink"].reshape(A_KV_HEADS, A_GROUP))
    yb = hyena_mix(bu, p)
    cq, ck, cv = jnp.split(cqkv, 3, axis=-1)
    cq = cq.reshape(Bn, L, C_HEADS, HEAD_DIM)
    ck = ck.reshape(Bn, L, C_HEADS, HEAD_DIM)
    cv = cv.reshape(Bn, L, C_HEADS, HEAD_DIM)
    yc = neighborhood_attn_latent(cq, ck, cv, ctx_ck, ctx_cv, p["c_rpb"])
    out = merge_branches(ya, ag, yb, bg, yc, cg, mg, p)
    return x + gate * out


def setup_inputs(seed: int = 0) -> dict:
    key = jax.random.key(seed)
    ks = jax.random.split(key, 32)

    def nrm(k, shape, s):
        return s * jax.random.normal(k, shape, F32)

    return {
        "x_prompt": nrm(ks[0], (BATCH, SEQ, D_MODEL), 1.0),
        "x_sample": nrm(ks[1], (DEC_BATCH, DEC_SEQ, D_MODEL), 1.0),
        "c": nrm(ks[2], (DEC_BATCH, D_MODEL), 1.0),
        "cache_a_k": nrm(ks[3], (DEC_BATCH, DEPTH, PAST_LEN, A_KV_HEADS, HEAD_DIM), 1.0),
        "cache_a_v": nrm(ks[4], (DEC_BATCH, DEPTH, PAST_LEN, A_KV_HEADS, HEAD_DIM), 1.0),
        "cache_c_k": nrm(ks[5], (DEC_BATCH, DEPTH, PAST_LEN, C_HEADS, HEAD_DIM), 1.0),
        "cache_c_v": nrm(ks[6], (DEC_BATCH, DEPTH, PAST_LEN, C_HEADS, HEAD_DIM), 1.0),
        "c_ctx": nrm(ks[7], (D_MODEL,), 1.0),
        "norm_w": 1.0 + nrm(ks[8], (DEPTH, D_MODEL), 0.01),
        "w_ada": nrm(ks[9], (DEPTH, D_MODEL, 3 * D_MODEL), 0.5 * D_MODEL ** -0.5),
        "b_ada": nrm(ks[10], (DEPTH, 3 * D_MODEL), 0.1),
        "w_in": nrm(ks[11], (DEPTH, D_MODEL, IN_WIDTH), D_MODEL ** -0.5),
        "a_sink": nrm(ks[12], (DEPTH, A_HEADS), 1.0),
        "hy_conv_w": nrm(ks[13], (DEPTH, SHORT_CONV, 3 * B_WIDTH), SHORT_CONV ** -0.5),
        "hy_conv_b": nrm(ks[14], (DEPTH, 3 * B_WIDTH), 0.02),
        "hy_w1": nrm(ks[15], (DEPTH, HYENA_EMB, HYENA_HIDDEN), HYENA_EMB ** -0.5),
        "hy_b1": nrm(ks[16], (DEPTH, HYENA_HIDDEN), 0.1),
        "hy_w2": nrm(ks[17], (DEPTH, HYENA_HIDDEN, HYENA_HIDDEN), HYENA_HIDDEN ** -0.5),
        "hy_b2": nrm(ks[18], (DEPTH, HYENA_HIDDEN), 0.1),
        "hy_freq": 1.0 + nrm(ks[19], (DEPTH, 2, HYENA_HIDDEN), 0.1),
        "hy_w3": nrm(ks[20], (DEPTH, HYENA_HIDDEN, HYENA_ORDER * 2 * B_WIDTH), 0.05 * HYENA_HIDDEN ** -0.5),
        "hy_decay": jax.random.uniform(ks[21], (DEPTH, HYENA_ORDER, 2, B_WIDTH), F32, 3.0, 15.0),
        "hy_skip": 1.0 + nrm(ks[22], (DEPTH, HYENA_ORDER, B_WIDTH), 0.1),
        "c_rpb": nrm(ks[23], (DEPTH, C_HEADS, 2 * NA_ROWS - 1, 2 * NA_COLS - 1), 0.1),
        "w_up_a": nrm(ks[24], (DEPTH, A_WIDTH, D_MODEL), A_WIDTH ** -0.5),
        "w_up_b": nrm(ks[25], (DEPTH, B_WIDTH, D_MODEL), B_WIDTH ** -0.5),
        "w_up_c": nrm(ks[26], (DEPTH, C_WIDTH, D_MODEL), C_WIDTH ** -0.5),
        "w_out": nrm(ks[27], (DEPTH, D_MODEL, D_MODEL), D_MODEL ** -0.5),
        "final_norm_w": 1.0 + nrm(ks[28], (D_MODEL,), 0.01),
    }


def reference(x_prompt, x_sample, c, cache_a_k, cache_a_v, cache_c_k, cache_c_v, c_ctx,
              norm_w, w_ada, b_ada, w_in, a_sink, hy_conv_w, hy_conv_b, hy_w1, hy_b1, hy_w2, hy_b2,
              hy_freq, hy_w3, hy_decay, hy_skip, c_rpb, w_up_a, w_up_b, w_up_c, w_out, final_norm_w):
    def layer_params(l):
        return {
            "norm_w": norm_w[l], "w_ada": w_ada[l], "b_ada": b_ada[l], "w_in": w_in[l],
            "a_sink": a_sink[l], "hy_conv_w": hy_conv_w[l], "hy_conv_b": hy_conv_b[l],
            "hy_w1": hy_w1[l], "hy_b1": hy_b1[l], "hy_w2": hy_w2[l], "hy_b2": hy_b2[l],
            "hy_freq": hy_freq[l], "hy_w3": hy_w3[l], "hy_decay": hy_decay[l], "hy_skip": hy_skip[l],
            "c_rpb": c_rpb[l], "w_up_a": w_up_a[l], "w_up_b": w_up_b[l], "w_up_c": w_up_c[l],
            "w_out": w_out[l],
        }

    cond_ctx = c_ctx[None, None, :]
    xp = x_prompt
    aks, avs, cks, cvs = [], [], [], []
    for l in range(DEPTH):
        xp, ak, av, ck, cv = context_layer(xp, cond_ctx, layer_params(l))
        aks.append(ak)
        avs.append(av)
        cks.append(ck)
        cvs.append(cv)
    y_prompt = rms_norm(xp, final_norm_w)
    new_a_k = jnp.stack(aks, axis=1)
    new_a_v = jnp.stack(avs, axis=1)
    new_c_k = jnp.stack(cks, axis=1)
    new_c_v = jnp.stack(cvs, axis=1)

    cond_lat = c[:, None, :]
    xs = x_sample
    for l in range(DEPTH):
        xs = latent_layer(xs, cond_lat, cache_a_k[:, l], cache_a_v[:, l], cache_c_k[:, l], cache_c_v[:, l],
                          layer_params(l))
    y_sample = rms_norm(xs, final_norm_w)

    return (y_prompt, y_sample, new_a_k, new_a_v, new_c_k, new_c_v)
```

```python
import functools

import numpy as np
import jax
import jax.numpy as jnp
from jax import lax
from jax.experimental import pallas as pl
from jax.experimental.pallas import tpu as pltpu

F32 = jnp.float32
BF16 = jnp.bfloat16
HIGHEST = lax.Precision.HIGHEST

D_MODEL = 2048
HEAD_DIM = 64
A_HEADS = 12
A_KV_HEADS = 4
A_GROUP = A_HEADS // A_KV_HEADS
A_WIDTH = A_HEADS * HEAD_DIM
A_KV_WIDTH = A_KV_HEADS * HEAD_DIM
A_WINDOW = 128
B_WIDTH = 512
HYENA_ORDER = 2
HYENA_BANDS = 16
C_HEADS = 12
C_WIDTH = C_HEADS * HEAD_DIM
GRID_W = 64
NA_ROWS = 8
NA_COLS = 16
ROPE_BASE = 10000.0
EPS = 1e-6
NEG_INF = -1e30
ATTN_SCALE = HEAD_DIM ** -0.5

LANES = 128
VMEM_LIMIT = 56 * 1024 * 1024

PROJ_TN = 512
QK_W = A_WIDTH + A_KV_WIDTH
GV_W = A_WIDTH + A_KV_WIDTH
CC_W = 4 * C_WIDTH
MG_W = 3 * D_MODEL
T_W = 4 * B_WIDTH
NAT_W = QK_W + GV_W + CC_W + MG_W


def _cparams(sem):
    return pltpu.CompilerParams(dimension_semantics=sem, vmem_limit_bytes=VMEM_LIMIT)


def _sigmoid(x):
    return 1.0 / (1.0 + jnp.exp(-x))


def _adaln_kernel(cond_ref, w_ref, b_ref, o_ref):
    c = cond_ref[...]
    s = (c * _sigmoid(c)).astype(BF16)
    acc = jnp.dot(s, w_ref[0].astype(BF16), preferred_element_type=F32)
    o_ref[0] = acc + b_ref[0]


def _adaln(cond, w_ada, b_ada):
    depth = w_ada.shape[0]
    rows = cond.shape[0]
    tn = 1024
    return pl.pallas_call(
        _adaln_kernel,
        grid=(depth, 3 * D_MODEL // tn),
        in_specs=[
            pl.BlockSpec((rows, D_MODEL), lambda l, j: (0, 0)),
            pl.BlockSpec((1, D_MODEL, tn), lambda l, j: (l, 0, j)),
            pl.BlockSpec((1, 1, tn), lambda l, j: (l, 0, j)),
        ],
        out_specs=pl.BlockSpec((1, rows, tn), lambda l, j: (l, 0, j)),
        out_shape=jax.ShapeDtypeStruct((depth, rows, 3 * D_MODEL), F32),
        compiler_params=_cparams(("arbitrary", "arbitrary")),
        name="adaln",
    )(cond, w_ada, b_ada.reshape(depth, 1, 3 * D_MODEL))


def _inproj_kernel(x_ref, sc_ref, sh_ref, nw_ref, w_ref, wt_ref, cos_ref, sin_ref,
                   qk_ref, gv_ref, cc_ref, mg_ref, t_ref, h_ref, *, rope, tm, lb):
    j = pl.program_id(1)
    tn = PROJ_TN

    @pl.when(j == 0)
    def _():
        x = x_ref[...]
        var = jnp.mean(x * x, axis=-1, keepdims=True)
        y = x * lax.rsqrt(var + EPS) * nw_ref[...]
        h_ref[...] = (y * sc_ref[0] + sh_ref[0]).astype(BF16)

    def rope_post(acc):
        cos = cos_ref[...]
        sin = sin_ref[...]
        lane = lax.broadcasted_iota(jnp.int32, (tm, LANES), 1)
        first = (lane % HEAD_DIM) < (HEAD_DIM // 2)
        outs = []
        for ci in range(tn // LANES):
            a = acc[:, ci * LANES:(ci + 1) * LANES]
            partner = jnp.where(first, pltpu.roll(a, LANES - HEAD_DIM // 2, 1),
                                pltpu.roll(a, HEAD_DIM // 2, 1))
            outs.append(a * cos + partner * sin)
        return jnp.concatenate(outs, axis=1)

    def nat_group(lo, hi, out_ref, post):
        @pl.when((j >= lo) & (j < hi))
        def _():
            acc = jnp.dot(h_ref[...], w_ref[...], preferred_element_type=F32)
            if post is not None:
                acc = post(acc)
            out_ref[...] = acc.astype(out_ref.dtype)

    n_qk, n_gv, n_cc, n_mg = QK_W // tn, GV_W // tn, CC_W // tn, MG_W // tn
    o0 = 0
    nat_group(o0, o0 + n_qk, qk_ref, rope_post if rope else None)
    o0 += n_qk
    nat_group(o0, o0 + n_gv, gv_ref, None)
    o0 += n_gv
    nat_group(o0, o0 + n_cc, cc_ref, None)
    o0 += n_cc
    nat_group(o0, o0 + n_mg, mg_ref, None)
    o0 += n_mg

    @pl.when(j >= o0)
    def _():
        acc_t = lax.dot_general(wt_ref[...], h_ref[...], (((1,), (1,)), ((), ())),
                                preferred_element_type=F32)
        if tm <= lb:
            t_ref[0] = acc_t
        else:
            for k in range(tm // lb):
                t_ref[k] = acc_t[:, k * lb:(k + 1) * lb]


def _inproj(x2d, sc, sh, norm_w, w_nat, w_t, cos_t, sin_t, *, nb, lb, tm, rope, kv_dtype):
    rows = nb * lb
    tn = PROJ_TN
    n_qk, n_gv, n_cc, n_mg, n_t = QK_W // tn, GV_W // tn, CC_W // tn, MG_W // tn, T_W // tn
    n_nat = n_qk + n_gv + n_cc + n_mg
    nj = n_nat + n_t
    per_mod = sc.shape[0] > 1
    bpb = max(lb // tm, 1)

    def mod_map(i, j):
        return ((i // bpb) if per_mod else 0, 0, 0)

    def grp_map(lo, n):
        return lambda i, j: (i, jnp.clip(j - lo, 0, n - 1))

    if tm <= lb:
        t_spec = pl.BlockSpec((1, tn, tm), lambda i, j: (i // bpb, jnp.clip(j - n_nat, 0, n_t - 1), i % bpb))
    else:
        t_spec = pl.BlockSpec((tm // lb, tn, lb), lambda i, j: (i, jnp.clip(j - n_nat, 0, n_t - 1), 0))

    kernel = functools.partial(_inproj_kernel, rope=rope, tm=tm, lb=lb)
    return pl.pallas_call(
        kernel,
        grid=(rows // tm, nj),
        in_specs=[
            pl.BlockSpec((tm, D_MODEL), lambda i, j: (i, 0)),
            pl.BlockSpec((1, 1, D_MODEL), mod_map),
            pl.BlockSpec((1, 1, D_MODEL), mod_map),
            pl.BlockSpec((1, D_MODEL), lambda i, j: (0, 0)),
            pl.BlockSpec((D_MODEL, tn), lambda i, j: (0, jnp.minimum(j, n_nat - 1))),
            pl.BlockSpec((tn, D_MODEL), lambda i, j: (jnp.maximum(j - n_nat, 0), 0)),
            pl.BlockSpec((tm, LANES), lambda i, j: (i % bpb, 0)),
            pl.BlockSpec((tm, LANES), lambda i, j: (i % bpb, 0)),
        ],
        out_specs=[
            pl.BlockSpec((tm, tn), grp_map(0, n_qk)),
            pl.BlockSpec((tm, tn), grp_map(n_qk, n_gv)),
            pl.BlockSpec((tm, tn), grp_map(n_qk + n_gv, n_cc)),
            pl.BlockSpec((tm, tn), grp_map(n_qk + n_gv + n_cc, n_mg)),
            t_spec,
        ],
        out_shape=[
            jax.ShapeDtypeStruct((rows, QK_W), kv_dtype),
            jax.ShapeDtypeStruct((rows, GV_W), kv_dtype),
            jax.ShapeDtypeStruct((rows, CC_W), kv_dtype),
            jax.ShapeDtypeStruct((rows, MG_W), BF16),
            jax.ShapeDtypeStruct((nb, T_W, lb), F32),
        ],
        scratch_shapes=[pltpu.VMEM((tm, D_MODEL), BF16)],
        compiler_params=_cparams(("arbitrary", "arbitrary")),
        name="inproj_rope" if rope else "inproj",
    )(x2d, sc, sh, norm_w.reshape(1, D_MODEL), w_nat, w_t, cos_t, sin_t)


def _softmax_pv(s, v, sink_col):
    m = jnp.max(s, axis=-1, keepdims=True)
    if sink_col is not None:
        m = jnp.maximum(m, sink_col)
    e = jnp.exp(s - m)
    den = jnp.sum(e, axis=-1, keepdims=True)
    if sink_col is not None:
        den = den + jnp.exp(sink_col - m)
    o = jnp.dot(e.astype(BF16), v, preferred_element_type=F32)
    return o * (1.0 / den)


def _sink_column(sink_ref, g, rows_per_head):
    row_head = lax.broadcasted_iota(jnp.int32, (A_GROUP * rows_per_head, 1), 0) // rows_per_head
    col = jnp.full((A_GROUP * rows_per_head, 1), sink_ref[A_GROUP * g + A_GROUP - 1], F32)
    for hh in range(A_GROUP - 1):
        col = jnp.where(row_head == hh, sink_ref[A_GROUP * g + hh], col)
    return col


def _ctx_attn_kernel(sink_ref, qk_ref, gv_ref, cc_ref, ya_ref, yc_ref):
    qk = qk_ref[0]
    gv = gv_ref[0]
    cc = cc_ref[0]
    s_len = qk.shape[0]
    nt = (((1,), (1,)), ((), ()))

    pieces = [None] * A_HEADS
    for g in range(A_KV_HEADS):
        q = jnp.concatenate(
            [qk[:, (A_GROUP * g + hh) * HEAD_DIM:(A_GROUP * g + hh + 1) * HEAD_DIM] for hh in range(A_GROUP)],
            axis=0).astype(BF16)
        k = qk[:, A_WIDTH + g * HEAD_DIM:A_WIDTH + (g + 1) * HEAD_DIM].astype(BF16)
        v = gv[:, A_WIDTH + g * HEAD_DIM:A_WIDTH + (g + 1) * HEAD_DIM].astype(BF16)
        s = lax.dot_general(q, k, nt, preferred_element_type=F32) * ATTN_SCALE
        o = _softmax_pv(s, v, _sink_column(sink_ref, g, s_len))
        for hh in range(A_GROUP):
            pieces[A_GROUP * g + hh] = o[hh * s_len:(hh + 1) * s_len]
    ya = jnp.concatenate(pieces, axis=1)
    ag = gv[:, :A_WIDTH]
    ya_ref[0] = (ya * (ag * _sigmoid(ag))).astype(BF16)

    pieces = []
    for h in range(C_HEADS):
        q = cc[:, h * HEAD_DIM:(h + 1) * HEAD_DIM].astype(BF16)
        k = cc[:, C_WIDTH + h * HEAD_DIM:C_WIDTH + (h + 1) * HEAD_DIM].astype(BF16)
        v = cc[:, 2 * C_WIDTH + h * HEAD_DIM:2 * C_WIDTH + (h + 1) * HEAD_DIM].astype(BF16)
        s = lax.dot_general(q, k, nt, preferred_element_type=F32) * ATTN_SCALE
        pieces.append(_softmax_pv(s, v, None))
    yc = jnp.concatenate(pieces, axis=1)
    cg = cc[:, 3 * C_WIDTH:]
    yc_ref[0] = (yc * (cg * _sigmoid(cg))).astype(BF16)


def _ctx_attn(sink, qk, gv, cc, *, nb, lb):
    return pl.pallas_call(
        _ctx_attn_kernel,
        grid=(nb,),
        in_specs=[
            pl.BlockSpec(memory_space=pltpu.SMEM),
            pl.BlockSpec((1, lb, QK_W), lambda b: (b, 0, 0)),
            pl.BlockSpec((1, lb, GV_W), lambda b: (b, 0, 0)),
            pl.BlockSpec((1, lb, CC_W), lambda b: (b, 0, 0)),
        ],
        out_specs=[
            pl.BlockSpec((1, lb, A_WIDTH), lambda b: (b, 0, 0)),
            pl.BlockSpec((1, lb, C_WIDTH), lambda b: (b, 0, 0)),
        ],
        out_shape=[
            jax.ShapeDtypeStruct((nb, lb, A_WIDTH), BF16),
            jax.ShapeDtypeStruct((nb, lb, C_WIDTH), BF16),
        ],
        compiler_params=_cparams(("arbitrary",)),
        name="ctx_attn",
    )(sink, qk.reshape(nb, lb, QK_W), gv.reshape(nb, lb, GV_W), cc.reshape(nb, lb, CC_W))


def _lat_win_kernel(sink_ref, q_ref, kp_ref, kc_ref, kn_ref, vp_ref, vc_ref, vn_ref, ag_ref,
                    ck_ref, cv_ref, o_ref):
    i = pl.program_id(1)
    nblk = pl.num_programs(1)
    blk = A_WINDOW
    q = q_ref[0]
    ctx_k = ck_ref[0, 0].astype(BF16)
    ctx_v = cv_ref[0, 0].astype(BF16)
    n_ctx = ctx_k.shape[0]
    k_all = jnp.concatenate([kp_ref[0], kc_ref[0], kn_ref[0], ctx_k], axis=0)
    v_all = jnp.concatenate([vp_ref[0], vc_ref[0], vn_ref[0], ctx_v], axis=0)
    n_keys = 3 * blk + n_ctx
    nt = (((1,), (1,)), ((), ()))

    rows = A_GROUP * blk
    qi = lax.broadcasted_iota(jnp.int32, (rows, n_keys), 0) % blk
    col = lax.broadcasted_iota(jnp.int32, (rows, n_keys), 1)
    far = 4 * blk
    prev_off = jnp.where(i > 0, 0, far)
    next_off = jnp.where(i < nblk - 1, 0, far)
    valid = ((col >= qi + prev_off) | (col >= blk)) & \
            ((col - 2 * blk <= qi - next_off) | (col < 2 * blk) | (col >= 3 * blk))

    pieces = [None] * A_HEADS
    for g in range(A_KV_HEADS):
        qg = jnp.concatenate(
            [q[:, (A_GROUP * g + hh) * HEAD_DIM:(A_GROUP * g + hh + 1) * HEAD_DIM] for hh in range(A_GROUP)],
            axis=0)
        kg = k_all[:, g * HEAD_DIM:(g + 1) * HEAD_DIM]
        vg = v_all[:, g * HEAD_DIM:(g + 1) * HEAD_DIM]
        s = lax.dot_general(qg, kg, nt, preferred_element_type=F32) * ATTN_SCALE
        s = jnp.where(valid, s, NEG_INF)
        o = _softmax_pv(s, vg, _sink_column(sink_ref, g, blk))
        for hh in range(A_GROUP):
            pieces[A_GROUP * g + hh] = o[hh * blk:(hh + 1) * blk]
    ya = jnp.concatenate(pieces, axis=1)
    ag = ag_ref[0].astype(F32)
    o_ref[0] = (ya * (ag * _sigmoid(ag))).astype(BF16)


def _lat_win_attn(sink, qk, gv, cache_k, cache_v, layer, *, nb, lb):
    blk = A_WINDOW
    nblk = lb // blk
    n_ctx = cache_k.shape[2]
    qk3 = qk.reshape(nb, lb, QK_W)
    gv3 = gv.reshape(nb, lb, GV_W)
    kcol = A_WIDTH // A_KV_WIDTH

    def kv_spec(delta):
        return pl.BlockSpec((1, blk, A_KV_WIDTH),
                            lambda b, i: (b, jnp.clip(i + delta, 0, nblk - 1), kcol))

    ctx_spec = pl.BlockSpec((1, 1, n_ctx, A_KV_WIDTH), lambda b, i: (b, layer, 0, 0))
    return pl.pallas_call(
        _lat_win_kernel,
        grid=(nb, nblk),
        in_specs=[
            pl.BlockSpec(memory_space=pltpu.SMEM),
            pl.BlockSpec((1, blk, A_WIDTH), lambda b, i: (b, i, 0)),
            kv_spec(-1), kv_spec(0), kv_spec(1),
            kv_spec(-1), kv_spec(0), kv_spec(1),
            pl.BlockSpec((1, blk, A_WIDTH), lambda b, i: (b, i, 0)),
            ctx_spec, ctx_spec,
        ],
        out_specs=pl.BlockSpec((1, blk, A_WIDTH), lambda b, i: (b, i, 0)),
        out_shape=jax.ShapeDtypeStruct((nb, lb, A_WIDTH), BF16),
        compiler_params=_cparams(("arbitrary", "arbitrary")),
        name="lat_win_attn",
    )(sink, qk3, qk3, qk3, qk3, gv3, gv3, gv3, gv3, cache_k, cache_v)


NBR_QROWS = 2
NBR_BAND = NA_ROWS + NBR_QROWS


def _lat_nbr_kernel(q_ref, k_ref, v_ref, cg_ref, ck_ref, cv_ref, t2_ref, o_ref, *, grid_rows):
    i = pl.program_id(1)
    r0 = NBR_QROWS * i
    nq = NBR_QROWS * GRID_W
    nk = NBR_BAND * GRID_W
    bs = jnp.clip(r0 - NA_ROWS // 2, 0, grid_rows - NBR_BAND)
    start = pl.multiple_of(bs * GRID_W, GRID_W)
    k_band = k_ref[0, pl.ds(start, nk), :]
    v_band = v_ref[0, pl.ds(start, nk), :]
    q = q_ref[0]
    ctx_k = ck_ref[0, 0].astype(BF16)
    ctx_v = cv_ref[0, 0].astype(BF16)
    nt = (((1,), (1,)), ((), ()))

    q_row = r0 + lax.broadcasted_iota(jnp.int32, (nq, nk), 0) // GRID_W
    rstart = jnp.clip(q_row - NA_ROWS // 2, 0, grid_rows - NA_ROWS)
    k_row = bs + lax.broadcasted_iota(jnp.int32, (nq, nk), 1) // GRID_W
    valid = (k_row >= rstart) & (k_row < rstart + NA_ROWS)

    pieces = []
    for h in range(C_HEADS):
        sl = slice(h * HEAD_DIM, (h + 1) * HEAD_DIM)
        qh = q[:, sl]
        bias_rows = []
        for a in range(NBR_QROWS):
            tiles = []
            for p in range(NBR_BAND // 2):
                d = (bs + 2 * p) - (r0 + a) + (NA_ROWS - 1)
                idx = jnp.clip(d, -1, 2 * NA_ROWS - 2) + 1
                tiles.append(t2_ref[h, idx])
            bias_rows.append(jnp.concatenate(tiles, axis=1))
        bias = jnp.concatenate(bias_rows, axis=0)
        s_nb = lax.dot_general(qh, k_band[:, sl], nt, preferred_element_type=F32) * ATTN_SCALE + bias
        s_nb = jnp.where(valid, s_nb, NEG_INF)
        s_cx = lax.dot_general(qh, ctx_k[:, sl], nt, preferred_element_type=F32) * ATTN_SCALE
        m = jnp.maximum(jnp.max(s_nb, axis=-1, keepdims=True), jnp.max(s_cx, axis=-1, keepdims=True))
        e_nb = jnp.exp(s_nb - m)
        e_cx = jnp.exp(s_cx - m)
        den = jnp.sum(e_nb, axis=-1, keepdims=True) + jnp.sum(e_cx, axis=-1, keepdims=True)
        o = (jnp.dot(e_nb.astype(BF16), v_band[:, sl], preferred_element_type=F32)
             + jnp.dot(e_cx.astype(BF16), ctx_v[:, sl], preferred_element_type=F32))
        pieces.append(o * (1.0 / den))
    yc = jnp.concatenate(pieces, axis=1)
    cg = cg_ref[0].astype(F32)
    o_ref[0] = (yc * (cg * _sigmoid(cg))).astype(BF16)


def _nbr_bias_table(rpb):
    w = np.arange(GRID_W)
    cstart = np.clip(w - NA_COLS // 2, 0, GRID_W - NA_COLS)
    j = np.arange(GRID_W)
    in_win = (j[None, :] >= cstart[:, None]) & (j[None, :] < cstart[:, None] + NA_COLS)
    dcol = np.clip(j[None, :] - w[:, None] + NA_COLS - 1, 0, 2 * NA_COLS - 2)
    t = rpb.astype(F32)[:, :, dcol]
    t = jnp.where(jnp.asarray(in_win)[None, None], t, NEG_INF)
    pad = jnp.full((rpb.shape[0], 1, GRID_W, GRID_W), NEG_INF, F32)
    t = jnp.concatenate([pad, t, pad], axis=1)
    return jnp.concatenate([t[:, :-1], t[:, 1:]], axis=-1)


def _lat_nbr_attn(cc, cache_k, cache_v, layer, t2, *, nb, lb):
    grid_rows = lb // GRID_W
    assert grid_rows >= NBR_BAND and grid_rows % NBR_QROWS == 0
    nq = NBR_QROWS * GRID_W
    n_ctx = cache_k.shape[2]
    cc3 = cc.reshape(nb, lb, CC_W)
    ctx_spec = pl.BlockSpec((1, 1, n_ctx, C_WIDTH), lambda b, i: (b, layer, 0, 0))
    kernel = functools.partial(_lat_nbr_kernel, grid_rows=grid_rows)
    return pl.pallas_call(
        kernel,
        grid=(nb, grid_rows // NBR_QROWS),
        in_specs=[
            pl.BlockSpec((1, nq, C_WIDTH), lambda b, i: (b, i, 0)),
            pl.BlockSpec((1, lb, C_WIDTH), lambda b, i: (b, 0, 1)),
            pl.BlockSpec((1, lb, C_WIDTH), lambda b, i: (b, 0, 2)),
            pl.BlockSpec((1, nq, C_WIDTH), lambda b, i: (b, i, 3)),
            ctx_spec, ctx_spec,
            pl.BlockSpec(t2.shape, lambda b, i: (0, 0, 0, 0)),
        ],
        out_specs=pl.BlockSpec((1, nq, C_WIDTH), lambda b, i: (b, i, 0)),
        out_shape=jax.ShapeDtypeStruct((nb, lb, C_WIDTH), BF16),
        compiler_params=_cparams(("arbitrary", "arbitrary")),
        name="lat_nbr_attn",
    )(cc3, cc3, cc3, cc3, cache_k, cache_v, t2)


def _hyena_filters(L, hy_w1, hy_b1, hy_w2, hy_b2, hy_freq, hy_w3, hy_decay):
    t = jnp.arange(L, dtype=F32) / L
    bands = 2.0 * jnp.pi * jnp.arange(1, HYENA_BANDS + 1, dtype=F32)
    ang = t[:, None] * bands[None, :]
    feats = jnp.concatenate([t[:, None], jnp.sin(ang), jnp.cos(ang)], axis=-1)
    z = jnp.sin(hy_freq[0] * (jnp.matmul(feats, hy_w1, precision=HIGHEST) + hy_b1))
    z = jnp.sin(hy_freq[1] * (jnp.matmul(z, hy_w2, precision=HIGHEST) + hy_b2))
    h = jnp.matmul(z, hy_w3, precision=HIGHEST).reshape(L, HYENA_ORDER, 2, B_WIDTH)
    h = h * jnp.exp(-jnp.abs(hy_decay)[None] * t[:, None, None, None])
    return jnp.transpose(h, (1, 2, 3, 0))


def _short_conv_params(hy_conv_w, hy_conv_b):
    return jnp.concatenate([hy_conv_w.T, hy_conv_b[:, None]], axis=1)


def _dense_dft_consts(S):
    n = 2 * S
    t = np.arange(S)[:, None]
    k = np.arange(n)[None, :]
    ang = -2.0 * np.pi * t * k / n
    cr, ci = np.cos(ang), np.sin(ang)
    wf = np.block([[cr, ci], [-ci, cr]])
    er, ei = cr.T, -ci.T
    wi = np.block([[er, ei], [-ei, er]])
    return jnp.asarray(wf, F32), jnp.asarray(wi, F32)


def _ctx_spec_kernel(f_ref, skip_ref, wf_ref, hr_ref, hi_ref, *, S):
    n = 2 * S
    fwd = f_ref[0, 0]
    bwd = f_ref[0, 1]
    lane = lax.broadcasted_iota(jnp.int32, bwd.shape, 1)
    bwd = jnp.where(lane == 0, 0.0, bwd)
    w = wf_ref[0:S, :]
    ff = jnp.dot(fwd, w, precision=HIGHEST, preferred_element_type=F32)
    fb = jnp.dot(bwd, w, precision=HIGHEST, preferred_element_type=F32)
    skip = skip_ref[0]
    hr_ref[0] = (ff[:, :n] + fb[:, :n] + skip) * (1.0 / n)
    hi_ref[0] = (ff[:, n:] - fb[:, n:]) * (1.0 / n)


def _ctx_spectrum(filt, skip, wf, S):
    n = 2 * S
    kernel = functools.partial(_ctx_spec_kernel, S=S)
    return pl.pallas_call(
        kernel,
        grid=(HYENA_ORDER,),
        in_specs=[
            pl.BlockSpec((1, 2, B_WIDTH, S), lambda o: (o, 0, 0, 0)),
            pl.BlockSpec((1, B_WIDTH, 1), lambda o: (o, 0, 0)),
            pl.BlockSpec(wf.shape, lambda o: (0, 0)),
        ],
        out_specs=[pl.BlockSpec((1, B_WIDTH, n), lambda o: (o, 0, 0))] * 2,
        out_shape=[jax.ShapeDtypeStruct((HYENA_ORDER, B_WIDTH, n), F32)] * 2,
        compiler_params=_cparams(("arbitrary",)),
        name="hyena_ctx_spectrum",
    )(filt, skip.reshape(HYENA_ORDER, B_WIDTH, 1), wf)


def _hyena_ctx_kernel(prm_ref, v_ref, x1_ref, x2_ref, bg_ref, hr_ref, hi_ref, wf_ref, wi_ref, o_ref, *, S):
    nb, cb, _ = v_ref.shape
    half = nb // 2
    n = 2 * S
    lane = lax.broadcasted_iota(jnp.int32, (nb, cb, S), 2)

    def short_conv(u, p):
        prev = jnp.where(lane == 0, 0.0, pltpu.roll(u, 1, 2))
        nxt = jnp.where(lane == S - 1, 0.0, pltpu.roll(u, S - 1, 2))
        return p[:, 3:4] + prev * p[:, 0:1] + u * p[:, 1:2] + nxt * p[:, 2:3]

    def stack(u):
        return jnp.concatenate([u[:half], u[half:]], axis=-1)

    def conv(xs, o):
        spec = jnp.dot(xs.reshape(half * cb, 2 * S), wf_ref[...], precision=HIGHEST,
                       preferred_element_type=F32).reshape(half, cb, 2 * n)
        xr, xi = spec[..., :n], spec[..., n:]
        hr, hi = hr_ref[o], hi_ref[o]
        y = jnp.concatenate([xr * hr - xi * hi, xr * hi + xi * hr], axis=-1)
        return jnp.dot(y.reshape(half * cb, 2 * n), wi_ref[...], precision=HIGHEST,
                       preferred_element_type=F32).reshape(half, cb, 2 * S)

    v = stack(short_conv(v_ref[...], prm_ref[0]))
    x1 = stack(short_conv(x1_ref[...], prm_ref[1]))
    x2 = stack(short_conv(x2_ref[...], prm_ref[2]))
    z = x1 * conv(v, 0)
    y = x2 * conv(z, 1)
    bg = stack(bg_ref[...])
    y = y * (bg * _sigmoid(bg))
    o_ref[0:half] = y[..., :S]
    o_ref[half:nb] = y[..., S:]


def _hyena_ctx(t_arr, prm, hr, hi, wf, wi, *, nb, S):
    cb = 16
    n = 2 * S
    ncb = B_WIDTH // cb
    kernel = functools.partial(_hyena_ctx_kernel, S=S)

    def part(k):
        return pl.BlockSpec((nb, cb, S), lambda c: (0, c + k * ncb, 0))

    return pl.pallas_call(
        kernel,
        grid=(ncb,),
        in_specs=[
            pl.BlockSpec((3, cb, 4), lambda c: (0, c, 0)),
            part(0), part(1), part(2), part(3),
            pl.BlockSpec((HYENA_ORDER, cb, n), lambda c: (0, c, 0)),
            pl.BlockSpec((HYENA_ORDER, cb, n), lambda c: (0, c, 0)),
            pl.BlockSpec(wf.shape, lambda c: (0, 0)),
            pl.BlockSpec(wi.shape, lambda c: (0, 0)),
        ],
        out_specs=pl.BlockSpec((nb, cb, S), lambda c: (0, c, 0)),
        out_shape=jax.ShapeDtypeStruct((nb, B_WIDTH, S), F32),
        compiler_params=_cparams(("arbitrary",)),
        name="hyena_ctx",
    )(prm.reshape(3, B_WIDTH, 4), t_arr, t_arr, t_arr, t_arr, hr, hi, wf, wi)


def _two_stage_consts(L, nseq):
    n2 = LANES
    n = 2 * L
    n1 = n // n2
    h1 = n1 // 2
    k1 = np.arange(n1)[:, None]
    a = -2.0 * np.pi * k1 * np.arange(h1)[None, :] / n1
    w1r, w1i = np.cos(a), np.sin(a)
    w1big = np.block([[w1r, -w1i], [w1i, w1r]])
    vr, vi = w1r.T, -w1i.T
    w1inv = np.block([[vr, -vi], [vi, vr]])
    a = -2.0 * np.pi * k1 * np.arange(n2)[None, :] / n
    twr, twi = np.cos(a), np.sin(a)
    a = -2.0 * np.pi * np.arange(n2)[:, None] * np.arange(n2)[None, :] / n2
    w2r, w2i = np.cos(a), np.sin(a)
    w2big = np.block([[w2r, w2i], [-w2i, w2r]])
    w2conj = np.block([[w2r, -w2i], [w2i, w2r]])
    f = lambda x: jnp.asarray(x, F32)
    return dict(
        w1b=jnp.broadcast_to(f(w1big)[None], (nseq,) + w1big.shape),
        w1ib=jnp.broadcast_to(f(w1inv)[None], (nseq,) + w1inv.shape),
        w2=f(w2big), w2c=f(w2conj), twr=f(twr), twi=f(twi), n1=n1, h1=h1)


def _stage_fwd(xs, w1b, twr, twi, w2):
    s = xs.shape[0]
    n1 = twr.shape[0]
    a = jnp.einsum('smk,skn->smn', w1b, xs, precision=HIGHEST, preferred_element_type=F32)
    ar, ai = a[:, :n1], a[:, n1:]
    p = jnp.concatenate([ar * twr - ai * twi, ar * twi + ai * twr], axis=-1)
    return jnp.dot(p.reshape(s * n1, 2 * LANES), w2, precision=HIGHEST, preferred_element_type=F32)


def _stage_inv(cm, w1ib, twr, twi, w2c, s):
    n1 = twr.shape[0]
    dm = jnp.dot(cm, w2c, precision=HIGHEST, preferred_element_type=F32).reshape(s, n1, 2 * LANES)
    dr, di = dm[..., :LANES], dm[..., LANES:]
    r = jnp.concatenate([dr * twr + di * twi, di * twr - dr * twi], axis=1)
    return jnp.einsum('smk,skn->smn', w1ib, r, precision=HIGHEST, preferred_element_type=F32)


def _lat_spec_kernel(f_ref, skip_ref, w1b_ref, twr_ref, twi_ref, w2_ref, hr_ref, hi_ref, *, n_fft):
    cb, h1 = f_ref.shape[2], f_ref.shape[3]
    n1 = twr_ref.shape[0]
    fwd = f_ref[0, 0]
    bwd = f_ref[0, 1]
    first = (lax.broadcasted_iota(jnp.int32, bwd.shape, 1) == 0) & \
            (lax.broadcasted_iota(jnp.int32, bwd.shape, 2) == 0)
    bwd = jnp.where(first, 0.0, bwd)
    xs = jnp.concatenate([fwd, bwd], axis=0)
    xs = jnp.concatenate([xs, jnp.zeros_like(xs)], axis=1)
    sp = _stage_fwd(xs, w1b_ref[...], twr_ref[...], twi_ref[...], w2_ref[...]).reshape(2, cb, n1, 2 * LANES)
    inv = 1.0 / n_fft
    for c in range(cb):
        skip = skip_ref[pl.program_id(0), pl.program_id(1) * cb + c]
        hr_ref[0, c] = (sp[0, c, :, :LANES] + sp[1, c, :, :LANES] + skip) * inv
        hi_ref[0, c] = (sp[0, c, :, LANES:] - sp[1, c, :, LANES:]) * inv


def _lat_spectrum(filt, skip, L):
    cb = 8
    cs = _two_stage_consts(L, 2 * cb)
    n1, h1 = cs["n1"], cs["h1"]
    filt5 = filt.reshape(HYENA_ORDER, 2, B_WIDTH, h1, LANES)
    kernel = functools.partial(_lat_spec_kernel, n_fft=2 * L)
    const = lambda a: pl.BlockSpec(a.shape, lambda o, c: (0,) * a.ndim)
    return pl.pallas_call(
        kernel,
        grid=(HYENA_ORDER, B_WIDTH // cb),
        in_specs=[
            pl.BlockSpec((1, 2, cb, h1, LANES), lambda o, c: (o, 0, c, 0, 0)),
            pl.BlockSpec(memory_space=pltpu.SMEM),
            const(cs["w1b"]), const(cs["twr"]), const(cs["twi"]), const(cs["w2"]),
        ],
        out_specs=[pl.BlockSpec((1, cb, n1, LANES), lambda o, c: (o, c, 0, 0))] * 2,
        out_shape=[jax.ShapeDtypeStruct((HYENA_ORDER, B_WIDTH, n1, LANES), F32)] * 2,
        compiler_params=_cparams(("arbitrary", "arbitrary")),
        name="hyena_lat_spectrum",
    )(filt5, skip, cs["w1b"], cs["twr"], cs["twi"], cs["w2"])


def _hyena_lat_kernel(prm_ref, v_ref, x1_ref, x2_ref, bg_ref, hr_ref, hi_ref,
                      w1b_ref, w1ib_ref, twr_ref, twi_ref, w2_ref, w2c_ref, o_ref,
                      vs_ref, x1s_ref, x2s_ref):
    nb, cb, h1, _ = v_ref.shape
    half = nb // 2
    s = cb * half
    n1 = twr_ref.shape[0]
    c0 = pl.program_id(0) * cb
    row = lax.broadcasted_iota(jnp.int32, (nb, h1, LANES), 1)
    lane = lax.broadcasted_iota(jnp.int32, (nb, h1, LANES), 2)

    def short_conv(u, part, c):
        r = pltpu.roll(u, 1, 2)
        prev = jnp.where(lane == 0, pltpu.roll(r, 1, 1), r)
        prev = jnp.where((lane == 0) & (row == 0), 0.0, prev)
        r = pltpu.roll(u, LANES - 1, 2)
        nxt = jnp.where(lane == LANES - 1, pltpu.roll(r, h1 - 1, 1), r)
        nxt = jnp.where((lane == LANES - 1) & (row == h1 - 1), 0.0, nxt)
        ch = part * B_WIDTH + c0 + c
        return prm_ref[3, ch] + prev * prm_ref[0, ch] + u * prm_ref[1, ch] + nxt * prm_ref[2, ch]

    def stack_into(dst_ref, u, c):
        dst_ref[c * half:(c + 1) * half, 0:h1, :] = u[:half]
        dst_ref[c * half:(c + 1) * half, h1:2 * h1, :] = u[half:]

    for c in range(cb):
        stack_into(vs_ref, short_conv(v_ref[:, c], 0, c), c)
        stack_into(x1s_ref, short_conv(x1_ref[:, c], 1, c), c)
        bg = bg_ref[:, c]
        stack_into(x2s_ref, short_conv(x2_ref[:, c], 2, c) * (bg * _sigmoid(bg)), c)

    twr, twi = twr_ref[...], twi_ref[...]

    def conv(xs, o):
        sp = _stage_fwd(xs, w1b_ref[...], twr, twi, w2_ref[...]).reshape(cb, half, n1, 2 * LANES)
        br, bi = sp[..., :LANES], sp[..., LANES:]
        hr = hr_ref[o][:, None]
        hi = hi_ref[o][:, None]
        cm = jnp.concatenate([br * hr - bi * hi, br * hi + bi * hr], axis=-1).reshape(s * n1, 2 * LANES)
        return _stage_inv(cm, w1ib_ref[...], twr, twi, w2c_ref[...], s)

    z = x1s_ref[...] * conv(vs_ref[...], 0)
    y = x2s_ref[...] * conv(z, 1)
    for c in range(cb):
        o_ref[0:half, c] = y[c * half:(c + 1) * half, 0:h1]
        o_ref[half:nb, c] = y[c * half:(c + 1) * half, h1:2 * h1]


def _hyena_lat(t_arr, prm, hr, hi, *, nb, L):
    cb = 8
    half = nb // 2
    s = cb * half
    cs = _two_stage_consts(L, s)
    n1, h1 = cs["n1"], cs["h1"]
    ncb = B_WIDTH // cb
    t5 = t_arr.reshape(nb, T_W, h1, LANES)

    def part(k):
        return pl.BlockSpec((nb, cb, h1, LANES), lambda c: (0, c + k * ncb, 0, 0))

    const = lambda a: pl.BlockSpec(a.shape, lambda c: (0,) * a.ndim)
    out = pl.pallas_call(
        _hyena_lat_kernel,
        grid=(ncb,),
        in_specs=[
            pl.BlockSpec(memory_space=pltpu.SMEM),
            part(0), part(1), part(2), part(3),
            pl.BlockSpec((HYENA_ORDER, cb, n1, LANES), lambda c: (0, c, 0, 0)),
            pl.BlockSpec((HYENA_ORDER, cb, n1, LANES), lambda c: (0, c, 0, 0)),
            const(cs["w1b"]), const(cs["w1ib"]), const(cs["twr"]), const(cs["twi"]),
            const(cs["w2"]), const(cs["w2c"]),
        ],
        out_specs=pl.BlockSpec((nb, cb, h1, LANES), lambda c: (0, c, 0, 0)),
        out_shape=jax.ShapeDtypeStruct((nb, B_WIDTH, h1, LANES), F32),
        scratch_shapes=[pltpu.VMEM((s, 2 * h1, LANES), F32)] * 3,
        compiler_params=_cparams(("arbitrary",)),
        name="hyena_lat",
    )(prm.T, t5, t5, t5, t5, hr, hi,
      cs["w1b"], cs["w1ib"], cs["twr"], cs["twi"], cs["w2"], cs["w2c"])
    return out.reshape(nb, B_WIDTH, L)


def _merge_kernel(x_ref, ya_ref, ybt_ref, yc_ref, mg_ref, gate_ref, wa_ref, wb_ref, wc_ref, wo_ref,
                  fw_ref, o_ref, *, final):
    mg = mg_ref[...]
    yb = ybt_ref[0].T.astype(BF16)
    m = (_sigmoid(mg[:, :D_MODEL].astype(F32)) * jnp.dot(ya_ref[...], wa_ref[...], preferred_element_type=F32)
         + _sigmoid(mg[:, D_MODEL:2 * D_MODEL].astype(F32)) * jnp.dot(yb, wb_ref[...], preferred_element_type=F32)
         + _sigmoid(mg[:, 2 * D_MODEL:].astype(F32)) * jnp.dot(yc_ref[...], wc_ref[...], preferred_element_type=F32))
    out = jnp.dot(m.astype(BF16), wo_ref[...], preferred_element_type=F32)
    xn = x_ref[...] + gate_ref[0] * out
    if final:
        var = jnp.mean(xn * xn, axis=-1, keepdims=True)
        xn = xn * lax.rsqrt(var + EPS) * fw_ref[...]
    o_ref[...] = xn


def _merge(x2d, ya, ybt, yc, mg, gate, wa, wb, wc, wo, fw, *, nb, lb, final):
    rows = nb * lb
    tm = 256
    bpb = lb // tm
    per_mod = gate.shape[0] > 1
    kernel = functools.partial(_merge_kernel, final=final)
    const = lambda a: pl.BlockSpec(a.shape, lambda i: (0,) * a.ndim, pipeline_mode=pl.Buffered(1))
    return pl.pallas_call(
        kernel,
        grid=(rows // tm,),
        in_specs=[
            pl.BlockSpec((tm, D_MODEL), lambda i: (i, 0)),
            pl.BlockSpec((tm, A_WIDTH), lambda i: (i, 0)),
            pl.BlockSpec((1, B_WIDTH, tm), lambda i: (i // bpb, 0, i % bpb)),
            pl.BlockSpec((tm, C_WIDTH), lambda i: (i, 0)),
            pl.BlockSpec((tm, MG_W), lambda i: (i, 0)),
            pl.BlockSpec((1, 1, D_MODEL), lambda i: ((i // bpb) if per_mod else 0, 0, 0)),
            const(wa), const(wb), const(wc), const(wo),
            pl.BlockSpec((1, D_MODEL), lambda i: (0, 0)),
        ],
        out_specs=pl.BlockSpec((tm, D_MODEL), lambda i: (i, 0)),
        out_shape=jax.ShapeDtypeStruct((rows, D_MODEL), F32),
        compiler_params=_cparams(("arbitrary",)),
        name="merge_final" if final else "merge",
    )(x2d, ya.reshape(rows, A_WIDTH), ybt, yc.reshape(rows, C_WIDTH), mg, gate, wa, wb, wc, wo,
      fw.reshape(1, D_MODEL))


def _rope_tables(L):
    t = np.arange(L)
    row = (t // GRID_W).astype(np.float32)
    col = (t % GRID_W).astype(np.float32)
    nf = HEAD_DIM // 4
    inv = np.power(np.float32(ROPE_BASE), -np.arange(nf, dtype=np.float32) / nf).astype(np.float32)
    ang = np.concatenate([row[:, None] * inv[None], col[:, None] * inv[None]], axis=-1)
    cos, sin = np.cos(ang), np.sin(ang)
    reps = LANES // HEAD_DIM
    cos_t = np.tile(np.concatenate([cos, cos], axis=-1), (1, reps))
    sin_t = np.tile(np.concatenate([-sin, sin], axis=-1), (1, reps))
    return jnp.asarray(cos_t, F32), jnp.asarray(sin_t, F32)


def _regroup_w_in(w):
    aq_ak = w[:, 0:QK_W]
    av = w[:, 1024:1280]
    ag = w[:, 1280:2048]
    bu_bg = w[:, 2048:4096]
    cc = w[:, 4096:7168]
    mg = w[:, 7168:]
    w_nat = jnp.concatenate([aq_ak, ag, av, cc, mg], axis=1).astype(BF16)
    return w_nat, bu_bg.T.astype(BF16)


def kernel(x_prompt, x_sample, c, cache_a_k, cache_a_v, cache_c_k, cache_c_v, c_ctx, norm_w, w_ada, b_ada, w_in, a_sink, hy_conv_w, hy_conv_b, hy_w1, hy_b1, hy_w2, hy_b2, hy_freq, hy_w3, hy_decay, hy_skip, c_rpb, w_up_a, w_up_b, w_up_c, w_out, final_norm_w):
    nbc, S, _ = x_prompt.shape
    nbl, L, _ = x_sample.shape
    depth = w_in.shape[0]
    n_ctx = cache_a_k.shape[2]
    assert nbc % 2 == 0 and nbl % 2 == 0 and L % (GRID_W * LANES // 2) == 0

    pad = (-(nbl + 1)) % 8
    cond = jnp.concatenate([c, c_ctx[None], jnp.zeros((pad, D_MODEL), F32)], axis=0)
    mod = _adaln(cond, w_ada, b_ada)

    cos_t, sin_t = _rope_tables(L)
    tm_ctx = 1024 if (nbc * S) % 1024 == 0 else S
    zeros_t = jnp.zeros((tm_ctx, LANES), F32)
    wf_c, wi_c = _dense_dft_consts(S)
    ca_k = cache_a_k.reshape(nbl, depth, n_ctx, A_KV_WIDTH)
    ca_v = cache_a_v.reshape(nbl, depth, n_ctx, A_KV_WIDTH)
    cc_k = cache_c_k.reshape(nbl, depth, n_ctx, C_WIDTH)
    cc_v = cache_c_v.reshape(nbl, depth, n_ctx, C_WIDTH)

    xp = x_prompt.reshape(nbc * S, D_MODEL)
    xs = x_sample.reshape(nbl * L, D_MODEL)
    aks, avs, cks, cvs = [], [], [], []
    for l in range(depth):
        final = l == depth - 1
        w_nat, w_t = _regroup_w_in(w_in[l])
        wa, wb, wc, wo = (w.astype(BF16) for w in (w_up_a[l], w_up_b[l], w_up_c[l], w_out[l]))
        prm = _short_conv_params(hy_conv_w[l], hy_conv_b[l])
        filt_args = (hy_w1[l], hy_b1[l], hy_w2[l], hy_b2[l], hy_freq[l], hy_w3[l], hy_decay[l])
        t2 = _nbr_bias_table(c_rpb[l])

        def mod_parts(rows):
            m = mod[l, rows][:, None, :]
            return 1.0 + m[..., D_MODEL:2 * D_MODEL], m[..., :D_MODEL], m[..., 2 * D_MODEL:]

        sc, sh, gate = mod_parts(slice(nbl, nbl + 1))
        qk, gv, cc, mg, t_arr = _inproj(xp, sc, sh, norm_w[l], w_nat, w_t, zeros_t, zeros_t,
                                        nb=nbc, lb=S, tm=tm_ctx, rope=False, kv_dtype=F32)
        aks.append(qk[:, A_WIDTH:].reshape(nbc, S, A_KV_HEADS, HEAD_DIM))
        avs.append(gv[:, A_WIDTH:].reshape(nbc, S, A_KV_HEADS, HEAD_DIM))
        cks.append(cc[:, C_WIDTH:2 * C_WIDTH].reshape(nbc, S, C_HEADS, HEAD_DIM))
        cvs.append(cc[:, 2 * C_WIDTH:3 * C_WIDTH].reshape(nbc, S, C_HEADS, HEAD_DIM))
        ya, yc = _ctx_attn(a_sink[l], qk, gv, cc, nb=nbc, lb=S)
        hr, hi = _ctx_spectrum(_hyena_filters(S, *filt_args), hy_skip[l], wf_c, S)
        ybt = _hyena_ctx(t_arr, prm, hr, hi, wf_c, wi_c, nb=nbc, S=S)
        xp = _merge(xp, ya, ybt, yc, mg, gate, wa, wb, wc, wo, final_norm_w, nb=nbc, lb=S, final=final)

        sc, sh, gate = mod_parts(slice(0, nbl))
        qk, gv, cc, mg, t_arr = _inproj(xs, sc, sh, norm_w[l], w_nat, w_t, cos_t, sin_t,
                                        nb=nbl, lb=L, tm=1024, rope=True, kv_dtype=BF16)
        ya = _lat_win_attn(a_sink[l], qk, gv, ca_k, ca_v, l, nb=nbl, lb=L)
        yc = _lat_nbr_attn(cc, cc_k, cc_v, l, t2, nb=nbl, lb=L)
        hr, hi = _lat_spectrum(_hyena_filters(L, *filt_args), hy_skip[l], L)
        ybt = _hyena_lat(t_arr, prm, hr, hi, nb=nbl, L=L)
        xs = _merge(xs, ya, ybt, yc, mg, gate, wa, wb, wc, wo, final_norm_w, nb=nbl, lb=L, final=final)

    y_prompt = xp.reshape(nbc, S, D_MODEL)
    y_sample = xs.reshape(nbl, L, D_MODEL)
    return (y_prompt, y_sample, jnp.stack(aks, axis=1), jnp.stack(avs, axis=1),
            jnp.stack(cks, axis=1), jnp.stack(cvs, axis=1))
```

```python
import functools

import numpy as np
import jax
import jax.numpy as jnp
from jax import lax
from jax.experimental import pallas as pl
from jax.experimental.pallas import tpu as pltpu

F32 = jnp.float32
BF16 = jnp.bfloat16
HIGHEST = lax.Precision.HIGHEST

D_MODEL = 2048
HEAD_DIM = 64
A_HEADS = 12
A_KV_HEADS = 4
A_GROUP = A_HEADS // A_KV_HEADS
A_WIDTH = A_HEADS * HEAD_DIM
A_KV_WIDTH = A_KV_HEADS * HEAD_DIM
A_WINDOW = 128
B_WIDTH = 512
HYENA_ORDER = 2
HYENA_BANDS = 16
C_HEADS = 12
C_WIDTH = C_HEADS * HEAD_DIM
GRID_W = 64
NA_ROWS = 8
NA_COLS = 16
ROPE_BASE = 10000.0
EPS = 1e-6
NEG_INF = -1e30
ATTN_SCALE = HEAD_DIM ** -0.5

LANES = 128
VMEM_LIMIT = 56 * 1024 * 1024

IN_AQ, IN_AK, IN_AV, IN_AG = 0, 768, 1024, 1280
IN_BU, IN_CQ, IN_MG = 2048, 4096, 7168
PROJ_TN = 1024
MG_W = 3 * D_MODEL
U_MG = 0
U_CQ = U_MG + MG_W
U_CK, U_CV, U_CG = U_CQ + C_WIDTH, U_CQ + 2 * C_WIDTH, U_CQ + 3 * C_WIDTH
U_AQ = U_CG + C_WIDTH
U_AG = U_AQ + A_WIDTH
U_AK = U_AG + A_WIDTH
U_AV = U_AK + A_KV_WIDTH
NAT_W = U_AV + A_KV_WIDTH
T_W = 4 * B_WIDTH

A_PERM = tuple(2 * A_GROUP * m + A_GROUP * half + j
               for m in range(A_KV_HEADS // 2) for j in range(A_GROUP) for half in range(2))


def _cparams(sem):
    return pltpu.CompilerParams(dimension_semantics=sem, vmem_limit_bytes=VMEM_LIMIT)


def _sigmoid(x):
    return 1.0 / (1.0 + jnp.exp(-x))


def _lane_chunks(off, width):
    blk, rem = divmod(off, PROJ_TN)
    assert rem % LANES == 0 and width % LANES == 0 and rem + width <= PROJ_TN
    return blk, rem // LANES, (rem + width) // LANES


def _adaln_kernel(cond_ref, w_ref, b_ref, o_ref):
    c = cond_ref[...]
    s = (c * _sigmoid(c)).astype(BF16)
    acc = jnp.dot(s, w_ref[0].astype(BF16), preferred_element_type=F32)
    o_ref[0] = acc + b_ref[0]


def _adaln(cond, w_ada, b_ada):
    depth = w_ada.shape[0]
    rows = cond.shape[0]
    tn = 1024
    return pl.pallas_call(
        _adaln_kernel,
        grid=(depth, 3 * D_MODEL // tn),
        in_specs=[
            pl.BlockSpec((rows, D_MODEL), lambda l, j: (0, 0)),
            pl.BlockSpec((1, D_MODEL, tn), lambda l, j: (l, 0, j)),
            pl.BlockSpec((1, 1, tn), lambda l, j: (l, 0, j)),
        ],
        out_specs=pl.BlockSpec((1, rows, tn), lambda l, j: (l, 0, j)),
        out_shape=jax.ShapeDtypeStruct((depth, rows, 3 * D_MODEL), F32),
        compiler_params=_cparams(("arbitrary", "arbitrary")),
        name="adaln",
    )(cond, w_ada, b_ada.reshape(depth, 1, 3 * D_MODEL))


def _inproj_kernel(x_ref, sc_ref, sh_ref, nw_ref, w_ref, wt_ref, cos_ref, sin_ref,
                   u_ref, t_ref, h_ref, *, rope, tm, lb):
    j = pl.program_id(1)
    tn = PROJ_TN
    n_nat = NAT_W // tn

    @pl.when(j == 0)
    def _():
        x = x_ref[...]
        var = jnp.mean(x * x, axis=-1, keepdims=True)
        y = x * lax.rsqrt(var + EPS) * nw_ref[...]
        h_ref[...] = (y * sc_ref[0] + sh_ref[0]).astype(BF16)

    def rotate(a):
        lane = lax.broadcasted_iota(jnp.int32, (tm, LANES), 1)
        first = (lane % HEAD_DIM) < (HEAD_DIM // 2)
        partner = jnp.where(first, pltpu.roll(a, LANES - HEAD_DIM // 2, 1), pltpu.roll(a, HEAD_DIM // 2, 1))
        return a * cos_ref[...] + partner * sin_ref[...]

    def nat_step(chunk_ops):
        acc = jnp.dot(h_ref[...], w_ref[...], preferred_element_type=F32)
        if not chunk_ops:
            u_ref[...] = acc.astype(u_ref.dtype)
            return
        for ci in range(tn // LANES):
            a = acc[:, ci * LANES:(ci + 1) * LANES]
            for op in chunk_ops.get(ci, ()):
                a = rotate(a) if op == "rope" else a * ATTN_SCALE
            u_ref[:, ci * LANES:(ci + 1) * LANES] = a.astype(u_ref.dtype)

    special = {}
    for off, width, ops in ((U_CQ, C_WIDTH, ("scale",)),
                            (U_AQ, A_WIDTH, ("rope", "scale") if rope else ("scale",)),
                            (U_AK, A_KV_WIDTH, ("rope",) if rope else ())):
        if ops:
            blk, c0, c1 = _lane_chunks(off, width)
            special.setdefault(blk, {}).update({ci: ops for ci in range(c0, c1)})

    plain = j < n_nat
    for blk, chunk_ops in special.items():
        plain = plain & (j != blk)
        pl.when(j == blk)(functools.partial(nat_step, chunk_ops))
    pl.when(plain)(functools.partial(nat_step, None))

    @pl.when(j >= n_nat)
    def _():
        acc_t = lax.dot_general(wt_ref[...], h_ref[...], (((1,), (1,)), ((), ())),
                                preferred_element_type=F32)
        if tm <= lb:
            t_ref[0] = acc_t
        else:
            for k in range(tm // lb):
                t_ref[k] = acc_t[:, k * lb:(k + 1) * lb]


def _inproj(x2d, sc, sh, norm_w, w_nat, w_t, cos_t, sin_t, *, nb, lb, tm, rope, u_dtype):
    rows = nb * lb
    tn = PROJ_TN
    n_nat, n_t = NAT_W // tn, T_W // tn
    per_mod = sc.shape[0] > 1
    bpb = max(lb // tm, 1)

    def mod_map(i, j):
        return ((i // bpb) if per_mod else 0, 0, 0)

    if tm <= lb:
        t_spec = pl.BlockSpec((1, tn, tm), lambda i, j: (i // bpb, jnp.clip(j - n_nat, 0, n_t - 1), i % bpb))
    else:
        t_spec = pl.BlockSpec((tm // lb, tn, lb), lambda i, j: (i, jnp.clip(j - n_nat, 0, n_t - 1), 0))

    kernel = functools.partial(_inproj_kernel, rope=rope, tm=tm, lb=lb)
    return pl.pallas_call(
        kernel,
        grid=(rows // tm, n_nat + n_t),
        in_specs=[
            pl.BlockSpec((tm, D_MODEL), lambda i, j: (i, 0), pipeline_mode=pl.Buffered(1)),
            pl.BlockSpec((1, 1, D_MODEL), mod_map),
            pl.BlockSpec((1, 1, D_MODEL), mod_map),
            pl.BlockSpec((1, D_MODEL), lambda i, j: (0, 0)),
            pl.BlockSpec((D_MODEL, tn), lambda i, j: (0, jnp.minimum(j, n_nat - 1))),
            pl.BlockSpec((tn, D_MODEL), lambda i, j: (jnp.maximum(j - n_nat, 0), 0)),
            pl.BlockSpec((tm, LANES), lambda i, j: (i % bpb, 0)),
            pl.BlockSpec((tm, LANES), lambda i, j: (i % bpb, 0)),
        ],
        out_specs=[
            pl.BlockSpec((tm, tn), lambda i, j: (i, jnp.minimum(j, n_nat - 1))),
            t_spec,
        ],
        out_shape=[
            jax.ShapeDtypeStruct((rows, NAT_W), u_dtype),
            jax.ShapeDtypeStruct((nb, T_W, lb), F32),
        ],
        scratch_shapes=[pltpu.VMEM((tm, D_MODEL), BF16)],
        compiler_params=_cparams(("arbitrary", "arbitrary")),
        name="inproj_rope" if rope else "inproj",
    )(x2d, sc, sh, norm_w.reshape(1, D_MODEL), w_nat, w_t, cos_t, sin_t)


def _softmax_pv(s, v, sink_col):
    m = jnp.max(s, axis=-1, keepdims=True)
    if sink_col is not None:
        m = jnp.maximum(m, sink_col)
    e = jnp.exp(s - m)
    den = jnp.sum(e, axis=-1, keepdims=True)
    if sink_col is not None:
        den = den + jnp.exp(sink_col - m)
    o = jnp.dot(e.astype(BF16), v, preferred_element_type=F32)
    return o * (1.0 / den)


def _sink_column(sink_ref, heads, rows_per_head):
    n = len(heads)
    row_head = lax.broadcasted_iota(jnp.int32, (n * rows_per_head, 1), 0) // rows_per_head
    col = jnp.full((n * rows_per_head, 1), sink_ref[heads[-1]], F32)
    for idx in range(n - 1):
        col = jnp.where(row_head == idx, sink_ref[heads[idx]], col)
    return col


def _ctx_attn_kernel(sink_ref, cc_ref, a0_ref, a1_ref, ya_ref, yc_ref):
    cc = cc_ref[0]
    a = jnp.concatenate([a0_ref[0], a1_ref[0]], axis=1)
    s_len = cc.shape[0]
    nt = (((1,), (1,)), ((), ()))
    o_aq, o_ag, o_ak, o_av = 0, U_AG - U_AQ, U_AK - U_AQ, U_AV - U_AQ

    pieces = [None] * A_HEADS
    for g in range(A_KV_HEADS):
        heads = [A_GROUP * g + hh for hh in range(A_GROUP)]
        pos = [A_PERM.index(h) for h in heads]
        q = jnp.concatenate([a[:, o_aq + p * HEAD_DIM:o_aq + (p + 1) * HEAD_DIM] for p in pos],
                            axis=0).astype(BF16)
        k = a[:, o_ak + g * HEAD_DIM:o_ak + (g + 1) * HEAD_DIM].astype(BF16)
        v = a[:, o_av + g * HEAD_DIM:o_av + (g + 1) * HEAD_DIM].astype(BF16)
        s = lax.dot_general(q, k, nt, preferred_element_type=F32)
        o = _softmax_pv(s, v, _sink_column(sink_ref, heads, s_len))
        for hh, p in enumerate(pos):
            pieces[p] = o[hh * s_len:(hh + 1) * s_len]
    ya = jnp.concatenate(pieces, axis=1)
    ag = a[:, o_ag:o_ag + A_WIDTH]
    ya_ref[0] = (ya * (ag * _sigmoid(ag))).astype(BF16)

    pieces = []
    for h in range(C_HEADS):
        q = cc[:, h * HEAD_DIM:(h + 1) * HEAD_DIM].astype(BF16)
        k = cc[:, C_WIDTH + h * HEAD_DIM:C_WIDTH + (h + 1) * HEAD_DIM].astype(BF16)
        v = cc[:, 2 * C_WIDTH + h * HEAD_DIM:2 * C_WIDTH + (h + 1) * HEAD_DIM].astype(BF16)
        s = lax.dot_general(q, k, nt, preferred_element_type=F32)
        pieces.append(_softmax_pv(s, v, None))
    yc = jnp.concatenate(pieces, axis=1)
    cg = cc[:, 3 * C_WIDTH:]
    yc_ref[0] = (yc * (cg * _sigmoid(cg))).astype(BF16)


def _ctx_attn(sink, u, *, nb, lb):
    u3 = u.reshape(nb, lb, NAT_W)
    cc_w = 4 * C_WIDTH
    return pl.pallas_call(
        _ctx_attn_kernel,
        grid=(nb,),
        in_specs=[
            pl.BlockSpec(memory_space=pltpu.SMEM),
            pl.BlockSpec((1, lb, cc_w), lambda b: (b, 0, U_CQ // cc_w)),
            pl.BlockSpec((1, lb, PROJ_TN), lambda b: (b, 0, U_AQ // PROJ_TN)),
            pl.BlockSpec((1, lb, PROJ_TN), lambda b: (b, 0, U_AQ // PROJ_TN + 1)),
        ],
        out_specs=[
            pl.BlockSpec((1, lb, A_WIDTH), lambda b: (b, 0, 0)),
            pl.BlockSpec((1, lb, C_WIDTH), lambda b: (b, 0, 0)),
        ],
        out_shape=[
            jax.ShapeDtypeStruct((nb, lb, A_WIDTH), BF16),
            jax.ShapeDtypeStruct((nb, lb, C_WIDTH), BF16),
        ],
        compiler_params=_cparams(("arbitrary",)),
        name="ctx_attn",
    )(sink, u3, u3, u3)


def _transpose_keys(k_ref, kt_scr, ck_ref, ckt_scr, cv_ref, cv_scr, chunk):
    n_tok = k_ref.shape[1]

    def body(c, carry):
        start = pl.multiple_of(c * chunk, chunk)
        blk = k_ref[0, pl.ds(start, chunk), :].astype(F32)
        kt_scr[:, pl.ds(start, chunk)] = blk.T.astype(BF16)
        return carry

    lax.fori_loop(0, n_tok // chunk, body, 0)
    ckt_scr[...] = ck_ref[0, 0].T.astype(BF16)
    cv_scr[...] = cv_ref[0, 0].astype(BF16)


def _pair_softmax_pv(q_half, kt_loc, bias_fn, kt_ctx, v_loc, v_ctx, sink_col):
    s_loc = bias_fn(jnp.dot(q_half, kt_loc, preferred_element_type=F32))
    s_ctx = jnp.dot(q_half, kt_ctx, preferred_element_type=F32)
    m = jnp.maximum(jnp.max(s_loc, axis=-1, keepdims=True), jnp.max(s_ctx, axis=-1, keepdims=True))
    if sink_col is not None:
        m = jnp.maximum(m, sink_col)
    e_loc = jnp.exp(s_loc - m)
    e_ctx = jnp.exp(s_ctx - m)
    den = jnp.sum(e_loc, axis=-1, keepdims=True) + jnp.sum(e_ctx, axis=-1, keepdims=True)
    if sink_col is not None:
        den = den + jnp.exp(sink_col - m)
    o = (jnp.dot(e_loc.astype(BF16), v_loc, preferred_element_type=F32)
         + jnp.dot(e_ctx.astype(BF16), v_ctx, preferred_element_type=F32))
    return o * (1.0 / den)


def _lat_win_kernel(sink_ref, q_ref, k_ref, v_ref, ag_ref, ck_ref, cv_ref, o_ref,
                    kt_scr, ckt_scr, cv_scr):
    i = pl.program_id(1)
    nblk = pl.num_programs(1)
    blk = A_WINDOW
    span = 3 * blk

    @pl.when(i == 0)
    def _():
        _transpose_keys(k_ref, kt_scr, ck_ref, ckt_scr, cv_ref, cv_scr, blk)

    start = pl.multiple_of(jnp.clip(i - 1, 0, nblk - 3) * blk, blk)
    rows = A_GROUP * blk
    q_pos = i * blk + lax.broadcasted_iota(jnp.int32, (rows, span), 0) % blk
    k_pos = start + lax.broadcasted_iota(jnp.int32, (rows, span), 1)
    valid = jnp.abs(q_pos - k_pos) <= A_WINDOW
    mask = lambda s: jnp.where(valid, s, NEG_INF)

    q = q_ref[0]
    lane = lax.broadcasted_iota(jnp.int32, (blk, LANES), 1)
    low = lane < HEAD_DIM
    outs = []
    for m in range(A_KV_HEADS // 2):
        rows_m = slice(m * LANES, (m + 1) * LANES)
        kt_loc = kt_scr[rows_m, pl.ds(start, span)]
        kt_ctx = ckt_scr[rows_m, :]
        v_loc = v_ref[0, pl.ds(start, span), rows_m]
        v_ctx = cv_scr[:, rows_m]
        pairs = [q[:, (A_GROUP * m + jj) * LANES:(A_GROUP * m + jj + 1) * LANES] for jj in range(A_GROUP)]
        halves = []
        for half in range(2):
            keep = low if half == 0 else ~low
            q_half = jnp.concatenate([jnp.where(keep, p, jnp.zeros_like(p)) for p in pairs], axis=0)
            heads = [A_PERM[2 * (A_GROUP * m + jj) + half] for jj in range(A_GROUP)]
            halves.append(_pair_softmax_pv(q_half, kt_loc, mask, kt_ctx, v_loc, v_ctx,
                                           _sink_column(sink_ref, heads, blk)))
        for jj in range(A_GROUP):
            outs.append(jnp.where(low, halves[0][jj * blk:(jj + 1) * blk], halves[1][jj * blk:(jj + 1) * blk]))
    ya = jnp.concatenate(outs, axis=1)
    ag = ag_ref[0].astype(F32)
    o_ref[0] = (ya * (ag * _sigmoid(ag))).astype(BF16)


def _lat_win_attn(sink, u, cache_k, cache_v, layer, *, nb, lb):
    blk = A_WINDOW
    nblk = lb // blk
    assert nblk >= 3
    n_ctx = cache_k.shape[2]
    u3 = u.reshape(nb, lb, NAT_W)
    ctx_spec = pl.BlockSpec((1, 1, n_ctx, A_KV_WIDTH), lambda b, i: (b, layer, 0, 0))
    return pl.pallas_call(
        _lat_win_kernel,
        grid=(nb, nblk),
        in_specs=[
            pl.BlockSpec(memory_space=pltpu.SMEM),
            pl.BlockSpec((1, blk, A_WIDTH), lambda b, i: (b, i, U_AQ // A_WIDTH)),
            pl.BlockSpec((1, lb, A_KV_WIDTH), lambda b, i: (b, 0, U_AK // A_KV_WIDTH)),
            pl.BlockSpec((1, lb, A_KV_WIDTH), lambda b, i: (b, 0, U_AV // A_KV_WIDTH)),
            pl.BlockSpec((1, blk, A_WIDTH), lambda b, i: (b, i, U_AG // A_WIDTH)),
            ctx_spec, ctx_spec,
        ],
        out_specs=pl.BlockSpec((1, blk, A_WIDTH), lambda b, i: (b, i, 0)),
        out_shape=jax.ShapeDtypeStruct((nb, lb, A_WIDTH), BF16),
        scratch_shapes=[pltpu.VMEM((A_KV_WIDTH, lb), BF16),
                        pltpu.VMEM((A_KV_WIDTH, n_ctx), BF16),
                        pltpu.VMEM((n_ctx, A_KV_WIDTH), BF16)],
        compiler_params=_cparams(("arbitrary", "arbitrary")),
        name="lat_win_attn",
    )(sink, u3, u3, u3, u3, cache_k, cache_v)


NBR_QROWS = 2
NBR_BAND = NA_ROWS + NBR_QROWS


def _lat_nbr_kernel(q_ref, k_ref, v_ref, cg_ref, ck_ref, cv_ref, t2_ref, o_ref,
                    kt_scr, ckt_scr, cv_scr, *, grid_rows):
    i = pl.program_id(1)

    @pl.when(i == 0)
    def _():
        _transpose_keys(k_ref, kt_scr, ck_ref, ckt_scr, cv_ref, cv_scr, LANES)

    r0 = NBR_QROWS * i
    nq = NBR_QROWS * GRID_W
    nk = NBR_BAND * GRID_W
    bs = jnp.clip(r0 - NA_ROWS // 2, 0, grid_rows - NBR_BAND)
    start = pl.multiple_of(bs * GRID_W, LANES)

    q_row = r0 + lax.broadcasted_iota(jnp.int32, (nq, nk), 0) // GRID_W
    rstart = jnp.clip(q_row - NA_ROWS // 2, 0, grid_rows - NA_ROWS)
    k_row = bs + lax.broadcasted_iota(jnp.int32, (nq, nk), 1) // GRID_W
    valid = (k_row >= rstart) & (k_row < rstart + NA_ROWS)

    q = q_ref[0]
    lane = lax.broadcasted_iota(jnp.int32, (nq, LANES), 1)
    low = lane < HEAD_DIM
    outs = []
    for m in range(C_HEADS // 2):
        rows_m = slice(m * LANES, (m + 1) * LANES)
        kt_loc = kt_scr[rows_m, pl.ds(start, nk)]
        kt_ctx = ckt_scr[rows_m, :]
        v_loc = v_ref[0, pl.ds(start, nk), rows_m]
        v_ctx = cv_scr[:, rows_m]
        qp = q[:, rows_m]
        halves = []
        for half in range(2):
            h = 2 * m + half
            bias_rows = []
            for a in range(NBR_QROWS):
                tiles = []
                for p in range(NBR_BAND // 2):
                    d = (bs + 2 * p) - (r0 + a) + (NA_ROWS - 1)
                    idx = jnp.clip(d, -1, 2 * NA_ROWS - 2) + 1
                    tiles.append(t2_ref[h, idx])
                bias_rows.append(jnp.concatenate(tiles, axis=1))
            bias = jnp.concatenate(bias_rows, axis=0)
            q_half = jnp.where(low if half == 0 else ~low, qp, jnp.zeros_like(qp))
            halves.append(_pair_softmax_pv(q_half, kt_loc, lambda s: jnp.where(valid, s + bias, NEG_INF),
                                           kt_ctx, v_loc, v_ctx, None))
        outs.append(jnp.where(low, halves[0], halves[1]))
    yc = jnp.concatenate(outs, axis=1)
    cg = cg_ref[0].astype(F32)
    o_ref[0] = (yc * (cg * _sigmoid(cg))).astype(BF16)


def _nbr_bias_table(rpb):
    w = np.arange(GRID_W)
    cstart = np.clip(w - NA_COLS // 2, 0, GRID_W - NA_COLS)
    j = np.arange(GRID_W)
    in_win = (j[None, :] >= cstart[:, None]) & (j[None, :] < cstart[:, None] + NA_COLS)
    dcol = np.clip(j[None, :] - w[:, None] + NA_COLS - 1, 0, 2 * NA_COLS - 2)
    t = rpb.astype(F32)[:, :, dcol]
    t = jnp.where(jnp.asarray(in_win)[None, None], t, NEG_INF)
    pad = jnp.full((rpb.shape[0], 1, GRID_W, GRID_W), NEG_INF, F32)
    t = jnp.concatenate([pad, t, pad], axis=1)
    return jnp.concatenate([t[:, :-1], t[:, 1:]], axis=-1)


def _lat_nbr_attn(u, cache_k, cache_v, layer, t2, *, nb, lb):
    grid_rows = lb // GRID_W
    assert grid_rows >= NBR_BAND and grid_rows % NBR_QROWS == 0 and NBR_QROWS * GRID_W == LANES
    nq = NBR_QROWS * GRID_W
    n_ctx = cache_k.shape[2]
    u3 = u.reshape(nb, lb, NAT_W)
    ctx_spec = pl.BlockSpec((1, 1, n_ctx, C_WIDTH), lambda b, i: (b, layer, 0, 0))
    kernel = functools.partial(_lat_nbr_kernel, grid_rows=grid_rows)
    once = pl.Buffered(1)
    return pl.pallas_call(
        kernel,
        grid=(nb, grid_rows // NBR_QROWS),
        in_specs=[
            pl.BlockSpec((1, nq, C_WIDTH), lambda b, i: (b, i, U_CQ // C_WIDTH)),
            pl.BlockSpec((1, lb, C_WIDTH), lambda b, i: (b, 0, U_CK // C_WIDTH), pipeline_mode=once),
            pl.BlockSpec((1, lb, C_WIDTH), lambda b, i: (b, 0, U_CV // C_WIDTH)),
            pl.BlockSpec((1, nq, C_WIDTH), lambda b, i: (b, i, U_CG // C_WIDTH)),
            ctx_spec, ctx_spec,
            pl.BlockSpec(t2.shape, lambda b, i: (0, 0, 0, 0), pipeline_mode=once),
        ],
        out_specs=pl.BlockSpec((1, nq, C_WIDTH), lambda b, i: (b, i, 0)),
        out_shape=jax.ShapeDtypeStruct((nb, lb, C_WIDTH), BF16),
        scratch_shapes=[pltpu.VMEM((C_WIDTH, lb), BF16),
                        pltpu.VMEM((C_WIDTH, n_ctx), BF16),
                        pltpu.VMEM((n_ctx, C_WIDTH), BF16)],
        compiler_params=_cparams(("arbitrary", "arbitrary")),
        name="lat_nbr_attn",
    )(u3, u3, u3, u3, cache_k, cache_v, t2)


def _hyena_filters(L, hy_w1, hy_b1, hy_w2, hy_b2, hy_freq, hy_w3, hy_decay):
    t = jnp.arange(L, dtype=F32) / L
    bands = 2.0 * jnp.pi * jnp.arange(1, HYENA_BANDS + 1, dtype=F32)
    ang = t[:, None] * bands[None, :]
    feats = jnp.concatenate([t[:, None], jnp.sin(ang), jnp.cos(ang)], axis=-1)
    z = jnp.sin(hy_freq[0] * (jnp.matmul(feats, hy_w1, precision=HIGHEST) + hy_b1))
    z = jnp.sin(hy_freq[1] * (jnp.matmul(z, hy_w2, precision=HIGHEST) + hy_b2))
    h = jnp.matmul(z, hy_w3, precision=HIGHEST).reshape(L, HYENA_ORDER, 2, B_WIDTH)
    h = h * jnp.exp(-jnp.abs(hy_decay)[None] * t[:, None, None, None])
    return jnp.transpose(h, (1, 2, 3, 0))


def _short_conv_params(hy_conv_w, hy_conv_b):
    return jnp.concatenate([hy_conv_w.T, hy_conv_b[:, None]], axis=1)


def _dense_dft_consts(S):
    n = 2 * S
    t = np.arange(S)[:, None]
    k = np.arange(n)[None, :]
    ang = -2.0 * np.pi * t * k / n
    cr, ci = np.cos(ang), np.sin(ang)
    wf = np.block([[cr, ci], [-ci, cr]])
    er, ei = cr.T, -ci.T
    wi = np.block([[er, ei], [-ei, er]])
    return jnp.asarray(wf, F32), jnp.asarray(wi, F32)


def _ctx_spec_kernel(f_ref, skip_ref, wf_ref, hr_ref, hi_ref, *, S):
    n = 2 * S
    fwd = f_ref[0, 0]
    bwd = f_ref[0, 1]
    lane = lax.broadcasted_iota(jnp.int32, bwd.shape, 1)
    bwd = jnp.where(lane == 0, 0.0, bwd)
    w = wf_ref[0:S, :]
    ff = jnp.dot(fwd, w, precision=HIGHEST, preferred_element_type=F32)
    fb = jnp.dot(bwd, w, precision=HIGHEST, preferred_element_type=F32)
    skip = skip_ref[0]
    hr_ref[0] = (ff[:, :n] + fb[:, :n] + skip) * (1.0 / n)
    hi_ref[0] = (ff[:, n:] - fb[:, n:]) * (1.0 / n)


def _ctx_spectrum(filt, skip, wf, S):
    n = 2 * S
    kernel = functools.partial(_ctx_spec_kernel, S=S)
    return pl.pallas_call(
        kernel,
        grid=(HYENA_ORDER,),
        in_specs=[
            pl.BlockSpec((1, 2, B_WIDTH, S), lambda o: (o, 0, 0, 0)),
            pl.BlockSpec((1, B_WIDTH, 1), lambda o: (o, 0, 0)),
            pl.BlockSpec(wf.shape, lambda o: (0, 0)),
        ],
        out_specs=[pl.BlockSpec((1, B_WIDTH, n), lambda o: (o, 0, 0))] * 2,
        out_shape=[jax.ShapeDtypeStruct((HYENA_ORDER, B_WIDTH, n), F32)] * 2,
        compiler_params=_cparams(("arbitrary",)),
        name="hyena_ctx_spectrum",
    )(filt, skip.reshape(HYENA_ORDER, B_WIDTH, 1), wf)


def _hyena_ctx_kernel(prm_ref, v_ref, x1_ref, x2_ref, bg_ref, hr_ref, hi_ref, wf_ref, wi_ref, o_ref, *, S):
    nb, cb, _ = v_ref.shape
    half = nb // 2
    n = 2 * S
    lane = lax.broadcasted_iota(jnp.int32, (nb, cb, S), 2)

    def short_conv(u, p):
        prev = jnp.where(lane == 0, 0.0, pltpu.roll(u, 1, 2))
        nxt = jnp.where(lane == S - 1, 0.0, pltpu.roll(u, S - 1, 2))
        return p[:, 3:4] + prev * p[:, 0:1] + u * p[:, 1:2] + nxt * p[:, 2:3]

    def stack(u):
        return jnp.concatenate([u[:half], u[half:]], axis=-1)

    def conv(xs, o):
        spec = jnp.dot(xs.reshape(half * cb, 2 * S).astype(BF16), wf_ref[...],
                       preferred_element_type=F32).reshape(half, cb, 2 * n)
        xr, xi = spec[..., :n], spec[..., n:]
        hr, hi = hr_ref[o], hi_ref[o]
        y = jnp.concatenate([xr * hr - xi * hi, xr * hi + xi * hr], axis=-1)
        return jnp.dot(y.reshape(half * cb, 2 * n).astype(BF16), wi_ref[...],
                       preferred_element_type=F32).reshape(half, cb, 2 * S)

    v = stack(short_conv(v_ref[...], prm_ref[0]))
    x1 = stack(short_conv(x1_ref[...], prm_ref[1]))
    x2 = stack(short_conv(x2_ref[...], prm_ref[2]))
    z = x1 * conv(v, 0)
    y = x2 * conv(z, 1)
    bg = stack(bg_ref[...])
    y = y * (bg * _sigmoid(bg))
    o_ref[0:half] = y[..., :S]
    o_ref[half:nb] = y[..., S:]


def _hyena_ctx(t_arr, prm, hr, hi, wf, wi, *, nb, S):
    cb = 16
    n = 2 * S
    ncb = B_WIDTH // cb
    kernel = functools.partial(_hyena_ctx_kernel, S=S)

    def part(k):
        return pl.BlockSpec((nb, cb, S), lambda c: (0, c + k * ncb, 0))

    return pl.pallas_call(
        kernel,
        grid=(ncb,),
        in_specs=[
            pl.BlockSpec((3, cb, 4), lambda c: (0, c, 0)),
            part(0), part(1), part(2), part(3),
            pl.BlockSpec((HYENA_ORDER, cb, n), lambda c: (0, c, 0)),
            pl.BlockSpec((HYENA_ORDER, cb, n), lambda c: (0, c, 0)),
            pl.BlockSpec(wf.shape, lambda c: (0, 0)),
            pl.BlockSpec(wi.shape, lambda c: (0, 0)),
        ],
        out_specs=pl.BlockSpec((nb, cb, S), lambda c: (0, c, 0)),
        out_shape=jax.ShapeDtypeStruct((nb, B_WIDTH, S), F32),
        compiler_params=_cparams(("arbitrary",)),
        name="hyena_ctx",
    )(prm.reshape(3, B_WIDTH, 4), t_arr, t_arr, t_arr, t_arr, hr, hi, wf.astype(BF16), wi.astype(BF16))


def _two_stage_consts(L, nseq, dtype):
    n2 = LANES
    n = 2 * L
    n1 = n // n2
    h1 = n1 // 2
    k1 = np.arange(n1)[:, None]
    a = -2.0 * np.pi * k1 * np.arange(h1)[None, :] / n1
    w1r, w1i = np.cos(a), np.sin(a)
    w1big = np.block([[w1r, -w1i], [w1i, w1r]])
    vr, vi = w1r.T, -w1i.T
    w1inv = np.block([[vr, -vi], [vi, vr]])
    a = -2.0 * np.pi * k1 * np.arange(n2)[None, :] / n
    twr, twi = np.cos(a), np.sin(a)
    a = -2.0 * np.pi * np.arange(n2)[:, None] * np.arange(n2)[None, :] / n2
    w2r, w2i = np.cos(a), np.sin(a)
    w2big = np.block([[w2r, w2i], [-w2i, w2r]])
    w2conj = np.block([[w2r, -w2i], [w2i, w2r]])
    f = lambda x: jnp.asarray(x, F32)
    m = lambda x: jnp.asarray(x, F32).astype(dtype)
    return dict(
        w1b=jnp.broadcast_to(m(w1big)[None], (nseq,) + w1big.shape),
        w1ib=jnp.broadcast_to(m(w1inv)[None], (nseq,) + w1inv.shape),
        w2=m(w2big), w2c=m(w2conj), twr=f(twr), twi=f(twi), n1=n1, h1=h1)


def _dft_dot(spec, a, b):
    if a.dtype == BF16 or b.dtype == BF16:
        return jnp.einsum(spec, a.astype(BF16), b.astype(BF16), preferred_element_type=F32)
    return jnp.einsum(spec, a, b, precision=HIGHEST, preferred_element_type=F32)


def _stage_fwd(xs, w1b, twr, twi, w2):
    s = xs.shape[0]
    n1 = twr.shape[0]
    a = _dft_dot('smk,skn->smn', w1b, xs)
    ar, ai = a[:, :n1], a[:, n1:]
    p = jnp.concatenate([ar * twr - ai * twi, ar * twi + ai * twr], axis=-1)
    return _dft_dot('mk,kn->mn', p.reshape(s * n1, 2 * LANES), w2)


def _stage_inv(cm, w1ib, twr, twi, w2c, s):
    n1 = twr.shape[0]
    dm = _dft_dot('mk,kn->mn', cm, w2c).reshape(s, n1, 2 * LANES)
    dr, di = dm[..., :LANES], dm[..., LANES:]
    r = jnp.concatenate([dr * twr + di * twi, di * twr - dr * twi], axis=1)
    return _dft_dot('smk,skn->smn', w1ib, r)


def _lat_spec_kernel(skip_ref, f_ref, w1b_ref, twr_ref, twi_ref, w2_ref, hr_ref, hi_ref, *, n_fft):
    cb, h1 = f_ref.shape[2], f_ref.shape[3]
    n1 = twr_ref.shape[0]
    fwd = f_ref[0, 0]
    bwd = f_ref[0, 1]
    first = (lax.broadcasted_iota(jnp.int32, bwd.shape, 1) == 0) & \
            (lax.broadcasted_iota(jnp.int32, bwd.shape, 2) == 0)
    bwd = jnp.where(first, 0.0, bwd)
    xs = jnp.concatenate([fwd, bwd], axis=0)
    xs = jnp.concatenate([xs, jnp.zeros_like(xs)], axis=1)
    sp = _stage_fwd(xs, w1b_ref[...], twr_ref[...], twi_ref[...], w2_ref[...]).reshape(2, cb, n1, 2 * LANES)
    inv = 1.0 / n_fft
    for c in range(cb):
        skip = skip_ref[pl.program_id(0), pl.program_id(1) * cb + c]
        hr_ref[0, c] = (sp[0, c, :, :LANES] + sp[1, c, :, :LANES] + skip) * inv
        hi_ref[0, c] = (sp[0, c, :, LANES:] - sp[1, c, :, LANES:]) * inv


def _lat_spectrum(filt, skip, L):
    cb = 8
    cs = _two_stage_consts(L, 2 * cb, F32)
    n1, h1 = cs["n1"], cs["h1"]
    filt5 = filt.reshape(HYENA_ORDER, 2, B_WIDTH, h1, LANES)
    kernel = functools.partial(_lat_spec_kernel, n_fft=2 * L)
    const = lambda a: pl.BlockSpec(a.shape, lambda o, c: (0,) * a.ndim)
    return pl.pallas_call(
        kernel,
        grid=(HYENA_ORDER, B_WIDTH // cb),
        in_specs=[
            pl.BlockSpec(memory_space=pltpu.SMEM),
            pl.BlockSpec((1, 2, cb, h1, LANES), lambda o, c: (o, 0, c, 0, 0)),
            const(cs["w1b"]), const(cs["twr"]), const(cs["twi"]), const(cs["w2"]),
        ],
        out_specs=[pl.BlockSpec((1, cb, n1, LANES), lambda o, c: (o, c, 0, 0))] * 2,
        out_shape=[jax.ShapeDtypeStruct((HYENA_ORDER, B_WIDTH, n1, LANES), F32)] * 2,
        compiler_params=_cparams(("arbitrary", "arbitrary")),
        name="hyena_lat_spectrum",
    )(skip, filt5, cs["w1b"], cs["twr"], cs["twi"], cs["w2"])


def _hyena_lat_kernel(prm_ref, v_ref, x1_ref, x2_ref, bg_ref, hr_ref, hi_ref,
                      w1b_ref, w1ib_ref, twr_ref, twi_ref, w2_ref, w2c_ref, o_ref,
                      vs_ref, x1s_ref, x2s_ref):
    nb, cb, h1, _ = v_ref.shape
    half = nb // 2
    s = cb * half
    n1 = twr_ref.shape[0]
    c0 = pl.program_id(0) * cb
    row = lax.broadcasted_iota(jnp.int32, (nb, h1, LANES), 1)
    lane = lax.broadcasted_iota(jnp.int32, (nb, h1, LANES), 2)

    def short_conv(u, part, c):
        r = pltpu.roll(u, 1, 2)
        prev = jnp.where(lane == 0, pltpu.roll(r, 1, 1), r)
        prev = jnp.where((lane == 0) & (row == 0), 0.0, prev)
        r = pltpu.roll(u, LANES - 1, 2)
        nxt = jnp.where(lane == LANES - 1, pltpu.roll(r, h1 - 1, 1), r)
        nxt = jnp.where((lane == LANES - 1) & (row == h1 - 1), 0.0, nxt)
        ch = part * B_WIDTH + c0 + c
        return prm_ref[3, ch] + prev * prm_ref[0, ch] + u * prm_ref[1, ch] + nxt * prm_ref[2, ch]

    def stack_into(dst_ref, u, c):
        dst_ref[c * half:(c + 1) * half, 0:h1, :] = u[:half]
        dst_ref[c * half:(c + 1) * half, h1:2 * h1, :] = u[half:]

    for c in range(cb):
        stack_into(vs_ref, short_conv(v_ref[:, c], 0, c), c)
        stack_into(x1s_ref, short_conv(x1_ref[:, c], 1, c), c)
        bg = bg_ref[:, c]
        stack_into(x2s_ref, short_conv(x2_ref[:, c], 2, c) * (bg * _sigmoid(bg)), c)

    twr, twi = twr_ref[...], twi_ref[...]

    def conv(xs, o):
        sp = _stage_fwd(xs, w1b_ref[...], twr, twi, w2_ref[...]).reshape(cb, half, n1, 2 * LANES)
        br, bi = sp[..., :LANES], sp[..., LANES:]
        hr = hr_ref[o][:, None]
        hi = hi_ref[o][:, None]
        cm = jnp.concatenate([br * hr - bi * hi, br * hi + bi * hr], axis=-1).reshape(s * n1, 2 * LANES)
        return _stage_inv(cm, w1ib_ref[...], twr, twi, w2c_ref[...], s)

    z = x1s_ref[...] * conv(vs_ref[...], 0)
    y = x2s_ref[...] * conv(z, 1)
    for c in range(cb):
        o_ref[0:half, c] = y[c * half:(c + 1) * half, 0:h1]
        o_ref[half:nb, c] = y[c * half:(c + 1) * half, h1:2 * h1]


def _hyena_lat(t_arr, prm, hr, hi, *, nb, L):
    cb = 8
    half = nb // 2
    s = cb * half
    cs = _two_stage_consts(L, s, BF16)
    n1, h1 = cs["n1"], cs["h1"]
    ncb = B_WIDTH // cb
    t5 = t_arr.reshape(nb, T_W, h1, LANES)

    def part(k):
        return pl.BlockSpec((nb, cb, h1, LANES), lambda c: (0, c + k * ncb, 0, 0))

    const = lambda a: pl.BlockSpec(a.shape, lambda c: (0,) * a.ndim)
    out = pl.pallas_call(
        _hyena_lat_kernel,
        grid=(ncb,),
        in_specs=[
            pl.BlockSpec(memory_space=pltpu.SMEM),
            part(0), part(1), part(2), part(3),
            pl.BlockSpec((HYENA_ORDER, cb, n1, LANES), lambda c: (0, c, 0, 0)),
            pl.BlockSpec((HYENA_ORDER, cb, n1, LANES), lambda c: (0, c, 0, 0)),
            const(cs["w1b"]), const(cs["w1ib"]), const(cs["twr"]), const(cs["twi"]),
            const(cs["w2"]), const(cs["w2c"]),
        ],
        out_specs=pl.BlockSpec((nb, cb, h1, LANES), lambda c: (0, c, 0, 0)),
        out_shape=jax.ShapeDtypeStruct((nb, B_WIDTH, h1, LANES), F32),
        scratch_shapes=[pltpu.VMEM((s, 2 * h1, LANES), F32)] * 3,
        compiler_params=_cparams(("arbitrary",)),
        name="hyena_lat",
    )(prm.T, t5, t5, t5, t5, hr, hi,
      cs["w1b"], cs["w1ib"], cs["twr"], cs["twi"], cs["w2"], cs["w2c"])
    return out.reshape(nb, B_WIDTH, L)


def _merge_kernel(x_ref, ya_ref, ybt_ref, yc_ref, mg_ref, gate_ref, wa_ref, wb_ref, wc_ref, wo_ref,
                  fw_ref, o_ref, *, final):
    mg = mg_ref[...]
    yb = ybt_ref[0].T.astype(BF16)
    m = (_sigmoid(mg[:, :D_MODEL].astype(F32)) * jnp.dot(ya_ref[...], wa_ref[...], preferred_element_type=F32)
         + _sigmoid(mg[:, D_MODEL:2 * D_MODEL].astype(F32)) * jnp.dot(yb, wb_ref[...], preferred_element_type=F32)
         + _sigmoid(mg[:, 2 * D_MODEL:].astype(F32)) * jnp.dot(yc_ref[...], wc_ref[...], preferred_element_type=F32))
    out = jnp.dot(m.astype(BF16), wo_ref[...], preferred_element_type=F32)
    xn = x_ref[...] + gate_ref[0] * out
    if final:
        var = jnp.mean(xn * xn, axis=-1, keepdims=True)
        xn = xn * lax.rsqrt(var + EPS) * fw_ref[...]
    o_ref[...] = xn


def _merge(x2d, ya, ybt, yc, u, gate, wa, wb, wc, wo, fw, *, nb, lb, tm, final):
    rows = nb * lb
    bpb = lb // tm
    per_mod = gate.shape[0] > 1
    kernel = functools.partial(_merge_kernel, final=final)
    const = lambda a: pl.BlockSpec(a.shape, lambda i: (0,) * a.ndim, pipeline_mode=pl.Buffered(1))
    return pl.pallas_call(
        kernel,
        grid=(rows // tm,),
        in_specs=[
            pl.BlockSpec((tm, D_MODEL), lambda i: (i, 0)),
            pl.BlockSpec((tm, A_WIDTH), lambda i: (i, 0)),
            pl.BlockSpec((1, B_WIDTH, tm), lambda i: (i // bpb, 0, i % bpb)),
            pl.BlockSpec((tm, C_WIDTH), lambda i: (i, 0)),
            pl.BlockSpec((tm, MG_W), lambda i: (i, U_MG // MG_W)),
            pl.BlockSpec((1, 1, D_MODEL), lambda i: ((i // bpb) if per_mod else 0, 0, 0)),
            const(wa), const(wb), const(wc), const(wo),
            pl.BlockSpec((1, D_MODEL), lambda i: (0, 0)),
        ],
        out_specs=pl.BlockSpec((tm, D_MODEL), lambda i: (i, 0)),
        out_shape=jax.ShapeDtypeStruct((rows, D_MODEL), F32),
        compiler_params=_cparams(("arbitrary",)),
        name="merge_final" if final else "merge",
    )(x2d, ya.reshape(rows, A_WIDTH), ybt, yc.reshape(rows, C_WIDTH), u, gate, wa, wb, wc, wo,
      fw.reshape(1, D_MODEL))


def _rope_tables(L):
    t = np.arange(L)
    row = (t // GRID_W).astype(np.float32)
    col = (t % GRID_W).astype(np.float32)
    nf = HEAD_DIM // 4
    inv = np.power(np.float32(ROPE_BASE), -np.arange(nf, dtype=np.float32) / nf).astype(np.float32)
    ang = np.concatenate([row[:, None] * inv[None], col[:, None] * inv[None]], axis=-1)
    cos, sin = np.cos(ang), np.sin(ang)
    reps = LANES // HEAD_DIM
    cos_t = np.tile(np.concatenate([cos, cos], axis=-1), (1, reps))
    sin_t = np.tile(np.concatenate([-sin, sin], axis=-1), (1, reps))
    return jnp.asarray(cos_t, F32), jnp.asarray(sin_t, F32)


def _head_cols(base):
    return np.concatenate([base + h * HEAD_DIM + np.arange(HEAD_DIM) for h in A_PERM])


def _regroup_w_in(w):
    w_nat = jnp.concatenate([
        w[:, IN_MG:],
        w[:, IN_CQ:IN_MG],
        w[:, _head_cols(IN_AQ)], w[:, _head_cols(IN_AG)],
        w[:, IN_AK:IN_AV], w[:, IN_AV:IN_AG],
    ], axis=1).astype(BF16)
    return w_nat, w[:, IN_BU:IN_CQ].T.astype(BF16)


def kernel(x_prompt, x_sample, c, cache_a_k, cache_a_v, cache_c_k, cache_c_v, c_ctx, norm_w, w_ada, b_ada, w_in, a_sink, hy_conv_w, hy_conv_b, hy_w1, hy_b1, hy_w2, hy_b2, hy_freq, hy_w3, hy_decay, hy_skip, c_rpb, w_up_a, w_up_b, w_up_c, w_out, final_norm_w):
    nbc, S, _ = x_prompt.shape
    nbl, L, _ = x_sample.shape
    depth = w_in.shape[0]
    n_ctx = cache_a_k.shape[2]
    assert nbc % 2 == 0 and nbl % 2 == 0 and L % (GRID_W * LANES // 2) == 0

    pad = (-(nbl + 1)) % 8
    cond = jnp.concatenate([c, c_ctx[None], jnp.zeros((pad, D_MODEL), F32)], axis=0)
    mod = _adaln(cond, w_ada, b_ada)

    cos_t, sin_t = _rope_tables(L)
    tm_ctx = 512 if (nbc * S) % 512 == 0 else S
    zeros_t = jnp.zeros((tm_ctx, LANES), F32)
    wf_c, wi_c = _dense_dft_consts(S)
    ca_k = cache_a_k.reshape(nbl, depth, n_ctx, A_KV_WIDTH)
    ca_v = cache_a_v.reshape(nbl, depth, n_ctx, A_KV_WIDTH)
    cc_k = cache_c_k.reshape(nbl, depth, n_ctx, C_WIDTH)
    cc_v = cache_c_v.reshape(nbl, depth, n_ctx, C_WIDTH)
    a_rows = _head_cols(0)

    xp = x_prompt.reshape(nbc * S, D_MODEL)
    xs = x_sample.reshape(nbl * L, D_MODEL)
    aks, avs, cks, cvs = [], [], [], []
    for l in range(depth):
        final = l == depth - 1
        w_nat, w_t = _regroup_w_in(w_in[l])
        wa, wb, wc, wo = (w.astype(BF16) for w in (w_up_a[l][a_rows], w_up_b[l], w_up_c[l], w_out[l]))
        prm = _short_conv_params(hy_conv_w[l], hy_conv_b[l])
        filt_args = (hy_w1[l], hy_b1[l], hy_w2[l], hy_b2[l], hy_freq[l], hy_w3[l], hy_decay[l])
        t2 = _nbr_bias_table(c_rpb[l])

        def mod_parts(rows):
            m = mod[l, rows][:, None, :]
            return 1.0 + m[..., D_MODEL:2 * D_MODEL], m[..., :D_MODEL], m[..., 2 * D_MODEL:]

        sc, sh, gate = mod_parts(slice(nbl, nbl + 1))
        u, t_arr = _inproj(xp, sc, sh, norm_w[l], w_nat, w_t, zeros_t, zeros_t,
                           nb=nbc, lb=S, tm=tm_ctx, rope=False, u_dtype=F32)
        aks.append(u[:, U_AK:U_AV].reshape(nbc, S, A_KV_HEADS, HEAD_DIM))
        avs.append(u[:, U_AV:NAT_W].reshape(nbc, S, A_KV_HEADS, HEAD_DIM))
        cks.append(u[:, U_CK:U_CV].reshape(nbc, S, C_HEADS, HEAD_DIM))
        cvs.append(u[:, U_CV:U_CG].reshape(nbc, S, C_HEADS, HEAD_DIM))
        ya, yc = _ctx_attn(a_sink[l], u, nb=nbc, lb=S)
        hr, hi = _ctx_spectrum(_hyena_filters(S, *filt_args), hy_skip[l], wf_c, S)
        ybt = _hyena_ctx(t_arr, prm, hr, hi, wf_c, wi_c, nb=nbc, S=S)
        xp = _merge(xp, ya, ybt, yc, u, gate, wa, wb, wc, wo, final_norm_w,
                    nb=nbc, lb=S, tm=S, final=final)

        sc, sh, gate = mod_parts(slice(0, nbl))
        u, t_arr = _inproj(xs, sc, sh, norm_w[l], w_nat, w_t, cos_t, sin_t,
                           nb=nbl, lb=L, tm=1024, rope=True, u_dtype=BF16)
        ya = _lat_win_attn(a_sink[l], u, ca_k, ca_v, l, nb=nbl, lb=L)
        yc = _lat_nbr_attn(u, cc_k, cc_v, l, t2, nb=nbl, lb=L)
        hr, hi = _lat_spectrum(_hyena_filters(L, *filt_args), hy_skip[l], L)
        ybt = _hyena_lat(t_arr, prm, hr, hi, nb=nbl, L=L)
        xs = _merge(xs, ya, ybt, yc, u, gate, wa, wb, wc, wo, final_norm_w,
                    nb=nbl, lb=L, tm=512, final=final)

    y_prompt = xp.reshape(nbc, S, D_MODEL)
    y_sample = xs.reshape(nbl, L, D_MODEL)
    return (y_prompt, y_sample, jnp.stack(aks, axis=1), jnp.stack(avs, axis=1),
            jnp.stack(cks, axis=1), jnp.stack(cvs, axis=1))
```

```python
import functools

import numpy as np
import jax
import jax.numpy as jnp
from jax import lax
from jax.experimental import pallas as pl
from jax.experimental.pallas import tpu as pltpu

F32 = jnp.float32
BF16 = jnp.bfloat16
HIGHEST = lax.Precision.HIGHEST

D_MODEL = 2048
HEAD_DIM = 64
A_HEADS = 12
A_KV_HEADS = 4
A_GROUP = A_HEADS // A_KV_HEADS
A_WIDTH = A_HEADS * HEAD_DIM
A_KV_WIDTH = A_KV_HEADS * HEAD_DIM
A_WINDOW = 128
B_WIDTH = 512
HYENA_ORDER = 2
HYENA_BANDS = 16
C_HEADS = 12
C_WIDTH = C_HEADS * HEAD_DIM
GRID_W = 64
NA_ROWS = 8
NA_COLS = 16
ROPE_BASE = 10000.0
EPS = 1e-6
NEG_INF = -1e30
ATTN_SCALE = HEAD_DIM ** -0.5
LOG2E = 1.4426950408889634
QK_SCALE = ATTN_SCALE * LOG2E

LANES = 128
VMEM_LIMIT = 56 * 1024 * 1024

IN_AQ, IN_AK, IN_AV, IN_AG = 0, 768, 1024, 1280
IN_BU, IN_CQ, IN_MG = 2048, 4096, 7168
PROJ_TN = 1024
MG_W = 3 * D_MODEL
U_MG = 0
U_CQ = U_MG + MG_W
U_CK, U_CV, U_CG = U_CQ + C_WIDTH, U_CQ + 2 * C_WIDTH, U_CQ + 3 * C_WIDTH
U_AQ = U_CG + C_WIDTH
U_AG = U_AQ + A_WIDTH
U_AK = U_AG + A_WIDTH
U_AV = U_AK + A_KV_WIDTH
NAT_W = U_AV + A_KV_WIDTH
T_W = 4 * B_WIDTH

A_PERM = tuple(2 * A_GROUP * m + A_GROUP * half + j
               for m in range(A_KV_HEADS // 2) for j in range(A_GROUP) for half in range(2))


def _cparams(sem):
    return pltpu.CompilerParams(dimension_semantics=sem, vmem_limit_bytes=VMEM_LIMIT)


def _sigmoid(x):
    return 1.0 / (1.0 + jnp.exp(-x))


def _lane_chunks(off, width):
    blk, rem = divmod(off, PROJ_TN)
    assert rem % LANES == 0 and width % LANES == 0 and rem + width <= PROJ_TN
    return blk, rem // LANES, (rem + width) // LANES


def _adaln_kernel(cond_ref, w_ref, b_ref, o_ref):
    c = cond_ref[...]
    s = (c * _sigmoid(c)).astype(BF16)
    acc = jnp.dot(s, w_ref[0].astype(BF16), preferred_element_type=F32)
    o_ref[0] = acc + b_ref[0]


def _adaln(cond, w_ada, b_ada):
    depth = w_ada.shape[0]
    rows = cond.shape[0]
    tn = 1024
    return pl.pallas_call(
        _adaln_kernel,
        grid=(depth, 3 * D_MODEL // tn),
        in_specs=[
            pl.BlockSpec((rows, D_MODEL), lambda l, j: (0, 0)),
            pl.BlockSpec((1, D_MODEL, tn), lambda l, j: (l, 0, j)),
            pl.BlockSpec((1, 1, tn), lambda l, j: (l, 0, j)),
        ],
        out_specs=pl.BlockSpec((1, rows, tn), lambda l, j: (l, 0, j)),
        out_shape=jax.ShapeDtypeStruct((depth, rows, 3 * D_MODEL), F32),
        compiler_params=_cparams(("arbitrary", "arbitrary")),
        name="adaln",
    )(cond, w_ada, b_ada.reshape(depth, 1, 3 * D_MODEL))


def _inproj_kernel(x_ref, sc_ref, sh_ref, nw_ref, w_ref, wt_ref, cos_ref, sin_ref,
                   u_ref, t_ref, h_ref, *, rope, tm, lb):
    j = pl.program_id(1)
    tn = PROJ_TN
    n_nat = NAT_W // tn

    @pl.when(j == 0)
    def _():
        x = x_ref[...]
        var = jnp.mean(x * x, axis=-1, keepdims=True)
        y = x * lax.rsqrt(var + EPS) * nw_ref[...]
        h_ref[...] = (y * sc_ref[0] + sh_ref[0]).astype(BF16)

    def rotate(a):
        lane = lax.broadcasted_iota(jnp.int32, (tm, LANES), 1)
        first = (lane % HEAD_DIM) < (HEAD_DIM // 2)
        partner = jnp.where(first, pltpu.roll(a, LANES - HEAD_DIM // 2, 1), pltpu.roll(a, HEAD_DIM // 2, 1))
        return a * cos_ref[...] + partner * sin_ref[...]

    def nat_step(chunk_ops):
        acc = jnp.dot(h_ref[...], w_ref[...], preferred_element_type=F32)
        if not chunk_ops:
            u_ref[...] = acc.astype(u_ref.dtype)
            return
        for ci in range(tn // LANES):
            a = acc[:, ci * LANES:(ci + 1) * LANES]
            for op in chunk_ops.get(ci, ()):
                a = rotate(a) if op == "rope" else a * QK_SCALE
            u_ref[:, ci * LANES:(ci + 1) * LANES] = a.astype(u_ref.dtype)

    special = {}
    for off, width, ops in ((U_CQ, C_WIDTH, ("scale",)),
                            (U_AQ, A_WIDTH, ("rope", "scale") if rope else ("scale",)),
                            (U_AK, A_KV_WIDTH, ("rope",) if rope else ())):
        if ops:
            blk, c0, c1 = _lane_chunks(off, width)
            special.setdefault(blk, {}).update({ci: ops for ci in range(c0, c1)})

    plain = j < n_nat
    for blk, chunk_ops in special.items():
        plain = plain & (j != blk)
        pl.when(j == blk)(functools.partial(nat_step, chunk_ops))
    pl.when(plain)(functools.partial(nat_step, None))

    @pl.when(j >= n_nat)
    def _():
        acc_t = lax.dot_general(wt_ref[...], h_ref[...], (((1,), (1,)), ((), ())),
                                preferred_element_type=F32)
        if tm <= lb:
            t_ref[0] = acc_t
        else:
            for k in range(tm // lb):
                t_ref[k] = acc_t[:, k * lb:(k + 1) * lb]


def _inproj(x2d, sc, sh, norm_w, w_nat, w_t, cos_t, sin_t, *, nb, lb, tm, rope, u_dtype):
    rows = nb * lb
    tn = PROJ_TN
    n_nat, n_t = NAT_W // tn, T_W // tn
    per_mod = sc.shape[0] > 1
    bpb = max(lb // tm, 1)

    def mod_map(i, j):
        return ((i // bpb) if per_mod else 0, 0, 0)

    if tm <= lb:
        t_spec = pl.BlockSpec((1, tn, tm), lambda i, j: (i // bpb, jnp.clip(j - n_nat, 0, n_t - 1), i % bpb))
    else:
        t_spec = pl.BlockSpec((tm // lb, tn, lb), lambda i, j: (i, jnp.clip(j - n_nat, 0, n_t - 1), 0))

    kernel = functools.partial(_inproj_kernel, rope=rope, tm=tm, lb=lb)
    return pl.pallas_call(
        kernel,
        grid=(rows // tm, n_nat + n_t),
        in_specs=[
            pl.BlockSpec((tm, D_MODEL), lambda i, j: (i, 0), pipeline_mode=pl.Buffered(1)),
            pl.BlockSpec((1, 1, D_MODEL), mod_map),
            pl.BlockSpec((1, 1, D_MODEL), mod_map),
            pl.BlockSpec((1, D_MODEL), lambda i, j: (0, 0)),
            pl.BlockSpec((D_MODEL, tn), lambda i, j: (0, jnp.minimum(j, n_nat - 1))),
            pl.BlockSpec((tn, D_MODEL), lambda i, j: (jnp.maximum(j - n_nat, 0), 0)),
            pl.BlockSpec((tm, LANES), lambda i, j: (i % bpb, 0)),
            pl.BlockSpec((tm, LANES), lambda i, j: (i % bpb, 0)),
        ],
        out_specs=[
            pl.BlockSpec((tm, tn), lambda i, j: (i, jnp.minimum(j, n_nat - 1))),
            t_spec,
        ],
        out_shape=[
            jax.ShapeDtypeStruct((rows, NAT_W), u_dtype),
            jax.ShapeDtypeStruct((nb, T_W, lb), F32),
        ],
        scratch_shapes=[pltpu.VMEM((tm, D_MODEL), BF16)],
        compiler_params=_cparams(("arbitrary", "arbitrary")),
        name="inproj_rope" if rope else "inproj",
    )(x2d, sc, sh, norm_w.reshape(1, D_MODEL), w_nat, w_t, cos_t, sin_t)


def _softmax_pv(s, v, sink_col):
    m = jnp.max(s, axis=-1, keepdims=True)
    if sink_col is not None:
        m = jnp.maximum(m, sink_col)
    e = jnp.exp2(s - m)
    den = jnp.sum(e, axis=-1, keepdims=True)
    if sink_col is not None:
        den = den + jnp.exp2(sink_col - m)
    o = jnp.dot(e.astype(BF16), v, preferred_element_type=F32)
    return o * (1.0 / den)


def _sink_column(sink_ref, heads, rows_per_head):
    n = len(heads)
    row_head = lax.broadcasted_iota(jnp.int32, (n * rows_per_head, 1), 0) // rows_per_head
    col = jnp.full((n * rows_per_head, 1), sink_ref[heads[-1]], F32)
    for idx in range(n - 1):
        col = jnp.where(row_head == idx, sink_ref[heads[idx]], col)
    return col * LOG2E


def _ctx_attn_kernel(sink_ref, cc_ref, a0_ref, a1_ref, ya_ref, yc_ref):
    cc = cc_ref[0]
    a = jnp.concatenate([a0_ref[0], a1_ref[0]], axis=1)
    s_len = cc.shape[0]
    nt = (((1,), (1,)), ((), ()))
    o_aq, o_ag, o_ak, o_av = 0, U_AG - U_AQ, U_AK - U_AQ, U_AV - U_AQ

    pieces = [None] * A_HEADS
    for g in range(A_KV_HEADS):
        heads = [A_GROUP * g + hh for hh in range(A_GROUP)]
        pos = [A_PERM.index(h) for h in heads]
        q = jnp.concatenate([a[:, o_aq + p * HEAD_DIM:o_aq + (p + 1) * HEAD_DIM] for p in pos],
                            axis=0).astype(BF16)
        k = a[:, o_ak + g * HEAD_DIM:o_ak + (g + 1) * HEAD_DIM].astype(BF16)
        v = a[:, o_av + g * HEAD_DIM:o_av + (g + 1) * HEAD_DIM].astype(BF16)
        s = lax.dot_general(q, k, nt, preferred_element_type=F32)
        o = _softmax_pv(s, v, _sink_column(sink_ref, heads, s_len))
        for hh, p in enumerate(pos):
            pieces[p] = o[hh * s_len:(hh + 1) * s_len]
    ya = jnp.concatenate(pieces, axis=1)
    ag = a[:, o_ag:o_ag + A_WIDTH]
    ya_ref[0] = (ya * (ag * _sigmoid(ag))).astype(BF16)

    pieces = []
    for h in range(C_HEADS):
        q = cc[:, h * HEAD_DIM:(h + 1) * HEAD_DIM].astype(BF16)
        k = cc[:, C_WIDTH + h * HEAD_DIM:C_WIDTH + (h + 1) * HEAD_DIM].astype(BF16)
        v = cc[:, 2 * C_WIDTH + h * HEAD_DIM:2 * C_WIDTH + (h + 1) * HEAD_DIM].astype(BF16)
        s = lax.dot_general(q, k, nt, preferred_element_type=F32)
        pieces.append(_softmax_pv(s, v, None))
    yc = jnp.concatenate(pieces, axis=1)
    cg = cc[:, 3 * C_WIDTH:]
    yc_ref[0] = (yc * (cg * _sigmoid(cg))).astype(BF16)


def _ctx_attn(sink, u, *, nb, lb):
    u3 = u.reshape(nb, lb, NAT_W)
    cc_w = 4 * C_WIDTH
    return pl.pallas_call(
        _ctx_attn_kernel,
        grid=(nb,),
        in_specs=[
            pl.BlockSpec(memory_space=pltpu.SMEM),
            pl.BlockSpec((1, lb, cc_w), lambda b: (b, 0, U_CQ // cc_w)),
            pl.BlockSpec((1, lb, PROJ_TN), lambda b: (b, 0, U_AQ // PROJ_TN)),
            pl.BlockSpec((1, lb, PROJ_TN), lambda b: (b, 0, U_AQ // PROJ_TN + 1)),
        ],
        out_specs=[
            pl.BlockSpec((1, lb, A_WIDTH), lambda b: (b, 0, 0)),
            pl.BlockSpec((1, lb, C_WIDTH), lambda b: (b, 0, 0)),
        ],
        out_shape=[
            jax.ShapeDtypeStruct((nb, lb, A_WIDTH), BF16),
            jax.ShapeDtypeStruct((nb, lb, C_WIDTH), BF16),
        ],
        compiler_params=_cparams(("arbitrary",)),
        name="ctx_attn",
    )(sink, u3, u3, u3)


def _own_half(v, half):
    lane = lax.broadcasted_iota(jnp.int32, v.shape, v.ndim - 1) % LANES
    keep = (lane < HEAD_DIM) if half == 0 else (lane >= HEAD_DIM)
    return jnp.where(keep, v, jnp.ones_like(v))


def _transpose_keys(k_ref, kt_scr, ck_ref, ckt_scr, cv_ref, cv_scr, chunk):
    n_tok = k_ref.shape[1]

    def body(c, carry):
        start = pl.multiple_of(c * chunk, chunk)
        blk = k_ref[0, pl.ds(start, chunk), :].astype(F32)
        kt_scr[:, pl.ds(start, chunk)] = blk.T.astype(BF16)
        return carry

    lax.fori_loop(0, n_tok // chunk, body, 0)
    ckt_scr[...] = ck_ref[0, 0].T.astype(BF16)
    ctx_v = cv_ref[0, 0].astype(BF16)
    for half in range(2):
        cv_scr[half] = _own_half(ctx_v, half)


PIPELINE_DEPTH = 2


def _software_pipeline(n, first_stage, second_stage):
    pending = [first_stage(h) for h in range(min(PIPELINE_DEPTH, n))]
    res = []
    for h in range(n):
        if h + PIPELINE_DEPTH < n:
            pending.append(first_stage(h + PIPELINE_DEPTH))
        res.append(second_stage(h, pending.pop(0)))
    return res


def _pair_scores(q_half, kt_loc, bias_fn, kt_ctx):
    s_loc = bias_fn(jnp.dot(q_half, kt_loc, preferred_element_type=F32))
    s_ctx = jnp.dot(q_half, kt_ctx, preferred_element_type=F32)
    return s_loc, s_ctx


def _pair_finish(scores, v_loc, v_ctx, sink_col):
    s_loc, s_ctx = scores
    m = jnp.maximum(jnp.max(s_loc, axis=-1, keepdims=True), jnp.max(s_ctx, axis=-1, keepdims=True))
    if sink_col is not None:
        m = jnp.maximum(m, sink_col)
    e_loc = jnp.exp2(s_loc - m).astype(BF16)
    e_ctx = jnp.exp2(s_ctx - m).astype(BF16)
    o = (jnp.dot(e_loc, v_loc, preferred_element_type=F32)
         + jnp.dot(e_ctx, v_ctx, preferred_element_type=F32))
    den = pltpu.roll(o, HEAD_DIM, 1)
    if sink_col is not None:
        den = den + jnp.exp2(sink_col - m)
    return o * (1.0 / den)


def _lat_win_kernel(sink_ref, q_ref, k_ref, v_ref, ag_ref, ck_ref, cv_ref, o_ref,
                    kt_scr, ckt_scr, cv_scr):
    i = pl.program_id(1)
    nblk = pl.num_programs(1)
    blk = A_WINDOW
    span = 3 * blk

    @pl.when(i == 0)
    def _():
        _transpose_keys(k_ref, kt_scr, ck_ref, ckt_scr, cv_ref, cv_scr, blk)

    start = pl.multiple_of(jnp.clip(i - 1, 0, nblk - 3) * blk, blk)
    q_pos = i * blk + lax.broadcasted_iota(jnp.int32, (blk, span), 0)
    k_pos = start + lax.broadcasted_iota(jnp.int32, (blk, span), 1)
    valid = jnp.abs(q_pos - k_pos) <= A_WINDOW
    mask = lambda s: jnp.where(valid, s, NEG_INF)

    q = q_ref[0]
    lane = lax.broadcasted_iota(jnp.int32, (blk, LANES), 1)
    low = lane < HEAD_DIM

    def scores(job):
        pair, half = divmod(job, 2)
        rows_m = slice(pair // A_GROUP * LANES, (pair // A_GROUP + 1) * LANES)
        qp = q[:, pair * LANES:(pair + 1) * LANES]
        q_half = jnp.where(low if half == 0 else ~low, qp, jnp.zeros_like(qp))
        return _pair_scores(q_half, kt_scr[rows_m, pl.ds(start, span)], mask, ckt_scr[rows_m, :])

    def finish(job, sc):
        pair, half = divmod(job, 2)
        rows_m = slice(pair // A_GROUP * LANES, (pair // A_GROUP + 1) * LANES)
        v_loc = v_ref[0, pl.ds(start, span), rows_m]
        return _pair_finish(sc, _own_half(v_loc, half), cv_scr[half, :, rows_m],
                            sink_ref[A_PERM[job]] * LOG2E)

    res = _software_pipeline(A_HEADS, scores, finish)
    ya = jnp.concatenate([jnp.where(low, res[2 * p], res[2 * p + 1]) for p in range(A_HEADS // 2)], axis=1)
    ag = ag_ref[0].astype(F32)
    o_ref[0] = (ya * (ag * _sigmoid(ag))).astype(BF16)


def _lat_win_attn(sink, u, cache_k, cache_v, layer, *, nb, lb):
    blk = A_WINDOW
    nblk = lb // blk
    assert nblk >= 3
    n_ctx = cache_k.shape[2]
    u3 = u.reshape(nb, lb, NAT_W)
    ctx_spec = pl.BlockSpec((1, 1, n_ctx, A_KV_WIDTH), lambda b, i: (b, layer, 0, 0))
    return pl.pallas_call(
        _lat_win_kernel,
        grid=(nb, nblk),
        in_specs=[
            pl.BlockSpec(memory_space=pltpu.SMEM),
            pl.BlockSpec((1, blk, A_WIDTH), lambda b, i: (b, i, U_AQ // A_WIDTH)),
            pl.BlockSpec((1, lb, A_KV_WIDTH), lambda b, i: (b, 0, U_AK // A_KV_WIDTH)),
            pl.BlockSpec((1, lb, A_KV_WIDTH), lambda b, i: (b, 0, U_AV // A_KV_WIDTH)),
            pl.BlockSpec((1, blk, A_WIDTH), lambda b, i: (b, i, U_AG // A_WIDTH)),
            ctx_spec, ctx_spec,
        ],
        out_specs=pl.BlockSpec((1, blk, A_WIDTH), lambda b, i: (b, i, 0)),
        out_shape=jax.ShapeDtypeStruct((nb, lb, A_WIDTH), BF16),
        scratch_shapes=[pltpu.VMEM((A_KV_WIDTH, lb), BF16),
                        pltpu.VMEM((A_KV_WIDTH, n_ctx), BF16),
                        pltpu.VMEM((2, n_ctx, A_KV_WIDTH), BF16)],
        compiler_params=_cparams(("arbitrary", "arbitrary")),
        name="lat_win_attn",
    )(sink, u3, u3, u3, u3, cache_k, cache_v)


NBR_QROWS = 2
NBR_BAND = NA_ROWS + NBR_QROWS


def _lat_nbr_kernel(q_ref, k_ref, v_ref, cg_ref, ck_ref, cv_ref, t2_ref, o_ref,
                    kt_scr, ckt_scr, cv_scr, *, grid_rows):
    i = pl.program_id(1)

    @pl.when(i == 0)
    def _():
        _transpose_keys(k_ref, kt_scr, ck_ref, ckt_scr, cv_ref, cv_scr, LANES)

    r0 = NBR_QROWS * i
    nq = NBR_QROWS * GRID_W
    nk = NBR_BAND * GRID_W
    bs = jnp.clip(r0 - NA_ROWS // 2, 0, grid_rows - NBR_BAND)
    start = pl.multiple_of(bs * GRID_W, LANES)

    q_row = r0 + lax.broadcasted_iota(jnp.int32, (nq, nk), 0) // GRID_W
    rstart = jnp.clip(q_row - NA_ROWS // 2, 0, grid_rows - NA_ROWS)
    k_row = bs + lax.broadcasted_iota(jnp.int32, (nq, nk), 1) // GRID_W
    valid = (k_row >= rstart) & (k_row < rstart + NA_ROWS)

    q = q_ref[0]
    lane = lax.broadcasted_iota(jnp.int32, (nq, LANES), 1)
    low = lane < HEAD_DIM
    def scores(h):
        m, half = divmod(h, 2)
        rows_m = slice(m * LANES, (m + 1) * LANES)
        bias_rows = []
        for a in range(NBR_QROWS):
            tiles = []
            for p in range(NBR_BAND // 2):
                d = (bs + 2 * p) - (r0 + a) + (NA_ROWS - 1)
                idx = jnp.clip(d, -1, 2 * NA_ROWS - 2) + 1
                tiles.append(t2_ref[h, idx])
            bias_rows.append(jnp.concatenate(tiles, axis=1))
        bias = jnp.concatenate(bias_rows, axis=0)
        qp = q[:, rows_m]
        q_half = jnp.where(low if half == 0 else ~low, qp, jnp.zeros_like(qp))
        return _pair_scores(q_half, kt_scr[rows_m, pl.ds(start, nk)],
                            lambda s: jnp.where(valid, s + bias, NEG_INF), ckt_scr[rows_m, :])

    def finish(h, sc):
        m, half = divmod(h, 2)
        rows_m = slice(m * LANES, (m + 1) * LANES)
        v_loc = v_ref[0, pl.ds(start, nk), rows_m]
        return _pair_finish(sc, _own_half(v_loc, half), cv_scr[half, :, rows_m], None)

    res = _software_pipeline(C_HEADS, scores, finish)
    outs = [jnp.where(low, res[2 * m], res[2 * m + 1]) for m in range(C_HEADS // 2)]
    yc = jnp.concatenate(outs, axis=1)
    cg = cg_ref[0].astype(F32)
    o_ref[0] = (yc * (cg * _sigmoid(cg))).astype(BF16)


def _nbr_bias_table(rpb):
    w = np.arange(GRID_W)
    cstart = np.clip(w - NA_COLS // 2, 0, GRID_W - NA_COLS)
    j = np.arange(GRID_W)
    in_win = (j[None, :] >= cstart[:, None]) & (j[None, :] < cstart[:, None] + NA_COLS)
    dcol = np.clip(j[None, :] - w[:, None] + NA_COLS - 1, 0, 2 * NA_COLS - 2)
    t = rpb.astype(F32)[:, :, dcol] * LOG2E
    t = jnp.where(jnp.asarray(in_win)[None, None], t, NEG_INF)
    pad = jnp.full((rpb.shape[0], 1, GRID_W, GRID_W), NEG_INF, F32)
    t = jnp.concatenate([pad, t, pad], axis=1)
    return jnp.concatenate([t[:, :-1], t[:, 1:]], axis=-1)


def _lat_nbr_attn(u, cache_k, cache_v, layer, t2, *, nb, lb):
    grid_rows = lb // GRID_W
    assert grid_rows >= NBR_BAND and grid_rows % NBR_QROWS == 0 and NBR_QROWS * GRID_W == LANES
    nq = NBR_QROWS * GRID_W
    n_ctx = cache_k.shape[2]
    u3 = u.reshape(nb, lb, NAT_W)
    ctx_spec = pl.BlockSpec((1, 1, n_ctx, C_WIDTH), lambda b, i: (b, layer, 0, 0))
    kernel = functools.partial(_lat_nbr_kernel, grid_rows=grid_rows)
    once = pl.Buffered(1)
    return pl.pallas_call(
        kernel,
        grid=(nb, grid_rows // NBR_QROWS),
        in_specs=[
            pl.BlockSpec((1, nq, C_WIDTH), lambda b, i: (b, i, U_CQ // C_WIDTH)),
            pl.BlockSpec((1, lb, C_WIDTH), lambda b, i: (b, 0, U_CK // C_WIDTH), pipeline_mode=once),
            pl.BlockSpec((1, lb, C_WIDTH), lambda b, i: (b, 0, U_CV // C_WIDTH)),
            pl.BlockSpec((1, nq, C_WIDTH), lambda b, i: (b, i, U_CG // C_WIDTH)),
            ctx_spec, ctx_spec,
            pl.BlockSpec(t2.shape, lambda b, i: (0, 0, 0, 0), pipeline_mode=once),
        ],
        out_specs=pl.BlockSpec((1, nq, C_WIDTH), lambda b, i: (b, i, 0)),
        out_shape=jax.ShapeDtypeStruct((nb, lb, C_WIDTH), BF16),
        scratch_shapes=[pltpu.VMEM((C_WIDTH, lb), BF16),
                        pltpu.VMEM((C_WIDTH, n_ctx), BF16),
                        pltpu.VMEM((2, n_ctx, C_WIDTH), BF16)],
        compiler_params=_cparams(("arbitrary", "arbitrary")),
        name="lat_nbr_attn",
    )(u3, u3, u3, u3, cache_k, cache_v, t2)


HYENA_EMB_PAD = 40


def _filter_kernel(w1t_ref, b1_ref, fr_ref, w2t_ref, b2_ref, w3t_ref, decay_ref, o_ref, z_scr):
    L = o_ref.shape[1]

    @pl.when(pl.program_id(0) == 0)
    def _():
        t = lax.broadcasted_iota(jnp.int32, (HYENA_EMB_PAD, L), 1).astype(F32) / L
        r = lax.broadcasted_iota(jnp.int32, (HYENA_EMB_PAD, L), 0)
        band = ((r - 1) % HYENA_BANDS + 1).astype(F32) * (2.0 * np.pi)
        ang = t * band
        feats = jnp.where(r == 0, t, jnp.where(r <= HYENA_BANDS, jnp.sin(ang), jnp.cos(ang)))
        z = jnp.dot(w1t_ref[...], feats, precision=HIGHEST, preferred_element_type=F32) + b1_ref[...]
        z = jnp.sin(fr_ref[0] * z)
        z = jnp.dot(w2t_ref[...], z, precision=HIGHEST, preferred_element_type=F32) + b2_ref[...]
        z_scr[...] = jnp.sin(fr_ref[1] * z)

    h = jnp.dot(w3t_ref[...], z_scr[...], precision=HIGHEST, preferred_element_type=F32)
    t_row = lax.broadcasted_iota(jnp.int32, (1, L), 1).astype(F32) / L
    o_ref[...] = h * jnp.exp(-jnp.abs(decay_ref[...]) * t_row)


def _hyena_filters(L, hy_w1, hy_b1, hy_w2, hy_b2, hy_freq, hy_w3, hy_decay):
    hidden = hy_w1.shape[1]
    n_out = hy_w3.shape[1]
    rb = 256
    w1t = jnp.pad(hy_w1.T, ((0, 0), (0, HYENA_EMB_PAD - hy_w1.shape[0])))
    const = lambda a: pl.BlockSpec(a.shape, lambda i: (0,) * a.ndim)
    args = (w1t, hy_b1.reshape(hidden, 1), hy_freq.reshape(2, hidden, 1), hy_w2.T, hy_b2.reshape(hidden, 1))
    out = pl.pallas_call(
        _filter_kernel,
        grid=(n_out // rb,),
        in_specs=[const(a) for a in args] + [
            pl.BlockSpec((rb, hidden), lambda i: (i, 0)),
            pl.BlockSpec((rb, 1), lambda i: (i, 0)),
        ],
        out_specs=pl.BlockSpec((rb, L), lambda i: (i, 0)),
        out_shape=jax.ShapeDtypeStruct((n_out, L), F32),
        scratch_shapes=[pltpu.VMEM((hidden, L), F32)],
        compiler_params=_cparams(("arbitrary",)),
        name="hyena_filter",
    )(*args, hy_w3.T, hy_decay.reshape(n_out, 1))
    return out.reshape(HYENA_ORDER, 2, B_WIDTH, L)


def _short_conv_params(hy_conv_w, hy_conv_b):
    return jnp.concatenate([hy_conv_w.T, hy_conv_b[:, None]], axis=1)


def _dense_dft_consts(S):
    n = 2 * S
    t = np.arange(S)[:, None]
    k = np.arange(n)[None, :]
    ang = -2.0 * np.pi * t * k / n
    cr, ci = np.cos(ang), np.sin(ang)
    wf = np.block([[cr, ci], [-ci, cr]])
    er, ei = cr.T, -ci.T
    wi = np.block([[er, ei], [-ei, er]])
    return jnp.asarray(wf, F32), jnp.asarray(wi, F32)


def _ctx_spec_kernel(f_ref, skip_ref, wf_ref, hr_ref, hi_ref, *, S):
    n = 2 * S
    fwd = f_ref[0, 0]
    bwd = f_ref[0, 1]
    lane = lax.broadcasted_iota(jnp.int32, bwd.shape, 1)
    bwd = jnp.where(lane == 0, 0.0, bwd)
    w = wf_ref[0:S, :]
    ff = jnp.dot(fwd, w, precision=HIGHEST, preferred_element_type=F32)
    fb = jnp.dot(bwd, w, precision=HIGHEST, preferred_element_type=F32)
    skip = skip_ref[0]
    hr_ref[0] = (ff[:, :n] + fb[:, :n] + skip) * (1.0 / n)
    hi_ref[0] = (ff[:, n:] - fb[:, n:]) * (1.0 / n)


def _ctx_spectrum(filt, skip, wf, S):
    n = 2 * S
    kernel = functools.partial(_ctx_spec_kernel, S=S)
    return pl.pallas_call(
        kernel,
        grid=(HYENA_ORDER,),
        in_specs=[
            pl.BlockSpec((1, 2, B_WIDTH, S), lambda o: (o, 0, 0, 0)),
            pl.BlockSpec((1, B_WIDTH, 1), lambda o: (o, 0, 0)),
            pl.BlockSpec(wf.shape, lambda o: (0, 0)),
        ],
        out_specs=[pl.BlockSpec((1, B_WIDTH, n), lambda o: (o, 0, 0))] * 2,
        out_shape=[jax.ShapeDtypeStruct((HYENA_ORDER, B_WIDTH, n), F32)] * 2,
        compiler_params=_cparams(("arbitrary",)),
        name="hyena_ctx_spectrum",
    )(filt, skip.reshape(HYENA_ORDER, B_WIDTH, 1), wf)


def _hyena_ctx_kernel(prm_ref, v_ref, x1_ref, x2_ref, bg_ref, hr_ref, hi_ref, wf_ref, wi_ref, o_ref, *, S):
    nb, cb, _ = v_ref.shape
    half = nb // 2
    n = 2 * S
    lane = lax.broadcasted_iota(jnp.int32, (nb, cb, S), 2)

    def short_conv(u, p):
        prev = jnp.where(lane == 0, 0.0, pltpu.roll(u, 1, 2))
        nxt = jnp.where(lane == S - 1, 0.0, pltpu.roll(u, S - 1, 2))
        return p[:, 3:4] + prev * p[:, 0:1] + u * p[:, 1:2] + nxt * p[:, 2:3]

    def stack(u):
        return jnp.concatenate([u[:half], u[half:]], axis=-1)

    def conv(xs, o):
        spec = jnp.dot(xs.reshape(half * cb, 2 * S).astype(BF16), wf_ref[...],
                       preferred_element_type=F32).reshape(half, cb, 2 * n)
        xr, xi = spec[..., :n], spec[..., n:]
        hr, hi = hr_ref[o], hi_ref[o]
        y = jnp.concatenate([xr * hr - xi * hi, xr * hi + xi * hr], axis=-1)
        return jnp.dot(y.reshape(half * cb, 2 * n).astype(BF16), wi_ref[...],
                       preferred_element_type=F32).reshape(half, cb, 2 * S)

    v = stack(short_conv(v_ref[...], prm_ref[0]))
    x1 = stack(short_conv(x1_ref[...], prm_ref[1]))
    x2 = stack(short_conv(x2_ref[...], prm_ref[2]))
    z = x1 * conv(v, 0)
    y = x2 * conv(z, 1)
    bg = stack(bg_ref[...])
    y = y * (bg * _sigmoid(bg))
    o_ref[0:half] = y[..., :S]
    o_ref[half:nb] = y[..., S:]


def _hyena_ctx(t_arr, prm, hr, hi, wf, wi, *, nb, S):
    cb = 16
    n = 2 * S
    ncb = B_WIDTH // cb
    kernel = functools.partial(_hyena_ctx_kernel, S=S)

    def part(k):
        return pl.BlockSpec((nb, cb, S), lambda c: (0, c + k * ncb, 0))

    return pl.pallas_call(
        kernel,
        grid=(ncb,),
        in_specs=[
            pl.BlockSpec((3, cb, 4), lambda c: (0, c, 0)),
            part(0), part(1), part(2), part(3),
            pl.BlockSpec((HYENA_ORDER, cb, n), lambda c: (0, c, 0)),
            pl.BlockSpec((HYENA_ORDER, cb, n), lambda c: (0, c, 0)),
            pl.BlockSpec(wf.shape, lambda c: (0, 0)),
            pl.BlockSpec(wi.shape, lambda c: (0, 0)),
        ],
        out_specs=pl.BlockSpec((nb, cb, S), lambda c: (0, c, 0)),
        out_shape=jax.ShapeDtypeStruct((nb, B_WIDTH, S), F32),
        compiler_params=_cparams(("arbitrary",)),
        name="hyena_ctx",
    )(prm.reshape(3, B_WIDTH, 4), t_arr, t_arr, t_arr, t_arr, hr, hi, wf.astype(BF16), wi.astype(BF16))


def _two_stage_consts(L, nseq, dtype):
    n2 = LANES
    n = 2 * L
    n1 = n // n2
    h1 = n1 // 2
    k1 = np.arange(n1)[:, None]
    a = -2.0 * np.pi * k1 * np.arange(h1)[None, :] / n1
    w1r, w1i = np.cos(a), np.sin(a)
    w1big = np.block([[w1r, -w1i], [w1i, w1r]])
    vr, vi = w1r.T, -w1i.T
    w1inv = np.block([[vr, -vi], [vi, vr]])
    a = -2.0 * np.pi * k1 * np.arange(n2)[None, :] / n
    twr, twi = np.cos(a), np.sin(a)
    a = -2.0 * np.pi * np.arange(n2)[:, None] * np.arange(n2)[None, :] / n2
    w2r, w2i = np.cos(a), np.sin(a)
    w2big = np.block([[w2r, w2i], [-w2i, w2r]])
    w2conj = np.block([[w2r, -w2i], [w2i, w2r]])
    f = lambda x: jnp.asarray(x, F32)
    m = lambda x: jnp.asarray(x, F32).astype(dtype)
    return dict(
        w1b=jnp.broadcast_to(m(w1big)[None], (nseq,) + w1big.shape),
        w1ib=jnp.broadcast_to(m(w1inv)[None], (nseq,) + w1inv.shape),
        w2=m(w2big), w2c=m(w2conj), twr=f(twr), twi=f(twi), n1=n1, h1=h1)


def _dft_dot(spec, a, b):
    if a.dtype == BF16 or b.dtype == BF16:
        return jnp.einsum(spec, a.astype(BF16), b.astype(BF16), preferred_element_type=F32)
    a_hi, b_hi = a.astype(BF16), b.astype(BF16)
    a_lo = (a - a_hi.astype(F32)).astype(BF16)
    b_lo = (b - b_hi.astype(F32)).astype(BF16)
    mm = lambda x, y: jnp.einsum(spec, x, y, preferred_element_type=F32)
    return mm(a_hi, b_hi) + (mm(a_hi, b_lo) + mm(a_lo, b_hi))


def _stage_fwd(xs, w1b, twr, twi, w2):
    s = xs.shape[0]
    n1 = twr.shape[0]
    a = _dft_dot('smk,skn->smn', w1b, xs)
    ar, ai = a[:, :n1], a[:, n1:]
    p = jnp.concatenate([ar * twr - ai * twi, ar * twi + ai * twr], axis=-1)
    return _dft_dot('mk,kn->mn', p.reshape(s * n1, 2 * LANES), w2)


def _stage_inv(cm, w1ib, twr, twi, w2c, s):
    n1 = twr.shape[0]
    dm = _dft_dot('mk,kn->mn', cm, w2c).reshape(s, n1, 2 * LANES)
    dr, di = dm[..., :LANES], dm[..., LANES:]
    r = jnp.concatenate([dr * twr + di * twi, di * twr - dr * twi], axis=1)
    return _dft_dot('smk,skn->smn', w1ib, r)


def _lat_spec_kernel(skip_ref, f_ref, w1b_ref, twr_ref, twi_ref, w2_ref, hr_ref, hi_ref, *, n_fft):
    cb, h1 = f_ref.shape[2], f_ref.shape[3]
    n1 = twr_ref.shape[0]
    fwd = f_ref[0, 0]
    bwd = f_ref[0, 1]
    first = (lax.broadcasted_iota(jnp.int32, bwd.shape, 1) == 0) & \
            (lax.broadcasted_iota(jnp.int32, bwd.shape, 2) == 0)
    bwd = jnp.where(first, 0.0, bwd)
    xs = jnp.concatenate([fwd, bwd], axis=0)
    xs = jnp.concatenate([xs, jnp.zeros_like(xs)], axis=1)
    sp = _stage_fwd(xs, w1b_ref[...], twr_ref[...], twi_ref[...], w2_ref[...]).reshape(2, cb, n1, 2 * LANES)
    inv = 1.0 / n_fft
    for c in range(cb):
        skip = skip_ref[pl.program_id(0), pl.program_id(1) * cb + c]
        hr_ref[0, c] = (sp[0, c, :, :LANES] + sp[1, c, :, :LANES] + skip) * inv
        hi_ref[0, c] = (sp[0, c, :, LANES:] - sp[1, c, :, LANES:]) * inv


def _lat_spectrum(filt, skip, L):
    cb = 8
    cs = _two_stage_consts(L, 2 * cb, F32)
    n1, h1 = cs["n1"], cs["h1"]
    filt5 = filt.reshape(HYENA_ORDER, 2, B_WIDTH, h1, LANES)
    kernel = functools.partial(_lat_spec_kernel, n_fft=2 * L)
    const = lambda a: pl.BlockSpec(a.shape, lambda o, c: (0,) * a.ndim)
    return pl.pallas_call(
        kernel,
        grid=(HYENA_ORDER, B_WIDTH // cb),
        in_specs=[
            pl.BlockSpec(memory_space=pltpu.SMEM),
            pl.BlockSpec((1, 2, cb, h1, LANES), lambda o, c: (o, 0, c, 0, 0)),
            const(cs["w1b"]), const(cs["twr"]), const(cs["twi"]), const(cs["w2"]),
        ],
        out_specs=[pl.BlockSpec((1, cb, n1, LANES), lambda o, c: (o, c, 0, 0))] * 2,
        out_shape=[jax.ShapeDtypeStruct((HYENA_ORDER, B_WIDTH, n1, LANES), F32)] * 2,
        compiler_params=_cparams(("arbitrary", "arbitrary")),
        name="hyena_lat_spectrum",
    )(skip, filt5, cs["w1b"], cs["twr"], cs["twi"], cs["w2"])


def _hyena_lat_kernel(prm_ref, v_ref, x1_ref, x2_ref, bg_ref, hr_ref, hi_ref,
                      w1b_ref, w1ib_ref, twr_ref, twi_ref, w2_ref, w2c_ref, o_ref,
                      vs_ref, x1s_ref, x2s_ref):
    nb, cb, h1, _ = v_ref.shape
    half = nb // 2
    s = cb * half
    n1 = twr_ref.shape[0]
    c0 = pl.program_id(0) * cb
    row = lax.broadcasted_iota(jnp.int32, (nb, h1, LANES), 1)
    lane = lax.broadcasted_iota(jnp.int32, (nb, h1, LANES), 2)

    def short_conv(u, part, c):
        r = pltpu.roll(u, 1, 2)
        prev = jnp.where(lane == 0, pltpu.roll(r, 1, 1), r)
        prev = jnp.where((lane == 0) & (row == 0), 0.0, prev)
        r = pltpu.roll(u, LANES - 1, 2)
        nxt = jnp.where(lane == LANES - 1, pltpu.roll(r, h1 - 1, 1), r)
        nxt = jnp.where((lane == LANES - 1) & (row == h1 - 1), 0.0, nxt)
        ch = part * B_WIDTH + c0 + c
        return prm_ref[3, ch] + prev * prm_ref[0, ch] + u * prm_ref[1, ch] + nxt * prm_ref[2, ch]

    def stack_into(dst_ref, u, c):
        dst_ref[c * half:(c + 1) * half, 0:h1, :] = u[:half]
        dst_ref[c * half:(c + 1) * half, h1:2 * h1, :] = u[half:]

    for c in range(cb):
        stack_into(vs_ref, short_conv(v_ref[:, c], 0, c), c)
        stack_into(x1s_ref, short_conv(x1_ref[:, c], 1, c), c)
        bg = bg_ref[:, c]
        stack_into(x2s_ref, short_conv(x2_ref[:, c], 2, c) * (bg * _sigmoid(bg)), c)

    twr, twi = twr_ref[...], twi_ref[...]

    def conv(xs, o):
        sp = _stage_fwd(xs, w1b_ref[...], twr, twi, w2_ref[...]).reshape(cb, half, n1, 2 * LANES)
        br, bi = sp[..., :LANES], sp[..., LANES:]
        hr = hr_ref[o][:, None]
        hi = hi_ref[o][:, None]
        cm = jnp.concatenate([br * hr - bi * hi, br * hi + bi * hr], axis=-1).reshape(s * n1, 2 * LANES)
        return _stage_inv(cm, w1ib_ref[...], twr, twi, w2c_ref[...], s)

    z = x1s_ref[...] * conv(vs_ref[...], 0)
    y = x2s_ref[...] * conv(z, 1)
    for c in range(cb):
        o_ref[0:half, c] = y[c * half:(c + 1) * half, 0:h1]
        o_ref[half:nb, c] = y[c * half:(c + 1) * half, h1:2 * h1]


def _hyena_lat(t_arr, prm, hr, hi, *, nb, L):
    cb = 8
    half = nb // 2
    s = cb * half
    cs = _two_stage_consts(L, s, BF16)
    n1, h1 = cs["n1"], cs["h1"]
    ncb = B_WIDTH // cb
    t5 = t_arr.reshape(nb, T_W, h1, LANES)

    def part(k):
        return pl.BlockSpec((nb, cb, h1, LANES), lambda c: (0, c + k * ncb, 0, 0))

    const = lambda a: pl.BlockSpec(a.shape, lambda c: (0,) * a.ndim)
    out = pl.pallas_call(
        _hyena_lat_kernel,
        grid=(ncb,),
        in_specs=[
            pl.BlockSpec(memory_space=pltpu.SMEM),
            part(0), part(1), part(2), part(3),
            pl.BlockSpec((HYENA_ORDER, cb, n1, LANES), lambda c: (0, c, 0, 0)),
            pl.BlockSpec((HYENA_ORDER, cb, n1, LANES), lambda c: (0, c, 0, 0)),
            const(cs["w1b"]), const(cs["w1ib"]), const(cs["twr"]), const(cs["twi"]),
            const(cs["w2"]), const(cs["w2c"]),
        ],
        out_specs=pl.BlockSpec((nb, cb, h1, LANES), lambda c: (0, c, 0, 0)),
        out_shape=jax.ShapeDtypeStruct((nb, B_WIDTH, h1, LANES), F32),
        scratch_shapes=[pltpu.VMEM((s, 2 * h1, LANES), F32)] * 3,
        compiler_params=_cparams(("arbitrary",)),
        name="hyena_lat",
    )(prm.T, t5, t5, t5, t5, hr, hi,
      cs["w1b"], cs["w1ib"], cs["twr"], cs["twi"], cs["w2"], cs["w2c"])
    return out.reshape(nb, B_WIDTH, L)


def _merge_kernel(x_ref, ya_ref, ybt_ref, yc_ref, mg_ref, gate_ref, wa_ref, wb_ref, wc_ref, wo_ref,
                  fw_ref, o_ref, *, final):
    mg = mg_ref[...]
    yb = ybt_ref[0].T.astype(BF16)
    m = (_sigmoid(mg[:, :D_MODEL].astype(F32)) * jnp.dot(ya_ref[...], wa_ref[...], preferred_element_type=F32)
         + _sigmoid(mg[:, D_MODEL:2 * D_MODEL].astype(F32)) * jnp.dot(yb, wb_ref[...], preferred_element_type=F32)
         + _sigmoid(mg[:, 2 * D_MODEL:].astype(F32)) * jnp.dot(yc_ref[...], wc_ref[...], preferred_element_type=F32))
    out = jnp.dot(m.astype(BF16), wo_ref[...], preferred_element_type=F32)
    xn = x_ref[...] + gate_ref[0] * out
    if final:
        var = jnp.mean(xn * xn, axis=-1, keepdims=True)
        xn = xn * lax.rsqrt(var + EPS) * fw_ref[...]
    o_ref[...] = xn


def _merge(x2d, ya, ybt, yc, u, gate, wa, wb, wc, wo, fw, *, nb, lb, tm, final):
    rows = nb * lb
    bpb = lb // tm
    per_mod = gate.shape[0] > 1
    kernel = functools.partial(_merge_kernel, final=final)
    const = lambda a: pl.BlockSpec(a.shape, lambda i: (0,) * a.ndim, pipeline_mode=pl.Buffered(1))
    return pl.pallas_call(
        kernel,
        grid=(rows // tm,),
        in_specs=[
            pl.BlockSpec((tm, D_MODEL), lambda i: (i, 0)),
            pl.BlockSpec((tm, A_WIDTH), lambda i: (i, 0)),
            pl.BlockSpec((1, B_WIDTH, tm), lambda i: (i // bpb, 0, i % bpb)),
            pl.BlockSpec((tm, C_WIDTH), lambda i: (i, 0)),
            pl.BlockSpec((tm, MG_W), lambda i: (i, U_MG // MG_W)),
            pl.BlockSpec((1, 1, D_MODEL), lambda i: ((i // bpb) if per_mod else 0, 0, 0)),
            const(wa), const(wb), const(wc), const(wo),
            pl.BlockSpec((1, D_MODEL), lambda i: (0, 0)),
        ],
        out_specs=pl.BlockSpec((tm, D_MODEL), lambda i: (i, 0)),
        out_shape=jax.ShapeDtypeStruct((rows, D_MODEL), F32),
        compiler_params=_cparams(("arbitrary",)),
        name="merge_final" if final else "merge",
    )(x2d, ya.reshape(rows, A_WIDTH), ybt, yc.reshape(rows, C_WIDTH), u, gate, wa, wb, wc, wo,
      fw.reshape(1, D_MODEL))


def _rope_tables(L):
    t = np.arange(L)
    row = (t // GRID_W).astype(np.float32)
    col = (t % GRID_W).astype(np.float32)
    nf = HEAD_DIM // 4
    inv = np.power(np.float32(ROPE_BASE), -np.arange(nf, dtype=np.float32) / nf).astype(np.float32)
    ang = np.concatenate([row[:, None] * inv[None], col[:, None] * inv[None]], axis=-1)
    cos, sin = np.cos(ang), np.sin(ang)
    reps = LANES // HEAD_DIM
    cos_t = np.tile(np.concatenate([cos, cos], axis=-1), (1, reps))
    sin_t = np.tile(np.concatenate([-sin, sin], axis=-1), (1, reps))
    return jnp.asarray(cos_t, F32), jnp.asarray(sin_t, F32)


def _perm_heads(w, base, axis):
    return [lax.slice_in_dim(w, base + h * HEAD_DIM, base + (h + 1) * HEAD_DIM, axis=axis) for h in A_PERM]


def _regroup_w_in(w):
    w_nat = jnp.concatenate(
        [w[:, IN_MG:], w[:, IN_CQ:IN_MG]]
        + _perm_heads(w, IN_AQ, 1) + _perm_heads(w, IN_AG, 1)
        + [w[:, IN_AK:IN_AV], w[:, IN_AV:IN_AG]], axis=1).astype(BF16)
    return w_nat, w[:, IN_BU:IN_CQ].T.astype(BF16)


def kernel(x_prompt, x_sample, c, cache_a_k, cache_a_v, cache_c_k, cache_c_v, c_ctx, norm_w, w_ada, b_ada, w_in, a_sink, hy_conv_w, hy_conv_b, hy_w1, hy_b1, hy_w2, hy_b2, hy_freq, hy_w3, hy_decay, hy_skip, c_rpb, w_up_a, w_up_b, w_up_c, w_out, final_norm_w):
    nbc, S, _ = x_prompt.shape
    nbl, L, _ = x_sample.shape
    depth = w_in.shape[0]
    n_ctx = cache_a_k.shape[2]
    assert nbc % 2 == 0 and nbl % 2 == 0 and L % (GRID_W * LANES // 2) == 0

    pad = (-(nbl + 1)) % 8
    cond = jnp.concatenate([c, c_ctx[None], jnp.zeros((pad, D_MODEL), F32)], axis=0)
    mod = _adaln(cond, w_ada, b_ada)

    cos_t, sin_t = _rope_tables(L)
    tm_ctx = 512 if (nbc * S) % 512 == 0 else S
    zeros_t = jnp.zeros((tm_ctx, LANES), F32)
    wf_c, wi_c = _dense_dft_consts(S)
    ca_k = cache_a_k.reshape(nbl, depth, n_ctx, A_KV_WIDTH)
    ca_v = cache_a_v.reshape(nbl, depth, n_ctx, A_KV_WIDTH)
    cc_k = cache_c_k.reshape(nbl, depth, n_ctx, C_WIDTH)
    cc_v = cache_c_v.reshape(nbl, depth, n_ctx, C_WIDTH)

    xp = x_prompt.reshape(nbc * S, D_MODEL)
    xs = x_sample.reshape(nbl * L, D_MODEL)
    aks, avs, cks, cvs = [], [], [], []
    for l in range(depth):
        final = l == depth - 1
        w_nat, w_t = _regroup_w_in(w_in[l])
        wa = jnp.concatenate(_perm_heads(w_up_a[l], 0, 0), axis=0).astype(BF16)
        wb, wc, wo = (w.astype(BF16) for w in (w_up_b[l], w_up_c[l], w_out[l]))
        prm = _short_conv_params(hy_conv_w[l], hy_conv_b[l])
        filt_args = (hy_w1[l], hy_b1[l], hy_w2[l], hy_b2[l], hy_freq[l], hy_w3[l], hy_decay[l])
        t2 = _nbr_bias_table(c_rpb[l])

        def mod_parts(rows):
            m = mod[l, rows][:, None, :]
            return 1.0 + m[..., D_MODEL:2 * D_MODEL], m[..., :D_MODEL], m[..., 2 * D_MODEL:]

        sc, sh, gate = mod_parts(slice(nbl, nbl + 1))
        u, t_arr = _inproj(xp, sc, sh, norm_w[l], w_nat, w_t, zeros_t, zeros_t,
                           nb=nbc, lb=S, tm=tm_ctx, rope=False, u_dtype=F32)
        aks.append(u[:, U_AK:U_AV].reshape(nbc, S, A_KV_HEADS, HEAD_DIM))
        avs.append(u[:, U_AV:NAT_W].reshape(nbc, S, A_KV_HEADS, HEAD_DIM))
        cks.append(u[:, U_CK:U_CV].reshape(nbc, S, C_HEADS, HEAD_DIM))
        cvs.append(u[:, U_CV:U_CG].reshape(nbc, S, C_HEADS, HEAD_DIM))
        ya, yc = _ctx_attn(a_sink[l], u, nb=nbc, lb=S)
        hr, hi = _ctx_spectrum(_hyena_filters(S, *filt_args), hy_skip[l], wf_c, S)
        ybt = _hyena_ctx(t_arr, prm, hr, hi, wf_c, wi_c, nb=nbc, S=S)
        xp = _merge(xp, ya, ybt, yc, u, gate, wa, wb, wc, wo, final_norm_w,
                    nb=nbc, lb=S, tm=S, final=final)

        sc, sh, gate = mod_parts(slice(0, nbl))
        u, t_arr = _inproj(xs, sc, sh, norm_w[l], w_nat, w_t, cos_t, sin_t,
                           nb=nbl, lb=L, tm=1024, rope=True, u_dtype=BF16)
        ya = _lat_win_attn(a_sink[l], u, ca_k, ca_v, l, nb=nbl, lb=L)
        yc = _lat_nbr_attn(u, cc_k, cc_v, l, t2, nb=nbl, lb=L)
        hr, hi = _lat_spectrum(_hyena_filters(L, *filt_args), hy_skip[l], L)
        ybt = _hyena_lat(t_arr, prm, hr, hi, nb=nbl, L=L)
        xs = _merge(xs, ya, ybt, yc, u, gate, wa, wb, wc, wo, final_norm_w,
                    nb=nbl, lb=L, tm=512, final=final)

    y_prompt = xp.reshape(nbc, S, D_MODEL)
    y_sample = xs.reshape(nbl, L, D_MODEL)
    return (y_prompt, y_sample, jnp.stack(aks, axis=1), jnp.stack(avs, axis=1),
            jnp.stack(cks, axis=1), jnp.stack(cvs, axis=1))
```

```python
import functools

import numpy as np
import jax
import jax.numpy as jnp
from jax import lax
from jax.experimental import pallas as pl
from jax.experimental.pallas import tpu as pltpu

F32 = jnp.float32
BF16 = jnp.bfloat16
HIGHEST = lax.Precision.HIGHEST

D_MODEL = 2048
HEAD_DIM = 64
A_HEADS = 12
A_KV_HEADS = 4
A_GROUP = A_HEADS // A_KV_HEADS
A_WIDTH = A_HEADS * HEAD_DIM
A_KV_WIDTH = A_KV_HEADS * HEAD_DIM
A_WINDOW = 128
B_WIDTH = 512
HYENA_ORDER = 2
HYENA_BANDS = 16
C_HEADS = 12
C_WIDTH = C_HEADS * HEAD_DIM
GRID_W = 64
NA_ROWS = 8
NA_COLS = 16
ROPE_BASE = 10000.0
EPS = 1e-6
NEG_INF = -1e30
ATTN_SCALE = HEAD_DIM ** -0.5
LOG2E = 1.4426950408889634
QK_SCALE = ATTN_SCALE * LOG2E

LANES = 128
VMEM_LIMIT = 56 * 1024 * 1024

IN_AQ, IN_AK, IN_AV, IN_AG = 0, 768, 1024, 1280
IN_BU, IN_CQ, IN_MG = 2048, 4096, 7168
PROJ_TN = 1024
MG_W = 3 * D_MODEL
U_MG = 0
U_CQ = U_MG + MG_W
U_CK, U_CV, U_CG = U_CQ + C_WIDTH, U_CQ + 2 * C_WIDTH, U_CQ + 3 * C_WIDTH
U_AQ = U_CG + C_WIDTH
U_AG = U_AQ + A_WIDTH
U_AK = U_AG + A_WIDTH
U_AV = U_AK + A_KV_WIDTH
NAT_W = U_AV + A_KV_WIDTH
T_W = 4 * B_WIDTH

A_PERM = tuple(2 * A_GROUP * m + A_GROUP * half + j
               for m in range(A_KV_HEADS // 2) for j in range(A_GROUP) for half in range(2))


def _cparams(sem):
    return pltpu.CompilerParams(dimension_semantics=sem, vmem_limit_bytes=VMEM_LIMIT)


def _sigmoid(x):
    return 1.0 / (1.0 + jnp.exp(-x))


def _lane_chunks(off, width):
    blk, rem = divmod(off, PROJ_TN)
    assert rem % LANES == 0 and width % LANES == 0 and rem + width <= PROJ_TN
    return blk, rem // LANES, (rem + width) // LANES


def _adaln_kernel(cond_ref, w_ref, b_ref, o_ref):
    c = cond_ref[...]
    s = (c * _sigmoid(c)).astype(BF16)
    acc = jnp.dot(s, w_ref[0].astype(BF16), preferred_element_type=F32)
    o_ref[0] = acc + b_ref[0]


def _adaln(cond, w_ada, b_ada):
    depth = w_ada.shape[0]
    rows = cond.shape[0]
    tn = 1024
    return pl.pallas_call(
        _adaln_kernel,
        grid=(depth, 3 * D_MODEL // tn),
        in_specs=[
            pl.BlockSpec((rows, D_MODEL), lambda l, j: (0, 0)),
            pl.BlockSpec((1, D_MODEL, tn), lambda l, j: (l, 0, j)),
            pl.BlockSpec((1, 1, tn), lambda l, j: (l, 0, j)),
        ],
        out_specs=pl.BlockSpec((1, rows, tn), lambda l, j: (l, 0, j)),
        out_shape=jax.ShapeDtypeStruct((depth, rows, 3 * D_MODEL), F32),
        compiler_params=_cparams(("arbitrary", "arbitrary")),
        name="adaln",
    )(cond, w_ada, b_ada.reshape(depth, 1, 3 * D_MODEL))


def _inproj_kernel(x_ref, sc_ref, sh_ref, nw_ref, w_ref, wt_ref, cos_ref, sin_ref,
                   u_ref, t_ref, h_ref, *, rope, tm, lb):
    j = pl.program_id(1)
    tn = PROJ_TN
    n_nat = NAT_W // tn

    @pl.when(j == 0)
    def _():
        x = x_ref[...]
        var = jnp.mean(x * x, axis=-1, keepdims=True)
        y = x * lax.rsqrt(var + EPS) * nw_ref[...]
        h_ref[...] = (y * sc_ref[0] + sh_ref[0]).astype(BF16)

    def rotate(a):
        lane = lax.broadcasted_iota(jnp.int32, (tm, LANES), 1)
        first = (lane % HEAD_DIM) < (HEAD_DIM // 2)
        partner = jnp.where(first, pltpu.roll(a, LANES - HEAD_DIM // 2, 1), pltpu.roll(a, HEAD_DIM // 2, 1))
        return a * cos_ref[...] + partner * sin_ref[...]

    def nat_step(chunk_ops):
        acc = jnp.dot(h_ref[...], w_ref[...], preferred_element_type=F32)
        if not chunk_ops:
            u_ref[...] = acc.astype(u_ref.dtype)
            return
        for ci in range(tn // LANES):
            a = acc[:, ci * LANES:(ci + 1) * LANES]
            for op in chunk_ops.get(ci, ()):
                a = rotate(a) if op == "rope" else a * QK_SCALE
            u_ref[:, ci * LANES:(ci + 1) * LANES] = a.astype(u_ref.dtype)

    special = {}
    for off, width, ops in ((U_CQ, C_WIDTH, ("scale",)),
                            (U_AQ, A_WIDTH, ("rope", "scale") if rope else ("scale",)),
                            (U_AK, A_KV_WIDTH, ("rope",) if rope else ())):
        if ops:
            blk, c0, c1 = _lane_chunks(off, width)
            special.setdefault(blk, {}).update({ci: ops for ci in range(c0, c1)})

    plain = j < n_nat
    for blk, chunk_ops in special.items():
        plain = plain & (j != blk)
        pl.when(j == blk)(functools.partial(nat_step, chunk_ops))
    pl.when(plain)(functools.partial(nat_step, None))

    @pl.when(j >= n_nat)
    def _():
        acc_t = lax.dot_general(wt_ref[...], h_ref[...], (((1,), (1,)), ((), ())),
                                preferred_element_type=F32)
        if tm <= lb:
            t_ref[0] = acc_t
        else:
            for k in range(tm // lb):
                t_ref[k] = acc_t[:, k * lb:(k + 1) * lb]


def _inproj(x2d, sc, sh, norm_w, w_nat, w_t, cos_t, sin_t, *, layer, nb, lb, tm, rope, u_dtype):
    rows = nb * lb
    tn = PROJ_TN
    n_nat, n_t = NAT_W // tn, T_W // tn
    per_mod = sc.shape[0] > 1
    bpb = max(lb // tm, 1)

    def mod_map(i, j):
        return ((i // bpb) if per_mod else 0, 0, 0)

    if tm <= lb:
        t_spec = pl.BlockSpec((1, tn, tm), lambda i, j: (i // bpb, jnp.clip(j - n_nat, 0, n_t - 1), i % bpb))
    else:
        t_spec = pl.BlockSpec((tm // lb, tn, lb), lambda i, j: (i, jnp.clip(j - n_nat, 0, n_t - 1), 0))

    kernel = functools.partial(_inproj_kernel, rope=rope, tm=tm, lb=lb)
    return pl.pallas_call(
        kernel,
        grid=(rows // tm, n_nat + n_t),
        in_specs=[
            pl.BlockSpec((tm, D_MODEL), lambda i, j: (i, 0), pipeline_mode=pl.Buffered(1)),
            pl.BlockSpec((1, 1, D_MODEL), mod_map),
            pl.BlockSpec((1, 1, D_MODEL), mod_map),
            pl.BlockSpec((1, D_MODEL), lambda i, j: (0, 0)),
            pl.BlockSpec((None, D_MODEL, tn), lambda i, j: (layer, 0, jnp.minimum(j, n_nat - 1))),
            pl.BlockSpec((None, tn, D_MODEL), lambda i, j: (layer, jnp.maximum(j - n_nat, 0), 0)),
            pl.BlockSpec((tm, LANES), lambda i, j: (i % bpb, 0)),
            pl.BlockSpec((tm, LANES), lambda i, j: (i % bpb, 0)),
        ],
        out_specs=[
            pl.BlockSpec((tm, tn), lambda i, j: (i, jnp.minimum(j, n_nat - 1))),
            t_spec,
        ],
        out_shape=[
            jax.ShapeDtypeStruct((rows, NAT_W), u_dtype),
            jax.ShapeDtypeStruct((nb, T_W, lb), F32),
        ],
        scratch_shapes=[pltpu.VMEM((tm, D_MODEL), BF16)],
        compiler_params=_cparams(("arbitrary", "arbitrary")),
        name="inproj_rope" if rope else "inproj",
    )(x2d, sc, sh, norm_w.reshape(1, D_MODEL), w_nat, w_t, cos_t, sin_t)


def _softmax_pv(s, v, sink_col):
    m = jnp.max(s, axis=-1, keepdims=True)
    if sink_col is not None:
        m = jnp.maximum(m, sink_col)
    e = jnp.exp2(s - m)
    den = jnp.sum(e, axis=-1, keepdims=True)
    if sink_col is not None:
        den = den + jnp.exp2(sink_col - m)
    o = jnp.dot(e.astype(BF16), v, preferred_element_type=F32)
    return o * (1.0 / den)


def _sink_column(sink_ref, heads, rows_per_head):
    n = len(heads)
    row_head = lax.broadcasted_iota(jnp.int32, (n * rows_per_head, 1), 0) // rows_per_head
    col = jnp.full((n * rows_per_head, 1), sink_ref[heads[-1]], F32)
    for idx in range(n - 1):
        col = jnp.where(row_head == idx, sink_ref[heads[idx]], col)
    return col * LOG2E


def _ctx_attn_kernel(sink_ref, cc_ref, a0_ref, a1_ref, ya_ref, yc_ref):
    cc = cc_ref[0]
    a = jnp.concatenate([a0_ref[0], a1_ref[0]], axis=1)
    s_len = cc.shape[0]
    nt = (((1,), (1,)), ((), ()))
    o_aq, o_ag, o_ak, o_av = 0, U_AG - U_AQ, U_AK - U_AQ, U_AV - U_AQ

    pieces = [None] * A_HEADS
    for g in range(A_KV_HEADS):
        heads = [A_GROUP * g + hh for hh in range(A_GROUP)]
        pos = [A_PERM.index(h) for h in heads]
        q = jnp.concatenate([a[:, o_aq + p * HEAD_DIM:o_aq + (p + 1) * HEAD_DIM] for p in pos],
                            axis=0).astype(BF16)
        k = a[:, o_ak + g * HEAD_DIM:o_ak + (g + 1) * HEAD_DIM].astype(BF16)
        v = a[:, o_av + g * HEAD_DIM:o_av + (g + 1) * HEAD_DIM].astype(BF16)
        s = lax.dot_general(q, k, nt, preferred_element_type=F32)
        o = _softmax_pv(s, v, _sink_column(sink_ref, heads, s_len))
        for hh, p in enumerate(pos):
            pieces[p] = o[hh * s_len:(hh + 1) * s_len]
    ya = jnp.concatenate(pieces, axis=1)
    ag = a[:, o_ag:o_ag + A_WIDTH]
    ya_ref[0] = (ya * (ag * _sigmoid(ag))).astype(BF16)

    pieces = []
    for h in range(C_HEADS):
        q = cc[:, h * HEAD_DIM:(h + 1) * HEAD_DIM].astype(BF16)
        k = cc[:, C_WIDTH + h * HEAD_DIM:C_WIDTH + (h + 1) * HEAD_DIM].astype(BF16)
        v = cc[:, 2 * C_WIDTH + h * HEAD_DIM:2 * C_WIDTH + (h + 1) * HEAD_DIM].astype(BF16)
        s = lax.dot_general(q, k, nt, preferred_element_type=F32)
        pieces.append(_softmax_pv(s, v, None))
    yc = jnp.concatenate(pieces, axis=1)
    cg = cc[:, 3 * C_WIDTH:]
    yc_ref[0] = (yc * (cg * _sigmoid(cg))).astype(BF16)


def _ctx_attn(sink, u, *, nb, lb):
    u3 = u.reshape(nb, lb, NAT_W)
    cc_w = 4 * C_WIDTH
    return pl.pallas_call(
        _ctx_attn_kernel,
        grid=(nb,),
        in_specs=[
            pl.BlockSpec(memory_space=pltpu.SMEM),
            pl.BlockSpec((1, lb, cc_w), lambda b: (b, 0, U_CQ // cc_w)),
            pl.BlockSpec((1, lb, PROJ_TN), lambda b: (b, 0, U_AQ // PROJ_TN)),
            pl.BlockSpec((1, lb, PROJ_TN), lambda b: (b, 0, U_AQ // PROJ_TN + 1)),
        ],
        out_specs=[
            pl.BlockSpec((1, lb, A_WIDTH), lambda b: (b, 0, 0)),
            pl.BlockSpec((1, lb, C_WIDTH), lambda b: (b, 0, 0)),
        ],
        out_shape=[
            jax.ShapeDtypeStruct((nb, lb, A_WIDTH), BF16),
            jax.ShapeDtypeStruct((nb, lb, C_WIDTH), BF16),
        ],
        compiler_params=_cparams(("arbitrary",)),
        name="ctx_attn",
    )(sink, u3, u3, u3)


def _with_ones(v_pair):
    return jnp.concatenate([v_pair, jnp.ones_like(v_pair)], axis=1)


def _transpose_keys(k_ref, kt_scr, ck_ref, ckt_scr, cv_ref, cv_scr, chunk):
    n_tok = k_ref.shape[1]

    def body(c, carry):
        start = pl.multiple_of(c * chunk, chunk)
        blk = k_ref[0, pl.ds(start, chunk), :].astype(F32)
        kt_scr[:, pl.ds(start, chunk)] = blk.T.astype(BF16)
        return carry

    lax.fori_loop(0, n_tok // chunk, body, 0)
    ckt_scr[...] = ck_ref[0, 0].T.astype(BF16)
    ctx_v = cv_ref[0, 0].astype(BF16)
    for m in range(ctx_v.shape[1] // LANES):
        cv_scr[m] = _with_ones(ctx_v[:, m * LANES:(m + 1) * LANES])


PIPELINE_DEPTH = 2


def _software_pipeline(n, first_stage, second_stage):
    pending = [first_stage(h) for h in range(min(PIPELINE_DEPTH, n))]
    res = []
    for h in range(n):
        if h + PIPELINE_DEPTH < n:
            pending.append(first_stage(h + PIPELINE_DEPTH))
        res.append(second_stage(h, pending.pop(0)))
    return res


def _pair_scores(q_half, kt_loc, bias_fn, kt_ctx):
    s_loc = bias_fn(jnp.dot(q_half, kt_loc, preferred_element_type=F32))
    s_ctx = jnp.dot(q_half, kt_ctx, preferred_element_type=F32)
    return s_loc, s_ctx


def _pair_finish(scores, v_loc, v_ctx, sink_col):
    s_loc, s_ctx = scores
    m = jnp.maximum(jnp.max(s_loc, axis=-1, keepdims=True), jnp.max(s_ctx, axis=-1, keepdims=True))
    if sink_col is not None:
        m = jnp.maximum(m, sink_col)
    e_loc = jnp.exp2(s_loc - m).astype(BF16)
    e_ctx = jnp.exp2(s_ctx - m).astype(BF16)
    o = (jnp.dot(e_loc, v_loc, preferred_element_type=F32)
         + jnp.dot(e_ctx, v_ctx, preferred_element_type=F32))
    den = o[:, LANES:]
    if sink_col is not None:
        den = den + jnp.exp2(sink_col - m)
    return o[:, :LANES] * (1.0 / den)


def _lat_win_kernel(sink_ref, q_ref, k_ref, v_ref, ag_ref, ck_ref, cv_ref, o_ref,
                    kt_scr, ckt_scr, cv_scr):
    i = pl.program_id(1)
    nblk = pl.num_programs(1)
    blk = A_WINDOW
    span = 3 * blk

    @pl.when(i == 0)
    def _():
        _transpose_keys(k_ref, kt_scr, ck_ref, ckt_scr, cv_ref, cv_scr, blk)

    start = pl.multiple_of(jnp.clip(i - 1, 0, nblk - 3) * blk, blk)
    q_pos = i * blk + lax.broadcasted_iota(jnp.int32, (blk, span), 0)
    k_pos = start + lax.broadcasted_iota(jnp.int32, (blk, span), 1)
    valid = jnp.abs(q_pos - k_pos) <= A_WINDOW
    mask = lambda s: jnp.where(valid, s, NEG_INF)

    q = q_ref[0]
    lane = lax.broadcasted_iota(jnp.int32, (blk, LANES), 1)
    low = lane < HEAD_DIM

    def scores(job):
        pair, half = divmod(job, 2)
        rows_m = slice(pair // A_GROUP * LANES, (pair // A_GROUP + 1) * LANES)
        qp = q[:, pair * LANES:(pair + 1) * LANES]
        q_half = jnp.where(low if half == 0 else ~low, qp, jnp.zeros_like(qp))
        return _pair_scores(q_half, kt_scr[rows_m, pl.ds(start, span)], mask, ckt_scr[rows_m, :])

    def finish(job, sc):
        pair, half = divmod(job, 2)
        rows_m = slice(pair // A_GROUP * LANES, (pair // A_GROUP + 1) * LANES)
        v_loc = v_ref[0, pl.ds(start, span), rows_m]
        return _pair_finish(sc, _with_ones(v_loc), cv_scr[pair // A_GROUP],
                            sink_ref[A_PERM[job]] * LOG2E)

    res = _software_pipeline(A_HEADS, scores, finish)
    ya = jnp.concatenate([jnp.where(low, res[2 * p], res[2 * p + 1]) for p in range(A_HEADS // 2)], axis=1)
    ag = ag_ref[0].astype(F32)
    o_ref[0] = (ya * (ag * _sigmoid(ag))).astype(BF16)


def _lat_win_attn(sink, u, cache_k, cache_v, layer, *, nb, lb):
    blk = A_WINDOW
    nblk = lb // blk
    assert nblk >= 3
    n_ctx = cache_k.shape[2]
    u3 = u.reshape(nb, lb, NAT_W)
    ctx_spec = pl.BlockSpec((1, 1, n_ctx, A_KV_WIDTH), lambda b, i: (b, layer, 0, 0))
    return pl.pallas_call(
        _lat_win_kernel,
        grid=(nb, nblk),
        in_specs=[
            pl.BlockSpec(memory_space=pltpu.SMEM),
            pl.BlockSpec((1, blk, A_WIDTH), lambda b, i: (b, i, U_AQ // A_WIDTH)),
            pl.BlockSpec((1, lb, A_KV_WIDTH), lambda b, i: (b, 0, U_AK // A_KV_WIDTH)),
            pl.BlockSpec((1, lb, A_KV_WIDTH), lambda b, i: (b, 0, U_AV // A_KV_WIDTH)),
            pl.BlockSpec((1, blk, A_WIDTH), lambda b, i: (b, i, U_AG // A_WIDTH)),
            ctx_spec, ctx_spec,
        ],
        out_specs=pl.BlockSpec((1, blk, A_WIDTH), lambda b, i: (b, i, 0)),
        out_shape=jax.ShapeDtypeStruct((nb, lb, A_WIDTH), BF16),
        scratch_shapes=[pltpu.VMEM((A_KV_WIDTH, lb), BF16),
                        pltpu.VMEM((A_KV_WIDTH, n_ctx), BF16),
                        pltpu.VMEM((A_KV_WIDTH // LANES, n_ctx, 2 * LANES), BF16)],
        compiler_params=_cparams(("arbitrary", "arbitrary")),
        name="lat_win_attn",
    )(sink, u3, u3, u3, u3, cache_k, cache_v)


NBR_QROWS = 2
NBR_BAND = NA_ROWS + NBR_QROWS


def _lat_nbr_kernel(q_ref, k_ref, v_ref, cg_ref, ck_ref, cv_ref, t2_ref, o_ref,
                    kt_scr, ckt_scr, cv_scr, *, grid_rows):
    i = pl.program_id(1)

    @pl.when(i == 0)
    def _():
        _transpose_keys(k_ref, kt_scr, ck_ref, ckt_scr, cv_ref, cv_scr, LANES)

    r0 = NBR_QROWS * i
    nq = NBR_QROWS * GRID_W
    nk = NBR_BAND * GRID_W
    bs = jnp.clip(r0 - NA_ROWS // 2, 0, grid_rows - NBR_BAND)
    start = pl.multiple_of(bs * GRID_W, LANES)

    q_row = r0 + lax.broadcasted_iota(jnp.int32, (nq, nk), 0) // GRID_W
    rstart = jnp.clip(q_row - NA_ROWS // 2, 0, grid_rows - NA_ROWS)
    k_row = bs + lax.broadcasted_iota(jnp.int32, (nq, nk), 1) // GRID_W
    valid = (k_row >= rstart) & (k_row < rstart + NA_ROWS)

    q = q_ref[0]
    lane = lax.broadcasted_iota(jnp.int32, (nq, LANES), 1)
    low = lane < HEAD_DIM
    def scores(h):
        m, half = divmod(h, 2)
        rows_m = slice(m * LANES, (m + 1) * LANES)
        bias_rows = []
        for a in range(NBR_QROWS):
            tiles = []
            for p in range(NBR_BAND // 2):
                d = (bs + 2 * p) - (r0 + a) + (NA_ROWS - 1)
                idx = jnp.clip(d, -1, 2 * NA_ROWS - 2) + 1
                tiles.append(t2_ref[h, idx])
            bias_rows.append(jnp.concatenate(tiles, axis=1))
        bias = jnp.concatenate(bias_rows, axis=0)
        qp = q[:, rows_m]
        q_half = jnp.where(low if half == 0 else ~low, qp, jnp.zeros_like(qp))
        return _pair_scores(q_half, kt_scr[rows_m, pl.ds(start, nk)],
                            lambda s: jnp.where(valid, s + bias, NEG_INF), ckt_scr[rows_m, :])

    def finish(h, sc):
        m, half = divmod(h, 2)
        rows_m = slice(m * LANES, (m + 1) * LANES)
        v_loc = v_ref[0, pl.ds(start, nk), rows_m]
        return _pair_finish(sc, _with_ones(v_loc), cv_scr[m], None)

    res = _software_pipeline(C_HEADS, scores, finish)
    outs = [jnp.where(low, res[2 * m], res[2 * m + 1]) for m in range(C_HEADS // 2)]
    yc = jnp.concatenate(outs, axis=1)
    cg = cg_ref[0].astype(F32)
    o_ref[0] = (yc * (cg * _sigmoid(cg))).astype(BF16)


def _nbr_bias_table(rpb):
    w = np.arange(GRID_W)
    cstart = np.clip(w - NA_COLS // 2, 0, GRID_W - NA_COLS)
    j = np.arange(GRID_W)
    in_win = (j[None, :] >= cstart[:, None]) & (j[None, :] < cstart[:, None] + NA_COLS)
    dcol = j[None, :] - w[:, None] + NA_COLS - 1
    onehot = (np.arange(2 * NA_COLS - 1)[:, None, None] == dcol[None]) & in_win[None]
    t = jnp.einsum('hdc,cwj->hdwj', rpb.astype(F32) * LOG2E, jnp.asarray(onehot, F32),
                   precision=HIGHEST)
    t = jnp.where(jnp.asarray(in_win)[None, None], t, NEG_INF)
    pad = jnp.full((rpb.shape[0], 1, GRID_W, GRID_W), NEG_INF, F32)
    t = jnp.concatenate([pad, t, pad], axis=1)
    return jnp.concatenate([t[:, :-1], t[:, 1:]], axis=-1)


def _lat_nbr_attn(u, cache_k, cache_v, layer, t2, *, nb, lb):
    grid_rows = lb // GRID_W
    assert grid_rows >= NBR_BAND and grid_rows % NBR_QROWS == 0 and NBR_QROWS * GRID_W == LANES
    nq = NBR_QROWS * GRID_W
    n_ctx = cache_k.shape[2]
    u3 = u.reshape(nb, lb, NAT_W)
    ctx_spec = pl.BlockSpec((1, 1, n_ctx, C_WIDTH), lambda b, i: (b, layer, 0, 0))
    kernel = functools.partial(_lat_nbr_kernel, grid_rows=grid_rows)
    once = pl.Buffered(1)
    return pl.pallas_call(
        kernel,
        grid=(nb, grid_rows // NBR_QROWS),
        in_specs=[
            pl.BlockSpec((1, nq, C_WIDTH), lambda b, i: (b, i, U_CQ // C_WIDTH)),
            pl.BlockSpec((1, lb, C_WIDTH), lambda b, i: (b, 0, U_CK // C_WIDTH), pipeline_mode=once),
            pl.BlockSpec((1, lb, C_WIDTH), lambda b, i: (b, 0, U_CV // C_WIDTH)),
            pl.BlockSpec((1, nq, C_WIDTH), lambda b, i: (b, i, U_CG // C_WIDTH)),
            ctx_spec, ctx_spec,
            pl.BlockSpec(t2.shape, lambda b, i: (0, 0, 0, 0), pipeline_mode=once),
        ],
        out_specs=pl.BlockSpec((1, nq, C_WIDTH), lambda b, i: (b, i, 0)),
        out_shape=jax.ShapeDtypeStruct((nb, lb, C_WIDTH), BF16),
        scratch_shapes=[pltpu.VMEM((C_WIDTH, lb), BF16),
                        pltpu.VMEM((C_WIDTH, n_ctx), BF16),
                        pltpu.VMEM((C_WIDTH // LANES, n_ctx, 2 * LANES), BF16)],
        compiler_params=_cparams(("arbitrary", "arbitrary")),
        name="lat_nbr_attn",
    )(u3, u3, u3, u3, cache_k, cache_v, t2)


HYENA_EMB_PAD = 40


def _filter_kernel(w1t_ref, b1_ref, fr_ref, w2t_ref, b2_ref, w3t_ref, decay_ref, o_ref, z_scr):
    L = o_ref.shape[1]

    @pl.when(pl.program_id(0) == 0)
    def _():
        t = lax.broadcasted_iota(jnp.int32, (HYENA_EMB_PAD, L), 1).astype(F32) / L
        r = lax.broadcasted_iota(jnp.int32, (HYENA_EMB_PAD, L), 0)
        band = ((r - 1) % HYENA_BANDS + 1).astype(F32) * (2.0 * np.pi)
        ang = t * band
        feats = jnp.where(r == 0, t, jnp.where(r <= HYENA_BANDS, jnp.sin(ang), jnp.cos(ang)))
        z = jnp.dot(w1t_ref[...], feats, precision=HIGHEST, preferred_element_type=F32) + b1_ref[...]
        z = jnp.sin(fr_ref[0] * z)
        z = jnp.dot(w2t_ref[...], z, precision=HIGHEST, preferred_element_type=F32) + b2_ref[...]
        z_scr[...] = jnp.sin(fr_ref[1] * z)

    h = jnp.dot(w3t_ref[...], z_scr[...], precision=HIGHEST, preferred_element_type=F32)
    t_row = lax.broadcasted_iota(jnp.int32, (1, L), 1).astype(F32) / L
    o_ref[...] = h * jnp.exp(-jnp.abs(decay_ref[...]) * t_row)


def _hyena_filters(L, hy_w1, hy_b1, hy_w2, hy_b2, hy_freq, hy_w3, hy_decay):
    hidden = hy_w1.shape[1]
    n_out = hy_w3.shape[1]
    rb = 512
    w1t = jnp.pad(hy_w1.T, ((0, 0), (0, HYENA_EMB_PAD - hy_w1.shape[0])))
    const = lambda a: pl.BlockSpec(a.shape, lambda i: (0,) * a.ndim)
    args = (w1t, hy_b1.reshape(hidden, 1), hy_freq.reshape(2, hidden, 1), hy_w2.T, hy_b2.reshape(hidden, 1))
    out = pl.pallas_call(
        _filter_kernel,
        grid=(n_out // rb,),
        in_specs=[const(a) for a in args] + [
            pl.BlockSpec((rb, hidden), lambda i: (i, 0)),
            pl.BlockSpec((rb, 1), lambda i: (i, 0)),
        ],
        out_specs=pl.BlockSpec((rb, L), lambda i: (i, 0)),
        out_shape=jax.ShapeDtypeStruct((n_out, L), F32),
        scratch_shapes=[pltpu.VMEM((hidden, L), F32)],
        compiler_params=_cparams(("arbitrary",)),
        name="hyena_filter",
    )(*args, hy_w3.T, hy_decay.reshape(n_out, 1))
    return out.reshape(HYENA_ORDER, 2, B_WIDTH, L)


def _short_conv_params(hy_conv_w, hy_conv_b):
    return jnp.concatenate([hy_conv_w.T, hy_conv_b[:, None]], axis=1)


def _dense_dft_consts(S):
    n = 2 * S
    t = np.arange(S)[:, None]
    k = np.arange(n)[None, :]
    ang = -2.0 * np.pi * t * k / n
    cr, ci = np.cos(ang), np.sin(ang)
    wf = np.block([[cr, ci], [-ci, cr]])
    er, ei = cr.T, -ci.T
    wi = np.block([[er, ei], [-ei, er]])
    return jnp.asarray(wf, F32), jnp.asarray(wi, F32)


def _ctx_spec_kernel(f_ref, skip_ref, wf_ref, hr_ref, hi_ref, *, S):
    n = 2 * S
    fwd = f_ref[0, 0]
    bwd = f_ref[0, 1]
    lane = lax.broadcasted_iota(jnp.int32, bwd.shape, 1)
    bwd = jnp.where(lane == 0, 0.0, bwd)
    w = wf_ref[0:S, :]
    ff = jnp.dot(fwd, w, precision=HIGHEST, preferred_element_type=F32)
    fb = jnp.dot(bwd, w, precision=HIGHEST, preferred_element_type=F32)
    skip = skip_ref[0]
    hr_ref[0] = (ff[:, :n] + fb[:, :n] + skip) * (1.0 / n)
    hi_ref[0] = (ff[:, n:] - fb[:, n:]) * (1.0 / n)


def _ctx_spectrum(filt, skip, wf, S):
    n = 2 * S
    kernel = functools.partial(_ctx_spec_kernel, S=S)
    return pl.pallas_call(
        kernel,
        grid=(HYENA_ORDER,),
        in_specs=[
            pl.BlockSpec((1, 2, B_WIDTH, S), lambda o: (o, 0, 0, 0)),
            pl.BlockSpec((1, B_WIDTH, 1), lambda o: (o, 0, 0)),
            pl.BlockSpec(wf.shape, lambda o: (0, 0)),
        ],
        out_specs=[pl.BlockSpec((1, B_WIDTH, n), lambda o: (o, 0, 0))] * 2,
        out_shape=[jax.ShapeDtypeStruct((HYENA_ORDER, B_WIDTH, n), F32)] * 2,
        compiler_params=_cparams(("arbitrary",)),
        name="hyena_ctx_spectrum",
    )(filt, skip.reshape(HYENA_ORDER, B_WIDTH, 1), wf)


def _hyena_ctx_kernel(prm_ref, v_ref, x1_ref, x2_ref, bg_ref, hr_ref, hi_ref, wf_ref, wi_ref, o_ref, *, S):
    nb, cb, _ = v_ref.shape
    half = nb // 2
    n = 2 * S
    lane = lax.broadcasted_iota(jnp.int32, (nb, cb, S), 2)

    def short_conv(u, p):
        prev = jnp.where(lane == 0, 0.0, pltpu.roll(u, 1, 2))
        nxt = jnp.where(lane == S - 1, 0.0, pltpu.roll(u, S - 1, 2))
        return p[:, 3:4] + prev * p[:, 0:1] + u * p[:, 1:2] + nxt * p[:, 2:3]

    def stack(u):
        return jnp.concatenate([u[:half], u[half:]], axis=-1)

    def conv(xs, o):
        spec = jnp.dot(xs.reshape(half * cb, 2 * S).astype(BF16), wf_ref[...],
                       preferred_element_type=F32).reshape(half, cb, 2 * n)
        xr, xi = spec[..., :n], spec[..., n:]
        hr, hi = hr_ref[o], hi_ref[o]
        y = jnp.concatenate([xr * hr - xi * hi, xr * hi + xi * hr], axis=-1)
        return jnp.dot(y.reshape(half * cb, 2 * n).astype(BF16), wi_ref[...],
                       preferred_element_type=F32).reshape(half, cb, 2 * S)

    v = stack(short_conv(v_ref[...], prm_ref[0]))
    x1 = stack(short_conv(x1_ref[...], prm_ref[1]))
    x2 = stack(short_conv(x2_ref[...], prm_ref[2]))
    z = x1 * conv(v, 0)
    y = x2 * conv(z, 1)
    bg = stack(bg_ref[...])
    y = y * (bg * _sigmoid(bg))
    o_ref[0:half] = y[..., :S]
    o_ref[half:nb] = y[..., S:]


def _hyena_ctx(t_arr, prm, hr, hi, wf, wi, *, nb, S):
    cb = 16
    n = 2 * S
    ncb = B_WIDTH // cb
    kernel = functools.partial(_hyena_ctx_kernel, S=S)

    def part(k):
        return pl.BlockSpec((nb, cb, S), lambda c: (0, c + k * ncb, 0))

    return pl.pallas_call(
        kernel,
        grid=(ncb,),
        in_specs=[
            pl.BlockSpec((3, cb, 4), lambda c: (0, c, 0)),
            part(0), part(1), part(2), part(3),
            pl.BlockSpec((HYENA_ORDER, cb, n), lambda c: (0, c, 0)),
            pl.BlockSpec((HYENA_ORDER, cb, n), lambda c: (0, c, 0)),
            pl.BlockSpec(wf.shape, lambda c: (0, 0)),
            pl.BlockSpec(wi.shape, lambda c: (0, 0)),
        ],
        out_specs=pl.BlockSpec((nb, cb, S), lambda c: (0, c, 0)),
        out_shape=jax.ShapeDtypeStruct((nb, B_WIDTH, S), F32),
        compiler_params=_cparams(("arbitrary",)),
        name="hyena_ctx",
    )(prm.reshape(3, B_WIDTH, 4), t_arr, t_arr, t_arr, t_arr, hr, hi, wf.astype(BF16), wi.astype(BF16))


def _two_stage_consts(L, nseq, dtype):
    n2 = LANES
    n = 2 * L
    n1 = n // n2
    h1 = n1 // 2
    k1 = np.arange(n1)[:, None]
    a = -2.0 * np.pi * k1 * np.arange(h1)[None, :] / n1
    w1r, w1i = np.cos(a), np.sin(a)
    w1big = np.block([[w1r, -w1i], [w1i, w1r]])
    vr, vi = w1r.T, -w1i.T
    w1inv = np.block([[vr, -vi], [vi, vr]])
    a = -2.0 * np.pi * k1 * np.arange(n2)[None, :] / n
    twr, twi = np.cos(a), np.sin(a)
    a = -2.0 * np.pi * np.arange(n2)[:, None] * np.arange(n2)[None, :] / n2
    w2r, w2i = np.cos(a), np.sin(a)
    w2big = np.block([[w2r, w2i], [-w2i, w2r]])
    w2conj = np.block([[w2r, -w2i], [w2i, w2r]])
    f = lambda x: jnp.asarray(x, F32)
    m = lambda x: jnp.asarray(x, F32).astype(dtype)
    return dict(
        w1b=jnp.broadcast_to(m(w1big)[None], (nseq,) + w1big.shape),
        w1ib=jnp.broadcast_to(m(w1inv)[None], (nseq,) + w1inv.shape),
        w2=m(w2big), w2c=m(w2conj), twr=f(twr), twi=f(twi), n1=n1, h1=h1)


def _dft_dot(spec, a, b):
    if a.dtype == BF16 or b.dtype == BF16:
        return jnp.einsum(spec, a.astype(BF16), b.astype(BF16), preferred_element_type=F32)
    a_hi, b_hi = a.astype(BF16), b.astype(BF16)
    a_lo = (a - a_hi.astype(F32)).astype(BF16)
    b_lo = (b - b_hi.astype(F32)).astype(BF16)
    mm = lambda x, y: jnp.einsum(spec, x, y, preferred_element_type=F32)
    return mm(a_hi, b_hi) + (mm(a_hi, b_lo) + mm(a_lo, b_hi))


def _stage_fwd(xs, w1b, twr, twi, w2):
    s = xs.shape[0]
    n1 = twr.shape[0]
    a = _dft_dot('smk,skn->smn', w1b, xs)
    ar, ai = a[:, :n1], a[:, n1:]
    p = jnp.concatenate([ar * twr - ai * twi, ar * twi + ai * twr], axis=-1)
    return _dft_dot('mk,kn->mn', p.reshape(s * n1, 2 * LANES), w2)


def _stage_inv(cm, w1ib, twr, twi, w2c, s):
    n1 = twr.shape[0]
    dm = _dft_dot('mk,kn->mn', cm, w2c).reshape(s, n1, 2 * LANES)
    dr, di = dm[..., :LANES], dm[..., LANES:]
    r = jnp.concatenate([dr * twr + di * twi, di * twr - dr * twi], axis=1)
    return _dft_dot('smk,skn->smn', w1ib, r)


def _lat_spec_kernel(skip_ref, f_ref, w1b_ref, twr_ref, twi_ref, w2_ref, hr_ref, hi_ref, *, n_fft):
    cb, h1 = f_ref.shape[2], f_ref.shape[3]
    n1 = twr_ref.shape[0]
    fwd = f_ref[0, 0]
    bwd = f_ref[0, 1]
    first = (lax.broadcasted_iota(jnp.int32, bwd.shape, 1) == 0) & \
            (lax.broadcasted_iota(jnp.int32, bwd.shape, 2) == 0)
    bwd = jnp.where(first, 0.0, bwd)
    xs = jnp.concatenate([fwd, bwd], axis=0)
    xs = jnp.concatenate([xs, jnp.zeros_like(xs)], axis=1)
    sp = _stage_fwd(xs, w1b_ref[...], twr_ref[...], twi_ref[...], w2_ref[...]).reshape(2, cb, n1, 2 * LANES)
    inv = 1.0 / n_fft
    for c in range(cb):
        skip = skip_ref[pl.program_id(0), pl.program_id(1) * cb + c]
        hr_ref[0, c] = (sp[0, c, :, :LANES] + sp[1, c, :, :LANES] + skip) * inv
        hi_ref[0, c] = (sp[0, c, :, LANES:] - sp[1, c, :, LANES:]) * inv


def _lat_spectrum(filt, skip, L):
    cb = 16
    cs = _two_stage_consts(L, 2 * cb, F32)
    n1, h1 = cs["n1"], cs["h1"]
    filt5 = filt.reshape(HYENA_ORDER, 2, B_WIDTH, h1, LANES)
    kernel = functools.partial(_lat_spec_kernel, n_fft=2 * L)
    const = lambda a: pl.BlockSpec(a.shape, lambda o, c: (0,) * a.ndim)
    return pl.pallas_call(
        kernel,
        grid=(HYENA_ORDER, B_WIDTH // cb),
        in_specs=[
            pl.BlockSpec(memory_space=pltpu.SMEM),
            pl.BlockSpec((1, 2, cb, h1, LANES), lambda o, c: (o, 0, c, 0, 0)),
            const(cs["w1b"]), const(cs["twr"]), const(cs["twi"]), const(cs["w2"]),
        ],
        out_specs=[pl.BlockSpec((1, cb, n1, LANES), lambda o, c: (o, c, 0, 0))] * 2,
        out_shape=[jax.ShapeDtypeStruct((HYENA_ORDER, B_WIDTH, n1, LANES), F32)] * 2,
        compiler_params=_cparams(("arbitrary", "arbitrary")),
        name="hyena_lat_spectrum",
    )(skip, filt5, cs["w1b"], cs["twr"], cs["twi"], cs["w2"])


def _hyena_lat_kernel(prm_ref, v_ref, x1_ref, x2_ref, bg_ref, hr_ref, hi_ref,
                      w1b_ref, w1ib_ref, twr_ref, twi_ref, w2_ref, w2c_ref, o_ref,
                      vs_ref, x1s_ref, x2s_ref):
    nb, cb, h1, _ = v_ref.shape
    half = nb // 2
    s = cb * half
    n1 = twr_ref.shape[0]
    c0 = pl.program_id(0) * cb
    row = lax.broadcasted_iota(jnp.int32, (nb, h1, LANES), 1)
    lane = lax.broadcasted_iota(jnp.int32, (nb, h1, LANES), 2)

    def short_conv(u, part, c):
        r = pltpu.roll(u, 1, 2)
        prev = jnp.where(lane == 0, pltpu.roll(r, 1, 1), r)
        prev = jnp.where((lane == 0) & (row == 0), 0.0, prev)
        r = pltpu.roll(u, LANES - 1, 2)
        nxt = jnp.where(lane == LANES - 1, pltpu.roll(r, h1 - 1, 1), r)
        nxt = jnp.where((lane == LANES - 1) & (row == h1 - 1), 0.0, nxt)
        ch = part * B_WIDTH + c0 + c
        return prm_ref[3, ch] + prev * prm_ref[0, ch] + u * prm_ref[1, ch] + nxt * prm_ref[2, ch]

    def stack_into(dst_ref, u, c):
        dst_ref[c * half:(c + 1) * half, 0:h1, :] = u[:half]
        dst_ref[c * half:(c + 1) * half, h1:2 * h1, :] = u[half:]

    for c in range(cb):
        stack_into(vs_ref, short_conv(v_ref[:, c], 0, c), c)
        stack_into(x1s_ref, short_conv(x1_ref[:, c], 1, c), c)
        bg = bg_ref[:, c]
        stack_into(x2s_ref, short_conv(x2_ref[:, c], 2, c) * (bg * _sigmoid(bg)), c)

    twr, twi = twr_ref[...], twi_ref[...]

    def conv(xs, o):
        sp = _stage_fwd(xs, w1b_ref[...], twr, twi, w2_ref[...]).reshape(cb, half, n1, 2 * LANES)
        br, bi = sp[..., :LANES], sp[..., LANES:]
        hr = hr_ref[o][:, None]
        hi = hi_ref[o][:, None]
        cm = jnp.concatenate([br * hr - bi * hi, br * hi + bi * hr], axis=-1).reshape(s * n1, 2 * LANES)
        return _stage_inv(cm, w1ib_ref[...], twr, twi, w2c_ref[...], s)

    z = x1s_ref[...] * conv(vs_ref[...], 0)
    y = x2s_ref[...] * conv(z, 1)
    for c in range(cb):
        o_ref[0:half, c] = y[c * half:(c + 1) * half, 0:h1]
        o_ref[half:nb, c] = y[c * half:(c + 1) * half, h1:2 * h1]


def _hyena_lat(t_arr, prm, hr, hi, *, nb, L):
    cb = 8
    half = nb // 2
    s = cb * half
    cs = _two_stage_consts(L, s, BF16)
    n1, h1 = cs["n1"], cs["h1"]
    ncb = B_WIDTH // cb
    t5 = t_arr.reshape(nb, T_W, h1, LANES)

    def part(k):
        return pl.BlockSpec((nb, cb, h1, LANES), lambda c: (0, c + k * ncb, 0, 0))

    const = lambda a: pl.BlockSpec(a.shape, lambda c: (0,) * a.ndim)
    return pl.pallas_call(
        _hyena_lat_kernel,
        grid=(ncb,),
        in_specs=[
            pl.BlockSpec(memory_space=pltpu.SMEM),
            part(0), part(1), part(2), part(3),
            pl.BlockSpec((HYENA_ORDER, cb, n1, LANES), lambda c: (0, c, 0, 0)),
            pl.BlockSpec((HYENA_ORDER, cb, n1, LANES), lambda c: (0, c, 0, 0)),
            const(cs["w1b"]), const(cs["w1ib"]), const(cs["twr"]), const(cs["twi"]),
            const(cs["w2"]), const(cs["w2c"]),
        ],
        out_specs=pl.BlockSpec((nb, cb, h1, LANES), lambda c: (0, c, 0, 0)),
        out_shape=jax.ShapeDtypeStruct((nb, B_WIDTH, h1, LANES), F32),
        scratch_shapes=[pltpu.VMEM((s, 2 * h1, LANES), F32)] * 3,
        compiler_params=_cparams(("arbitrary",)),
        name="hyena_lat",
    )(prm.T, t5, t5, t5, t5, hr, hi,
      cs["w1b"], cs["w1ib"], cs["twr"], cs["twi"], cs["w2"], cs["w2c"])


def _merge_kernel(x_ref, ya_ref, ybt_ref, yc_ref, mg_ref, gate_ref, wa_ref, wb_ref, wc_ref, wo_ref,
                  fw_ref, o_ref, *, final):
    mg = mg_ref[...]
    tm = mg.shape[0]
    if len(ybt_ref.shape) == 4:
        n_rows = tm // LANES
        row0 = (pl.program_id(0) % (ybt_ref.shape[2] // n_rows)) * n_rows
        yb = jnp.concatenate([ybt_ref[0, :, row0 + r, :].T for r in range(n_rows)], axis=0).astype(BF16)
    else:
        yb = ybt_ref[0].T.astype(BF16)
    m = (_sigmoid(mg[:, :D_MODEL].astype(F32)) * jnp.dot(ya_ref[...], wa_ref[...], preferred_element_type=F32)
         + _sigmoid(mg[:, D_MODEL:2 * D_MODEL].astype(F32)) * jnp.dot(yb, wb_ref[...], preferred_element_type=F32)
         + _sigmoid(mg[:, 2 * D_MODEL:].astype(F32)) * jnp.dot(yc_ref[...], wc_ref[...], preferred_element_type=F32))
    out = jnp.dot(m.astype(BF16), wo_ref[...], preferred_element_type=F32)
    xn = x_ref[...] + gate_ref[0] * out
    if final:
        var = jnp.mean(xn * xn, axis=-1, keepdims=True)
        xn = xn * lax.rsqrt(var + EPS) * fw_ref[...]
    o_ref[...] = xn


def _merge(x2d, ya, ybt, yc, u, gate, wa, wb, wc, wo, fw, *, nb, lb, tm, final):
    rows = nb * lb
    bpb = lb // tm
    per_mod = gate.shape[0] > 1
    kernel = functools.partial(_merge_kernel, final=final)
    const = lambda a: pl.BlockSpec(a.shape, lambda i: (0,) * a.ndim, pipeline_mode=pl.Buffered(1))
    if ybt.ndim == 4:
        sub = 8 * LANES // tm
        assert tm % LANES == 0 and (8 * LANES) % tm == 0 and bpb % sub == 0
        yb_spec = pl.BlockSpec((1, B_WIDTH, 8, LANES), lambda i: (i // bpb, 0, (i % bpb) // sub, 0))
    else:
        yb_spec = pl.BlockSpec((1, B_WIDTH, tm), lambda i: (i // bpb, 0, i % bpb))
    return pl.pallas_call(
        kernel,
        grid=(rows // tm,),
        in_specs=[
            pl.BlockSpec((tm, D_MODEL), lambda i: (i, 0)),
            pl.BlockSpec((tm, A_WIDTH), lambda i: (i, 0)),
            yb_spec,
            pl.BlockSpec((tm, C_WIDTH), lambda i: (i, 0)),
            pl.BlockSpec((tm, MG_W), lambda i: (i, U_MG // MG_W)),
            pl.BlockSpec((1, 1, D_MODEL), lambda i: ((i // bpb) if per_mod else 0, 0, 0)),
            const(wa), const(wb), const(wc), const(wo),
            pl.BlockSpec((1, D_MODEL), lambda i: (0, 0)),
        ],
        out_specs=pl.BlockSpec((tm, D_MODEL), lambda i: (i, 0)),
        out_shape=jax.ShapeDtypeStruct((rows, D_MODEL), F32),
        compiler_params=_cparams(("arbitrary",)),
        name="merge_final" if final else "merge",
    )(x2d, ya.reshape(rows, A_WIDTH), ybt, yc.reshape(rows, C_WIDTH), u, gate, wa, wb, wc, wo,
      fw.reshape(1, D_MODEL))


def _rope_tables(L):
    t = np.arange(L)
    row = (t // GRID_W).astype(np.float32)
    col = (t % GRID_W).astype(np.float32)
    nf = HEAD_DIM // 4
    inv = np.power(np.float32(ROPE_BASE), -np.arange(nf, dtype=np.float32) / nf).astype(np.float32)
    ang = np.concatenate([row[:, None] * inv[None], col[:, None] * inv[None]], axis=-1)
    cos, sin = np.cos(ang), np.sin(ang)
    reps = LANES // HEAD_DIM
    cos_t = np.tile(np.concatenate([cos, cos], axis=-1), (1, reps))
    sin_t = np.tile(np.concatenate([-sin, sin], axis=-1), (1, reps))
    return jnp.asarray(cos_t, F32), jnp.asarray(sin_t, F32)


def _perm_heads(w, base, axis):
    return [lax.slice_in_dim(w, base + h * HEAD_DIM, base + (h + 1) * HEAD_DIM, axis=axis) for h in A_PERM]


def _prep_w_in_kernel(w_ref, nat_ref, t_ref):
    def put(dst, src, width):
        nat_ref[0, :, dst:dst + width] = w_ref[0, :, src:src + width].astype(BF16)

    put(U_MG, IN_MG, MG_W)
    put(U_CQ, IN_CQ, 4 * C_WIDTH)
    for p, h in enumerate(A_PERM):
        put(U_AQ + p * HEAD_DIM, IN_AQ + h * HEAD_DIM, HEAD_DIM)
        put(U_AG + p * HEAD_DIM, IN_AG + h * HEAD_DIM, HEAD_DIM)
    put(U_AK, IN_AK, A_KV_WIDTH)
    put(U_AV, IN_AV, A_KV_WIDTH)
    t_ref[0] = w_ref[0, :, IN_BU:IN_BU + T_W].T.astype(BF16)


def _prep_w_in(w_in):
    depth, _, in_w = w_in.shape
    rb = 128
    return pl.pallas_call(
        _prep_w_in_kernel,
        grid=(depth, D_MODEL // rb),
        in_specs=[pl.BlockSpec((1, rb, in_w), lambda l, i: (l, i, 0))],
        out_specs=[pl.BlockSpec((1, rb, NAT_W), lambda l, i: (l, i, 0)),
                   pl.BlockSpec((1, T_W, rb), lambda l, i: (l, 0, i))],
        out_shape=[jax.ShapeDtypeStruct((depth, D_MODEL, NAT_W), BF16),
                   jax.ShapeDtypeStruct((depth, T_W, D_MODEL), BF16)],
        compiler_params=_cparams(("arbitrary", "arbitrary")),
        name="prep_w_in",
    )(w_in)


def kernel(x_prompt, x_sample, c, cache_a_k, cache_a_v, cache_c_k, cache_c_v, c_ctx, norm_w, w_ada, b_ada, w_in, a_sink, hy_conv_w, hy_conv_b, hy_w1, hy_b1, hy_w2, hy_b2, hy_freq, hy_w3, hy_decay, hy_skip, c_rpb, w_up_a, w_up_b, w_up_c, w_out, final_norm_w):
    nbc, S, _ = x_prompt.shape
    nbl, L, _ = x_sample.shape
    depth = w_in.shape[0]
    n_ctx = cache_a_k.shape[2]
    assert nbc % 2 == 0 and nbl % 2 == 0 and L % (GRID_W * LANES // 2) == 0

    pad = (-(nbl + 1)) % 8
    cond = jnp.concatenate([c, c_ctx[None], jnp.zeros((pad, D_MODEL), F32)], axis=0)
    mod = _adaln(cond, w_ada, b_ada)

    cos_t, sin_t = _rope_tables(L)
    tm_ctx = 512 if (nbc * S) % 512 == 0 else S
    zeros_t = jnp.zeros((tm_ctx, LANES), F32)
    wf_c, wi_c = _dense_dft_consts(S)
    ca_k = cache_a_k.reshape(nbl, depth, n_ctx, A_KV_WIDTH)
    ca_v = cache_a_v.reshape(nbl, depth, n_ctx, A_KV_WIDTH)
    cc_k = cache_c_k.reshape(nbl, depth, n_ctx, C_WIDTH)
    cc_v = cache_c_v.reshape(nbl, depth, n_ctx, C_WIDTH)

    w_nat, w_t = _prep_w_in(w_in)

    xp = x_prompt.reshape(nbc * S, D_MODEL)
    xs = x_sample.reshape(nbl * L, D_MODEL)
    aks, avs, cks, cvs = [], [], [], []
    for l in range(depth):
        final = l == depth - 1
        wa = jnp.concatenate(_perm_heads(w_up_a[l], 0, 0), axis=0).astype(BF16)
        wb, wc, wo = (w.astype(BF16) for w in (w_up_b[l], w_up_c[l], w_out[l]))
        prm = _short_conv_params(hy_conv_w[l], hy_conv_b[l])
        filt_args = (hy_w1[l], hy_b1[l], hy_w2[l], hy_b2[l], hy_freq[l], hy_w3[l], hy_decay[l])
        t2 = _nbr_bias_table(c_rpb[l])

        def mod_parts(rows):
            m = mod[l, rows][:, None, :]
            return 1.0 + m[..., D_MODEL:2 * D_MODEL], m[..., :D_MODEL], m[..., 2 * D_MODEL:]

        sc, sh, gate = mod_parts(slice(nbl, nbl + 1))
        u, t_arr = _inproj(xp, sc, sh, norm_w[l], w_nat, w_t, zeros_t, zeros_t,
                           layer=l, nb=nbc, lb=S, tm=tm_ctx, rope=False, u_dtype=F32)
        aks.append(u[:, U_AK:U_AV].reshape(nbc, S, A_KV_HEADS, HEAD_DIM))
        avs.append(u[:, U_AV:NAT_W].reshape(nbc, S, A_KV_HEADS, HEAD_DIM))
        cks.append(u[:, U_CK:U_CV].reshape(nbc, S, C_HEADS, HEAD_DIM))
        cvs.append(u[:, U_CV:U_CG].reshape(nbc, S, C_HEADS, HEAD_DIM))
        ya, yc = _ctx_attn(a_sink[l], u, nb=nbc, lb=S)
        hr, hi = _ctx_spectrum(_hyena_filters(S, *filt_args), hy_skip[l], wf_c, S)
        ybt = _hyena_ctx(t_arr, prm, hr, hi, wf_c, wi_c, nb=nbc, S=S)
        xp = _merge(xp, ya, ybt, yc, u, gate, wa, wb, wc, wo, final_norm_w,
                    nb=nbc, lb=S, tm=S, final=final)

        sc, sh, gate = mod_parts(slice(0, nbl))
        u, t_arr = _inproj(xs, sc, sh, norm_w[l], w_nat, w_t, cos_t, sin_t,
                           layer=l, nb=nbl, lb=L, tm=1024, rope=True, u_dtype=BF16)
        ya = _lat_win_attn(a_sink[l], u, ca_k, ca_v, l, nb=nbl, lb=L)
        yc = _lat_nbr_attn(u, cc_k, cc_v, l, t2, nb=nbl, lb=L)
        hr, hi = _lat_spectrum(_hyena_filters(L, *filt_args), hy_skip[l], L)
        ybt = _hyena_lat(t_arr, prm, hr, hi, nb=nbl, L=L)
        xs = _merge(xs, ya, ybt, yc, u, gate, wa, wb, wc, wo, final_norm_w,
                    nb=nbl, lb=L, tm=256, final=final)

    y_prompt = xp.reshape(nbc, S, D_MODEL)
    y_sample = xs.reshape(nbl, L, D_MODEL)
    return (y_prompt, y_sample, jnp.stack(aks, axis=1), jnp.stack(avs, axis=1),
            jnp.stack(cks, axis=1), jnp.stack(cvs, axis=1))
```

```python
import functools

import numpy as np
import jax
import jax.numpy as jnp
from jax import lax
from jax.experimental import pallas as pl
from jax.experimental.pallas import tpu as pltpu

F32 = jnp.float32
BF16 = jnp.bfloat16
HIGHEST = lax.Precision.HIGHEST

D_MODEL = 2048
HEAD_DIM = 64
A_HEADS = 12
A_KV_HEADS = 4
A_GROUP = A_HEADS // A_KV_HEADS
A_WIDTH = A_HEADS * HEAD_DIM
A_KV_WIDTH = A_KV_HEADS * HEAD_DIM
A_WINDOW = 128
B_WIDTH = 512
HYENA_ORDER = 2
HYENA_BANDS = 16
C_HEADS = 12
C_WIDTH = C_HEADS * HEAD_DIM
GRID_W = 64
NA_ROWS = 8
NA_COLS = 16
ROPE_BASE = 10000.0
EPS = 1e-6
NEG_INF = -1e30
ATTN_SCALE = HEAD_DIM ** -0.5
LOG2E = 1.4426950408889634
QK_SCALE = ATTN_SCALE * LOG2E

LANES = 128
VMEM_LIMIT = 56 * 1024 * 1024

IN_AQ, IN_AK, IN_AV, IN_AG = 0, 768, 1024, 1280
IN_BU, IN_CQ, IN_MG = 2048, 4096, 7168
PROJ_TN = 1024
MG_W = 3 * D_MODEL
U_MG = 0
U_CQ = U_MG + MG_W
U_CG = U_CQ + C_WIDTH
U_AQ = U_CG + C_WIDTH
U_AG = U_AQ + A_WIDTH
U_CK = U_AG + A_WIDTH
U_CV = U_CK + C_WIDTH
U_AK = U_CV + C_WIDTH
U_AV = U_AK + A_KV_WIDTH
NAT_W = U_AV + A_KV_WIDTH
U_KV_W = NAT_W - U_CK
T_W = 4 * B_WIDTH

A_PERM = tuple(2 * A_GROUP * m + A_GROUP * half + j
               for m in range(A_KV_HEADS // 2) for j in range(A_GROUP) for half in range(2))


def _cparams(sem):
    return pltpu.CompilerParams(dimension_semantics=sem, vmem_limit_bytes=VMEM_LIMIT)


def _sigmoid(x):
    return 1.0 / (1.0 + jnp.exp(-x))


def _lane_chunks(off, width):
    assert off % LANES == 0 and width % LANES == 0
    per_blk = PROJ_TN // LANES
    return [divmod(g, per_blk) for g in range(off // LANES, (off + width) // LANES)]


def _adaln_kernel(cond_ref, w_ref, b_ref, o_ref):
    c = cond_ref[...]
    s = (c * _sigmoid(c)).astype(BF16)
    acc = jnp.dot(s, w_ref[0].astype(BF16), preferred_element_type=F32)
    o_ref[0] = acc + b_ref[0]


def _adaln(cond, w_ada, b_ada):
    depth = w_ada.shape[0]
    rows = cond.shape[0]
    tn = 1024
    return pl.pallas_call(
        _adaln_kernel,
        grid=(depth, 3 * D_MODEL // tn),
        in_specs=[
            pl.BlockSpec((rows, D_MODEL), lambda l, j: (0, 0)),
            pl.BlockSpec((1, D_MODEL, tn), lambda l, j: (l, 0, j)),
            pl.BlockSpec((1, 1, tn), lambda l, j: (l, 0, j)),
        ],
        out_specs=pl.BlockSpec((1, rows, tn), lambda l, j: (l, 0, j)),
        out_shape=jax.ShapeDtypeStruct((depth, rows, 3 * D_MODEL), F32),
        compiler_params=_cparams(("arbitrary", "arbitrary")),
        name="adaln",
    )(cond, w_ada, b_ada.reshape(depth, 1, 3 * D_MODEL))


def _modulated_norm(x, nw, sc, sh):
    var = jnp.mean(x * x, axis=-1, keepdims=True)
    return ((x * lax.rsqrt(var + EPS) * nw) * sc + sh).astype(BF16)


def _inproj_kernel(*refs, rope, tm, lb, fused_norm, kv_f32):
    j = pl.program_id(1)
    tn = PROJ_TN
    n_nat = NAT_W // tn
    kv_blk0 = U_CK // tn
    if fused_norm:
        x_ref, sc_ref, sh_ref, nw_ref, *refs, h_ref = refs

        @pl.when(j == 0)
        def _():
            h_ref[...] = _modulated_norm(x_ref[...], nw_ref[...], sc_ref[0], sh_ref[0])
    else:
        h_ref, *refs = refs
    w_ref, wt_ref, cos_ref, sin_ref, u_ref, t_ref, *maybe_kv_ref = refs
    assert len(maybe_kv_ref) == int(kv_f32)

    def rotate(a):
        lane = lax.broadcasted_iota(jnp.int32, (tm, LANES), 1)
        first = (lane % HEAD_DIM) < (HEAD_DIM // 2)
        partner = jnp.where(first, pltpu.roll(a, LANES - HEAD_DIM // 2, 1), pltpu.roll(a, HEAD_DIM // 2, 1))
        return a * cos_ref[...] + partner * sin_ref[...]

    def nat_step(chunk_ops, keep_f32):
        acc = jnp.dot(h_ref[...], w_ref[...], preferred_element_type=F32)
        if keep_f32:
            maybe_kv_ref[0][...] = acc
        if not chunk_ops:
            u_ref[...] = acc.astype(u_ref.dtype)
            return
        for ci in range(tn // LANES):
            a = acc[:, ci * LANES:(ci + 1) * LANES]
            for op in chunk_ops.get(ci, ()):
                a = rotate(a) if op == "rope" else a * QK_SCALE
            u_ref[:, ci * LANES:(ci + 1) * LANES] = a.astype(u_ref.dtype)

    special = {}
    for off, width, ops in ((U_CQ, C_WIDTH, ("scale",)),
                            (U_AQ, A_WIDTH, ("rope", "scale") if rope else ("scale",)),
                            (U_AK, A_KV_WIDTH, ("rope",) if rope else ())):
        for blk, ci in (_lane_chunks(off, width) if ops else ()):
            special.setdefault(blk, {})[ci] = ops
    if maybe_kv_ref:
        assert not rope
        for blk in range(kv_blk0, n_nat):
            special.setdefault(blk, {})

    plain = j < n_nat
    for blk, chunk_ops in special.items():
        plain = plain & (j != blk)
        pl.when(j == blk)(functools.partial(nat_step, chunk_ops, bool(maybe_kv_ref) and blk >= kv_blk0))
    pl.when(plain)(functools.partial(nat_step, None, False))

    @pl.when(j >= n_nat)
    def _():
        acc_t = lax.dot_general(wt_ref[...], h_ref[...], (((1,), (1,)), ((), ())),
                                preferred_element_type=F32)
        if tm <= lb:
            t_ref[0] = acc_t
        else:
            for k in range(tm // lb):
                t_ref[k] = acc_t[:, k * lb:(k + 1) * lb]


def _inproj(h2d, norm, w_nat, w_t, cos_t, sin_t, *, layer, nb, lb, tm, rope, kv_f32):
    rows = nb * lb
    tn = PROJ_TN
    n_nat, n_t = NAT_W // tn, T_W // tn
    kv_blk0, n_kv = U_CK // tn, U_KV_W // tn
    assert U_CK % tn == 0 and kv_blk0 + n_kv == n_nat
    bpb = max(lb // tm, 1)

    if tm <= lb:
        t_spec = pl.BlockSpec((1, tn, tm), lambda i, j: (i // bpb, jnp.clip(j - n_nat, 0, n_t - 1), i % bpb))
    else:
        t_spec = pl.BlockSpec((tm // lb, tn, lb), lambda i, j: (i, jnp.clip(j - n_nat, 0, n_t - 1), 0))

    out_specs = [pl.BlockSpec((tm, tn), lambda i, j: (i, jnp.minimum(j, n_nat - 1))), t_spec]
    out_shape = [jax.ShapeDtypeStruct((rows, NAT_W), BF16), jax.ShapeDtypeStruct((nb, T_W, lb), F32)]
    if kv_f32:
        out_specs.append(pl.BlockSpec((tm, tn), lambda i, j: (i, jnp.clip(j - kv_blk0, 0, n_kv - 1))))
        out_shape.append(jax.ShapeDtypeStruct((rows, U_KV_W), F32))

    if norm is None:
        lead_specs = [pl.BlockSpec((tm, D_MODEL), lambda i, j: (i, 0))]
        lead_args, scratch = (h2d,), []
    else:
        sc, sh, norm_w = norm
        per_mod = sc.shape[0] > 1
        mod_spec = pl.BlockSpec((1, 1, D_MODEL), lambda i, j: ((i // bpb) if per_mod else 0, 0, 0))
        lead_specs = [pl.BlockSpec((tm, D_MODEL), lambda i, j: (i, 0), pipeline_mode=pl.Buffered(1)),
                      mod_spec, mod_spec, pl.BlockSpec((1, D_MODEL), lambda i, j: (0, 0))]
        lead_args, scratch = (h2d, sc, sh, norm_w.reshape(1, D_MODEL)), [pltpu.VMEM((tm, D_MODEL), BF16)]

    kernel = functools.partial(_inproj_kernel, rope=rope, tm=tm, lb=lb, fused_norm=norm is not None,
                               kv_f32=kv_f32)
    return pl.pallas_call(
        kernel,
        grid=(rows // tm, n_nat + n_t),
        in_specs=lead_specs + [
            pl.BlockSpec((None, D_MODEL, tn), lambda i, j: (layer, 0, jnp.minimum(j, n_nat - 1))),
            pl.BlockSpec((None, tn, D_MODEL), lambda i, j: (layer, jnp.maximum(j - n_nat, 0), 0)),
            pl.BlockSpec((tm, LANES), lambda i, j: (i % bpb, 0)),
            pl.BlockSpec((tm, LANES), lambda i, j: (i % bpb, 0)),
        ],
        out_specs=out_specs,
        out_shape=out_shape,
        scratch_shapes=scratch,
        compiler_params=_cparams(("arbitrary", "arbitrary")),
        name="inproj_rope" if rope else "inproj",
    )(*lead_args, w_nat, w_t, cos_t, sin_t)


def _softmax_pv(s, v, sink_col):
    m = jnp.max(s, axis=-1, keepdims=True)
    if sink_col is not None:
        m = jnp.maximum(m, sink_col)
    e = jnp.exp2(s - m)
    den = jnp.sum(e, axis=-1, keepdims=True)
    if sink_col is not None:
        den = den + jnp.exp2(sink_col - m)
    o = jnp.dot(e.astype(BF16), v, preferred_element_type=F32)
    return o * (1.0 / den)


def _sink_column(sink_ref, heads, rows_per_head):
    n = len(heads)
    row_head = lax.broadcasted_iota(jnp.int32, (n * rows_per_head, 1), 0) // rows_per_head
    col = jnp.full((n * rows_per_head, 1), sink_ref[heads[-1]], F32)
    for idx in range(n - 1):
        col = jnp.where(row_head == idx, sink_ref[heads[idx]], col)
    return col * LOG2E


def _ctx_attn_kernel(sink_ref, qg_ref, kv0_ref, kv1_ref, ya_ref, yc_ref):
    qg = qg_ref[0]
    kv = jnp.concatenate([kv0_ref[0], kv1_ref[0]], axis=1)
    s_len = qg.shape[0]
    nt = (((1,), (1,)), ((), ()))
    o_cg, o_aq, o_ag = U_CG - U_CQ, U_AQ - U_CQ, U_AG - U_CQ
    o_cv, o_ak, o_av = U_CV - U_CK, U_AK - U_CK, U_AV - U_CK
    head = lambda x, off, h: x[:, off + h * HEAD_DIM:off + (h + 1) * HEAD_DIM]
    a_heads = lambda g: [A_GROUP * g + hh for hh in range(A_GROUP)]

    def scores(job):
        if job < A_KV_HEADS:
            q = jnp.concatenate([head(qg, o_aq, A_PERM.index(h)) for h in a_heads(job)], axis=0)
            k = head(kv, o_ak, job)
        else:
            q = head(qg, 0, job - A_KV_HEADS)
            k = head(kv, 0, job - A_KV_HEADS)
        return lax.dot_general(q, k, nt, preferred_element_type=F32)

    def finish(job, s):
        if job < A_KV_HEADS:
            return _softmax_pv(s, head(kv, o_av, job), _sink_column(sink_ref, a_heads(job), s_len))
        return _softmax_pv(s, head(kv, o_cv, job - A_KV_HEADS), None)

    res = _software_pipeline(A_KV_HEADS + C_HEADS, scores, finish)
    pieces = [None] * A_HEADS
    for g in range(A_KV_HEADS):
        for hh, h in enumerate(a_heads(g)):
            pieces[A_PERM.index(h)] = res[g][hh * s_len:(hh + 1) * s_len]
    ya = jnp.concatenate(pieces, axis=1)
    ag = qg[:, o_ag:o_ag + A_WIDTH].astype(F32)
    ya_ref[0] = (ya * (ag * _sigmoid(ag))).astype(BF16)
    yc = jnp.concatenate(res[A_KV_HEADS:], axis=1)
    cg = qg[:, o_cg:o_cg + C_WIDTH].astype(F32)
    yc_ref[0] = (yc * (cg * _sigmoid(cg))).astype(BF16)


def _ctx_attn(sink, u, *, nb, lb):
    u3 = u.reshape(nb, lb, NAT_W)
    qg_w = U_CK - U_CQ
    assert U_CQ % qg_w == 0 and U_CK % PROJ_TN == 0 and U_KV_W == 2 * PROJ_TN
    return pl.pallas_call(
        _ctx_attn_kernel,
        grid=(nb,),
        in_specs=[
            pl.BlockSpec(memory_space=pltpu.SMEM),
            pl.BlockSpec((1, lb, qg_w), lambda b: (b, 0, U_CQ // qg_w)),
            pl.BlockSpec((1, lb, PROJ_TN), lambda b: (b, 0, U_CK // PROJ_TN)),
            pl.BlockSpec((1, lb, PROJ_TN), lambda b: (b, 0, U_CK // PROJ_TN + 1)),
        ],
        out_specs=[
            pl.BlockSpec((1, lb, A_WIDTH), lambda b: (b, 0, 0)),
            pl.BlockSpec((1, lb, C_WIDTH), lambda b: (b, 0, 0)),
        ],
        out_shape=[
            jax.ShapeDtypeStruct((nb, lb, A_WIDTH), BF16),
            jax.ShapeDtypeStruct((nb, lb, C_WIDTH), BF16),
        ],
        compiler_params=_cparams(("arbitrary",)),
        name="ctx_attn",
    )(sink, u3, u3, u3)


def _with_ones(v_pair):
    return jnp.concatenate([v_pair, jnp.ones_like(v_pair)], axis=1)


def _transpose_keys(k_ref, kt_scr, ck_ref, ckt_scr, cv_ref, cv_scr, chunk):
    n_tok = k_ref.shape[1]

    def body(c, carry):
        start = pl.multiple_of(c * chunk, chunk)
        blk = k_ref[0, pl.ds(start, chunk), :].astype(F32)
        kt_scr[:, pl.ds(start, chunk)] = blk.T.astype(BF16)
        return carry

    lax.fori_loop(0, n_tok // chunk, body, 0)
    ckt_scr[...] = ck_ref[0, 0].T.astype(BF16)
    ctx_v = cv_ref[0, 0].astype(BF16)
    for m in range(ctx_v.shape[1] // LANES):
        cv_scr[m] = _with_ones(ctx_v[:, m * LANES:(m + 1) * LANES])


PIPELINE_DEPTH = 2


def _software_pipeline(n, first_stage, second_stage):
    pending = [first_stage(h) for h in range(min(PIPELINE_DEPTH, n))]
    res = []
    for h in range(n):
        if h + PIPELINE_DEPTH < n:
            pending.append(first_stage(h + PIPELINE_DEPTH))
        res.append(second_stage(h, pending.pop(0)))
    return res


def _pair_scores(q_half, kt_loc, bias_fn, kt_ctx):
    s_loc = bias_fn(jnp.dot(q_half, kt_loc, preferred_element_type=F32))
    s_ctx = jnp.dot(q_half, kt_ctx, preferred_element_type=F32)
    return s_loc, s_ctx


def _pair_finish(scores, v_loc, v_ctx, sink_col):
    s_loc, s_ctx = scores
    m = jnp.maximum(jnp.max(s_loc, axis=-1, keepdims=True), jnp.max(s_ctx, axis=-1, keepdims=True))
    if sink_col is not None:
        m = jnp.maximum(m, sink_col)
    e_loc = jnp.exp2(s_loc - m).astype(BF16)
    e_ctx = jnp.exp2(s_ctx - m).astype(BF16)
    o = (jnp.dot(e_loc, v_loc, preferred_element_type=F32)
         + jnp.dot(e_ctx, v_ctx, preferred_element_type=F32))
    den = o[:, LANES:]
    if sink_col is not None:
        den = den + jnp.exp2(sink_col - m)
    return o[:, :LANES] * (1.0 / den)


def _lat_win_kernel(sink_ref, q_ref, k_ref, v_ref, ag_ref, ck_ref, cv_ref, o_ref,
                    kt_scr, ckt_scr, cv_scr):
    i = pl.program_id(1)
    nblk = pl.num_programs(1)
    blk = A_WINDOW
    span = 3 * blk

    @pl.when(i == 0)
    def _():
        _transpose_keys(k_ref, kt_scr, ck_ref, ckt_scr, cv_ref, cv_scr, blk)

    start = pl.multiple_of(jnp.clip(i - 1, 0, nblk - 3) * blk, blk)
    q_pos = i * blk + lax.broadcasted_iota(jnp.int32, (blk, span), 0)
    k_pos = start + lax.broadcasted_iota(jnp.int32, (blk, span), 1)
    valid = jnp.abs(q_pos - k_pos) <= A_WINDOW
    mask = lambda s: jnp.where(valid, s, NEG_INF)

    q = q_ref[0]
    lane = lax.broadcasted_iota(jnp.int32, (blk, LANES), 1)
    low = lane < HEAD_DIM

    def scores(job):
        pair, half = divmod(job, 2)
        rows_m = slice(pair // A_GROUP * LANES, (pair // A_GROUP + 1) * LANES)
        qp = q[:, pair * LANES:(pair + 1) * LANES]
        q_half = jnp.where(low if half == 0 else ~low, qp, jnp.zeros_like(qp))
        return _pair_scores(q_half, kt_scr[rows_m, pl.ds(start, span)], mask, ckt_scr[rows_m, :])

    def finish(job, sc):
        pair, half = divmod(job, 2)
        rows_m = slice(pair // A_GROUP * LANES, (pair // A_GROUP + 1) * LANES)
        v_loc = v_ref[0, pl.ds(start, span), rows_m]
        return _pair_finish(sc, _with_ones(v_loc), cv_scr[pair // A_GROUP],
                            sink_ref[A_PERM[job]] * LOG2E)

    res = _software_pipeline(A_HEADS, scores, finish)
    ya = jnp.concatenate([jnp.where(low, res[2 * p], res[2 * p + 1]) for p in range(A_HEADS // 2)], axis=1)
    ag = ag_ref[0].astype(F32)
    o_ref[0] = (ya * (ag * _sigmoid(ag))).astype(BF16)


def _lat_win_attn(sink, u, cache_k, cache_v, layer, *, nb, lb):
    blk = A_WINDOW
    nblk = lb // blk
    assert nblk >= 3
    n_ctx = cache_k.shape[2]
    u3 = u.reshape(nb, lb, NAT_W)
    ctx_spec = pl.BlockSpec((1, 1, n_ctx, A_KV_WIDTH), lambda b, i: (b, layer, 0, 0))
    return pl.pallas_call(
        _lat_win_kernel,
        grid=(nb, nblk),
        in_specs=[
            pl.BlockSpec(memory_space=pltpu.SMEM),
            pl.BlockSpec((1, blk, A_WIDTH), lambda b, i: (b, i, U_AQ // A_WIDTH)),
            pl.BlockSpec((1, lb, A_KV_WIDTH), lambda b, i: (b, 0, U_AK // A_KV_WIDTH)),
            pl.BlockSpec((1, lb, A_KV_WIDTH), lambda b, i: (b, 0, U_AV // A_KV_WIDTH)),
            pl.BlockSpec((1, blk, A_WIDTH), lambda b, i: (b, i, U_AG // A_WIDTH)),
            ctx_spec, ctx_spec,
        ],
        out_specs=pl.BlockSpec((1, blk, A_WIDTH), lambda b, i: (b, i, 0)),
        out_shape=jax.ShapeDtypeStruct((nb, lb, A_WIDTH), BF16),
        scratch_shapes=[pltpu.VMEM((A_KV_WIDTH, lb), BF16),
                        pltpu.VMEM((A_KV_WIDTH, n_ctx), BF16),
                        pltpu.VMEM((A_KV_WIDTH // LANES, n_ctx, 2 * LANES), BF16)],
        compiler_params=_cparams(("arbitrary", "arbitrary")),
        name="lat_win_attn",
    )(sink, u3, u3, u3, u3, cache_k, cache_v)


NBR_QROWS = 2
NBR_BAND = NA_ROWS + NBR_QROWS


def _lat_nbr_kernel(q_ref, k_ref, v_ref, cg_ref, ck_ref, cv_ref, t2_ref, o_ref,
                    kt_scr, ckt_scr, cv_scr, *, grid_rows):
    i = pl.program_id(1)

    @pl.when(i == 0)
    def _():
        _transpose_keys(k_ref, kt_scr, ck_ref, ckt_scr, cv_ref, cv_scr, LANES)

    r0 = NBR_QROWS * i
    nq = NBR_QROWS * GRID_W
    nk = NBR_BAND * GRID_W
    bs = jnp.clip(r0 - NA_ROWS // 2, 0, grid_rows - NBR_BAND)
    start = pl.multiple_of(bs * GRID_W, LANES)

    q_row = r0 + lax.broadcasted_iota(jnp.int32, (nq, nk), 0) // GRID_W
    rstart = jnp.clip(q_row - NA_ROWS // 2, 0, grid_rows - NA_ROWS)
    k_row = bs + lax.broadcasted_iota(jnp.int32, (nq, nk), 1) // GRID_W
    valid = (k_row >= rstart) & (k_row < rstart + NA_ROWS)

    q = q_ref[0]
    lane = lax.broadcasted_iota(jnp.int32, (nq, LANES), 1)
    low = lane < HEAD_DIM
    def scores(h):
        m, half = divmod(h, 2)
        rows_m = slice(m * LANES, (m + 1) * LANES)
        bias_rows = []
        for a in range(NBR_QROWS):
            tiles = []
            for p in range(NBR_BAND // 2):
                d = (bs + 2 * p) - (r0 + a) + (NA_ROWS - 1)
                idx = jnp.clip(d, -1, 2 * NA_ROWS - 2) + 1
                tiles.append(t2_ref[h, idx])
            bias_rows.append(jnp.concatenate(tiles, axis=1))
        bias = jnp.concatenate(bias_rows, axis=0)
        qp = q[:, rows_m]
        q_half = jnp.where(low if half == 0 else ~low, qp, jnp.zeros_like(qp))
        return _pair_scores(q_half, kt_scr[rows_m, pl.ds(start, nk)],
                            lambda s: jnp.where(valid, s + bias, NEG_INF), ckt_scr[rows_m, :])

    def finish(h, sc):
        m, half = divmod(h, 2)
        rows_m = slice(m * LANES, (m + 1) * LANES)
        v_loc = v_ref[0, pl.ds(start, nk), rows_m]
        return _pair_finish(sc, _with_ones(v_loc), cv_scr[m], None)

    res = _software_pipeline(C_HEADS, scores, finish)
    outs = [jnp.where(low, res[2 * m], res[2 * m + 1]) for m in range(C_HEADS // 2)]
    yc = jnp.concatenate(outs, axis=1)
    cg = cg_ref[0].astype(F32)
    o_ref[0] = (yc * (cg * _sigmoid(cg))).astype(BF16)


def _nbr_bias_table(rpb):
    w = np.arange(GRID_W)
    cstart = np.clip(w - NA_COLS // 2, 0, GRID_W - NA_COLS)
    j = np.arange(GRID_W)
    in_win = (j[None, :] >= cstart[:, None]) & (j[None, :] < cstart[:, None] + NA_COLS)
    n_dcol = 2 * NA_COLS - 1
    n_drow = 2 * NA_ROWS - 1
    dcol = j[None, :] - w[:, None] + NA_COLS - 1
    onehot = (np.arange(n_dcol)[:, None, None] == dcol[None]) & in_win[None]
    rp = jnp.pad(rpb.astype(F32) * LOG2E, ((0, 0), (1, 1), (0, 0)))
    r2 = jnp.concatenate([rp[:, :-1], rp[:, 1:]], axis=-1)
    sel = np.zeros((2, n_dcol, GRID_W, 2, GRID_W), np.float32)
    for half in range(2):
        sel[half, :, :, half, :] = onehot
    t = jnp.einsum('hek,kwsj->hewsj', r2, jnp.asarray(sel.reshape(2 * n_dcol, GRID_W, 2, GRID_W)),
                   precision=HIGHEST)
    d = np.arange(2 * NA_ROWS)[:, None] - 1 + np.arange(2)[None, :]
    ok = ((d >= 0) & (d < n_drow))[:, None, :, None] & in_win[None, :, None, :]
    t = jnp.where(jnp.asarray(ok)[None], t, NEG_INF)
    return t.reshape(rpb.shape[0], 2 * NA_ROWS, GRID_W, 2 * GRID_W)


def _lat_nbr_attn(u, cache_k, cache_v, layer, t2, *, nb, lb):
    grid_rows = lb // GRID_W
    assert grid_rows >= NBR_BAND and grid_rows % NBR_QROWS == 0 and NBR_QROWS * GRID_W == LANES
    nq = NBR_QROWS * GRID_W
    n_ctx = cache_k.shape[2]
    u3 = u.reshape(nb, lb, NAT_W)
    ctx_spec = pl.BlockSpec((1, 1, n_ctx, C_WIDTH), lambda b, i: (b, layer, 0, 0))
    kernel = functools.partial(_lat_nbr_kernel, grid_rows=grid_rows)
    once = pl.Buffered(1)
    return pl.pallas_call(
        kernel,
        grid=(nb, grid_rows // NBR_QROWS),
        in_specs=[
            pl.BlockSpec((1, nq, C_WIDTH), lambda b, i: (b, i, U_CQ // C_WIDTH)),
            pl.BlockSpec((1, lb, C_WIDTH), lambda b, i: (b, 0, U_CK // C_WIDTH), pipeline_mode=once),
            pl.BlockSpec((1, lb, C_WIDTH), lambda b, i: (b, 0, U_CV // C_WIDTH)),
            pl.BlockSpec((1, nq, C_WIDTH), lambda b, i: (b, i, U_CG // C_WIDTH)),
            ctx_spec, ctx_spec,
            pl.BlockSpec(t2.shape, lambda b, i: (0, 0, 0, 0), pipeline_mode=once),
        ],
        out_specs=pl.BlockSpec((1, nq, C_WIDTH), lambda b, i: (b, i, 0)),
        out_shape=jax.ShapeDtypeStruct((nb, lb, C_WIDTH), BF16),
        scratch_shapes=[pltpu.VMEM((C_WIDTH, lb), BF16),
                        pltpu.VMEM((C_WIDTH, n_ctx), BF16),
                        pltpu.VMEM((C_WIDTH // LANES, n_ctx, 2 * LANES), BF16)],
        compiler_params=_cparams(("arbitrary", "arbitrary")),
        name="lat_nbr_attn",
    )(u3, u3, u3, u3, cache_k, cache_v, t2)


HYENA_EMB_PAD = 40


def _filter_kernel(w1t_ref, b1_ref, fr_ref, w2t_ref, b2_ref, w3t_ref, decay_ref, o_ref, z_scr):
    L = o_ref.shape[1]

    @pl.when(pl.program_id(0) == 0)
    def _():
        t = lax.broadcasted_iota(jnp.int32, (HYENA_EMB_PAD, L), 1).astype(F32) / L
        r = lax.broadcasted_iota(jnp.int32, (HYENA_EMB_PAD, L), 0)
        band = ((r - 1) % HYENA_BANDS + 1).astype(F32) * (2.0 * np.pi)
        ang = t * band
        feats = jnp.where(r == 0, t, jnp.where(r <= HYENA_BANDS, jnp.sin(ang), jnp.cos(ang)))
        z = jnp.dot(w1t_ref[...], feats, precision=HIGHEST, preferred_element_type=F32) + b1_ref[...]
        z = jnp.sin(fr_ref[0] * z)
        z = jnp.dot(w2t_ref[...], z, precision=HIGHEST, preferred_element_type=F32) + b2_ref[...]
        z_scr[...] = jnp.sin(fr_ref[1] * z)

    h = jnp.dot(w3t_ref[...], z_scr[...], precision=HIGHEST, preferred_element_type=F32)
    t_row = lax.broadcasted_iota(jnp.int32, (1, L), 1).astype(F32) / L
    o_ref[...] = h * jnp.exp(-jnp.abs(decay_ref[...]) * t_row)


def _hyena_filters(L, hy_w1, hy_b1, hy_w2, hy_b2, hy_freq, hy_w3, hy_decay):
    hidden = hy_w1.shape[1]
    n_out = hy_w3.shape[1]
    rb = 512
    w1t = jnp.pad(hy_w1.T, ((0, 0), (0, HYENA_EMB_PAD - hy_w1.shape[0])))
    const = lambda a: pl.BlockSpec(a.shape, lambda i: (0,) * a.ndim)
    args = (w1t, hy_b1.reshape(hidden, 1), hy_freq.reshape(2, hidden, 1), hy_w2.T, hy_b2.reshape(hidden, 1))
    out = pl.pallas_call(
        _filter_kernel,
        grid=(n_out // rb,),
        in_specs=[const(a) for a in args] + [
            pl.BlockSpec((rb, hidden), lambda i: (i, 0)),
            pl.BlockSpec((rb, 1), lambda i: (i, 0)),
        ],
        out_specs=pl.BlockSpec((rb, L), lambda i: (i, 0)),
        out_shape=jax.ShapeDtypeStruct((n_out, L), F32),
        scratch_shapes=[pltpu.VMEM((hidden, L), F32)],
        compiler_params=_cparams(("arbitrary",)),
        name="hyena_filter",
    )(*args, hy_w3.T, hy_decay.reshape(n_out, 1))
    return out.reshape(HYENA_ORDER, 2, B_WIDTH, L)


def _short_conv_params(hy_conv_w, hy_conv_b):
    return jnp.concatenate([hy_conv_w.T, hy_conv_b[:, None]], axis=1)


def _dense_dft_consts(S):
    n = 2 * S
    t = np.arange(S)[:, None]
    k = np.arange(n)[None, :]
    ang = -2.0 * np.pi * t * k / n
    cr, ci = np.cos(ang), np.sin(ang)
    wf = np.block([[cr, ci], [-ci, cr]])
    er, ei = cr.T, -ci.T
    wi = np.block([[er, ei], [-ei, er]])
    return jnp.asarray(wf, F32), jnp.asarray(wi, F32)


def _ctx_spec_kernel(f_ref, skip_ref, wf_ref, hr_ref, hi_ref, *, S):
    n = 2 * S
    fwd = f_ref[0, 0]
    bwd = f_ref[0, 1]
    lane = lax.broadcasted_iota(jnp.int32, bwd.shape, 1)
    bwd = jnp.where(lane == 0, 0.0, bwd)
    w = wf_ref[0:S, :]
    ff = jnp.dot(fwd, w, precision=HIGHEST, preferred_element_type=F32)
    fb = jnp.dot(bwd, w, precision=HIGHEST, preferred_element_type=F32)
    skip = skip_ref[0]
    hr_ref[0] = (ff[:, :n] + fb[:, :n] + skip) * (1.0 / n)
    hi_ref[0] = (ff[:, n:] - fb[:, n:]) * (1.0 / n)


def _ctx_spectrum(filt, skip, wf, S):
    n = 2 * S
    kernel = functools.partial(_ctx_spec_kernel, S=S)
    return pl.pallas_call(
        kernel,
        grid=(HYENA_ORDER,),
        in_specs=[
            pl.BlockSpec((1, 2, B_WIDTH, S), lambda o: (o, 0, 0, 0)),
            pl.BlockSpec((1, B_WIDTH, 1), lambda o: (o, 0, 0)),
            pl.BlockSpec(wf.shape, lambda o: (0, 0)),
        ],
        out_specs=[pl.BlockSpec((1, B_WIDTH, n), lambda o: (o, 0, 0))] * 2,
        out_shape=[jax.ShapeDtypeStruct((HYENA_ORDER, B_WIDTH, n), F32)] * 2,
        compiler_params=_cparams(("arbitrary",)),
        name="hyena_ctx_spectrum",
    )(filt, skip.reshape(HYENA_ORDER, B_WIDTH, 1), wf)


def _hyena_ctx_kernel(prm_ref, v_ref, x1_ref, x2_ref, bg_ref, hr_ref, hi_ref, wf_ref, wi_ref, o_ref, *, S):
    nb, cb, _ = v_ref.shape
    half = nb // 2
    n = 2 * S
    lane = lax.broadcasted_iota(jnp.int32, (nb, cb, S), 2)

    def short_conv(u, p):
        prev = jnp.where(lane == 0, 0.0, pltpu.roll(u, 1, 2))
        nxt = jnp.where(lane == S - 1, 0.0, pltpu.roll(u, S - 1, 2))
        return p[:, 3:4] + prev * p[:, 0:1] + u * p[:, 1:2] + nxt * p[:, 2:3]

    def stack(u):
        return jnp.concatenate([u[:half], u[half:]], axis=-1)

    def conv(xs, o):
        spec = jnp.dot(xs.reshape(half * cb, 2 * S).astype(BF16), wf_ref[...],
                       preferred_element_type=F32).reshape(half, cb, 2 * n)
        xr, xi = spec[..., :n], spec[..., n:]
        hr, hi = hr_ref[o], hi_ref[o]
        y = jnp.concatenate([xr * hr - xi * hi, xr * hi + xi * hr], axis=-1)
        return jnp.dot(y.reshape(half * cb, 2 * n).astype(BF16), wi_ref[...],
                       preferred_element_type=F32).reshape(half, cb, 2 * S)

    v = stack(short_conv(v_ref[...], prm_ref[0]))
    x1 = stack(short_conv(x1_ref[...], prm_ref[1]))
    x2 = stack(short_conv(x2_ref[...], prm_ref[2]))
    z = x1 * conv(v, 0)
    y = x2 * conv(z, 1)
    bg = stack(bg_ref[...])
    y = y * (bg * _sigmoid(bg))
    o_ref[0:half] = y[..., :S]
    o_ref[half:nb] = y[..., S:]


def _hyena_ctx(t_arr, prm, hr, hi, wf, wi, *, nb, S):
    cb = 16
    n = 2 * S
    ncb = B_WIDTH // cb
    kernel = functools.partial(_hyena_ctx_kernel, S=S)

    def part(k):
        return pl.BlockSpec((nb, cb, S), lambda c: (0, c + k * ncb, 0))

    return pl.pallas_call(
        kernel,
        grid=(ncb,),
        in_specs=[
            pl.BlockSpec((3, cb, 4), lambda c: (0, c, 0)),
            part(0), part(1), part(2), part(3),
            pl.BlockSpec((HYENA_ORDER, cb, n), lambda c: (0, c, 0)),
            pl.BlockSpec((HYENA_ORDER, cb, n), lambda c: (0, c, 0)),
            pl.BlockSpec(wf.shape, lambda c: (0, 0)),
            pl.BlockSpec(wi.shape, lambda c: (0, 0)),
        ],
        out_specs=pl.BlockSpec((nb, cb, S), lambda c: (0, c, 0)),
        out_shape=jax.ShapeDtypeStruct((nb, B_WIDTH, S), F32),
        compiler_params=_cparams(("arbitrary",)),
        name="hyena_ctx",
    )(prm.reshape(3, B_WIDTH, 4), t_arr, t_arr, t_arr, t_arr, hr, hi, wf.astype(BF16), wi.astype(BF16))


def _two_stage_consts(L, nseq, dtype):
    n2 = LANES
    n = 2 * L
    n1 = n // n2
    h1 = n1 // 2
    k1 = np.arange(n1)[:, None]
    a = -2.0 * np.pi * k1 * np.arange(h1)[None, :] / n1
    w1r, w1i = np.cos(a), np.sin(a)
    w1big = np.block([[w1r, -w1i], [w1i, w1r]])
    vr, vi = w1r.T, -w1i.T
    w1inv = np.block([[vr, -vi], [vi, vr]])
    a = -2.0 * np.pi * k1 * np.arange(n2)[None, :] / n
    twr, twi = np.cos(a), np.sin(a)
    a = -2.0 * np.pi * np.arange(n2)[:, None] * np.arange(n2)[None, :] / n2
    w2r, w2i = np.cos(a), np.sin(a)
    w2big = np.block([[w2r, w2i], [-w2i, w2r]])
    w2conj = np.block([[w2r, -w2i], [w2i, w2r]])
    f = lambda x: jnp.asarray(x, F32)
    m = lambda x: jnp.asarray(x, F32).astype(dtype)
    return dict(
        w1b=jnp.broadcast_to(m(w1big)[None], (nseq,) + w1big.shape),
        w1ib=jnp.broadcast_to(m(w1inv)[None], (nseq,) + w1inv.shape),
        w2=m(w2big), w2c=m(w2conj), twr=f(twr), twi=f(twi), n1=n1, h1=h1)


def _dft_dot(spec, a, b):
    if a.dtype == BF16 or b.dtype == BF16:
        return jnp.einsum(spec, a.astype(BF16), b.astype(BF16), preferred_element_type=F32)
    a_hi, b_hi = a.astype(BF16), b.astype(BF16)
    a_lo = (a - a_hi.astype(F32)).astype(BF16)
    b_lo = (b - b_hi.astype(F32)).astype(BF16)
    mm = lambda x, y: jnp.einsum(spec, x, y, preferred_element_type=F32)
    return mm(a_hi, b_hi) + (mm(a_hi, b_lo) + mm(a_lo, b_hi))


def _stage_fwd(xs, w1b, twr, twi, w2):
    s = xs.shape[0]
    n1 = twr.shape[0]
    a = _dft_dot('smk,skn->smn', w1b, xs)
    ar, ai = a[:, :n1], a[:, n1:]
    p = jnp.concatenate([ar * twr - ai * twi, ar * twi + ai * twr], axis=-1)
    return _dft_dot('mk,kn->mn', p.reshape(s * n1, 2 * LANES), w2)


def _stage_inv(cm, w1ib, twr, twi, w2c, s):
    n1 = twr.shape[0]
    dm = _dft_dot('mk,kn->mn', cm, w2c).reshape(s, n1, 2 * LANES)
    dr, di = dm[..., :LANES], dm[..., LANES:]
    r = jnp.concatenate([dr * twr + di * twi, di * twr - dr * twi], axis=1)
    return _dft_dot('smk,skn->smn', w1ib, r)


def _lat_spec_kernel(skip_ref, f_ref, w1b_ref, twr_ref, twi_ref, w2_ref, hr_ref, hi_ref, *, n_fft):
    cb, h1 = f_ref.shape[2], f_ref.shape[3]
    n1 = twr_ref.shape[0]
    fwd = f_ref[0, 0]
    bwd = f_ref[0, 1]
    first = (lax.broadcasted_iota(jnp.int32, bwd.shape, 1) == 0) & \
            (lax.broadcasted_iota(jnp.int32, bwd.shape, 2) == 0)
    bwd = jnp.where(first, 0.0, bwd)
    xs = jnp.concatenate([fwd, bwd], axis=0)
    xs = jnp.concatenate([xs, jnp.zeros_like(xs)], axis=1)
    sp = _stage_fwd(xs, w1b_ref[...], twr_ref[...], twi_ref[...], w2_ref[...]).reshape(2, cb, n1, 2 * LANES)
    inv = 1.0 / n_fft
    for c in range(cb):
        skip = skip_ref[pl.program_id(0), pl.program_id(1) * cb + c]
        hr_ref[0, c] = (sp[0, c, :, :LANES] + sp[1, c, :, :LANES] + skip) * inv
        hi_ref[0, c] = (sp[0, c, :, LANES:] - sp[1, c, :, LANES:]) * inv


def _lat_spectrum(filt, skip, L):
    cb = 16
    cs = _two_stage_consts(L, 2 * cb, F32)
    n1, h1 = cs["n1"], cs["h1"]
    filt5 = filt.reshape(HYENA_ORDER, 2, B_WIDTH, h1, LANES)
    kernel = functools.partial(_lat_spec_kernel, n_fft=2 * L)
    const = lambda a: pl.BlockSpec(a.shape, lambda o, c: (0,) * a.ndim)
    return pl.pallas_call(
        kernel,
        grid=(HYENA_ORDER, B_WIDTH // cb),
        in_specs=[
            pl.BlockSpec(memory_space=pltpu.SMEM),
            pl.BlockSpec((1, 2, cb, h1, LANES), lambda o, c: (o, 0, c, 0, 0)),
            const(cs["w1b"]), const(cs["twr"]), const(cs["twi"]), const(cs["w2"]),
        ],
        out_specs=[pl.BlockSpec((1, cb, n1, LANES), lambda o, c: (o, c, 0, 0))] * 2,
        out_shape=[jax.ShapeDtypeStruct((HYENA_ORDER, B_WIDTH, n1, LANES), F32)] * 2,
        compiler_params=_cparams(("arbitrary", "arbitrary")),
        name="hyena_lat_spectrum",
    )(skip, filt5, cs["w1b"], cs["twr"], cs["twi"], cs["w2"])


def _hyena_lat_kernel(prm_ref, v_ref, x1_ref, x2_ref, bg_ref, hr_ref, hi_ref,
                      w1b_ref, w1ib_ref, twr_ref, twi_ref, w2_ref, w2c_ref, o_ref,
                      vs_ref, x1s_ref, x2s_ref):
    nb, cb, h1, _ = v_ref.shape
    half = nb // 2
    s = cb * half
    n1 = twr_ref.shape[0]
    c0 = pl.program_id(0) * cb
    row = lax.broadcasted_iota(jnp.int32, (nb, h1, LANES), 1)
    lane = lax.broadcasted_iota(jnp.int32, (nb, h1, LANES), 2)

    def short_conv(u, part, c):
        r = pltpu.roll(u, 1, 2)
        prev = jnp.where(lane == 0, pltpu.roll(r, 1, 1), r)
        prev = jnp.where((lane == 0) & (row == 0), 0.0, prev)
        r = pltpu.roll(u, LANES - 1, 2)
        nxt = jnp.where(lane == LANES - 1, pltpu.roll(r, h1 - 1, 1), r)
        nxt = jnp.where((lane == LANES - 1) & (row == h1 - 1), 0.0, nxt)
        ch = part * B_WIDTH + c0 + c
        return prm_ref[3, ch] + prev * prm_ref[0, ch] + u * prm_ref[1, ch] + nxt * prm_ref[2, ch]

    def stack_into(dst_ref, u, c):
        dst_ref[c * half:(c + 1) * half, 0:h1, :] = u[:half]
        dst_ref[c * half:(c + 1) * half, h1:2 * h1, :] = u[half:]

    for c in range(cb):
        stack_into(vs_ref, short_conv(v_ref[:, c], 0, c), c)
        stack_into(x1s_ref, short_conv(x1_ref[:, c], 1, c), c)
        bg = bg_ref[:, c]
        stack_into(x2s_ref, short_conv(x2_ref[:, c], 2, c) * (bg * _sigmoid(bg)), c)

    twr, twi = twr_ref[...], twi_ref[...]

    def conv(xs, o):
        sp = _stage_fwd(xs, w1b_ref[...], twr, twi, w2_ref[...]).reshape(cb, half, n1, 2 * LANES)
        br, bi = sp[..., :LANES], sp[..., LANES:]
        hr = hr_ref[o][:, None]
        hi = hi_ref[o][:, None]
        cm = jnp.concatenate([br * hr - bi * hi, br * hi + bi * hr], axis=-1).reshape(s * n1, 2 * LANES)
        return _stage_inv(cm, w1ib_ref[...], twr, twi, w2c_ref[...], s)

    z = x1s_ref[...] * conv(vs_ref[...], 0)
    y = x2s_ref[...] * conv(z, 1)
    for c in range(cb):
        o_ref[0:half, c] = y[c * half:(c + 1) * half, 0:h1]
        o_ref[half:nb, c] = y[c * half:(c + 1) * half, h1:2 * h1]


def _hyena_lat(t_arr, prm, hr, hi, *, nb, L):
    cb = 8
    half = nb // 2
    s = cb * half
    cs = _two_stage_consts(L, s, BF16)
    n1, h1 = cs["n1"], cs["h1"]
    ncb = B_WIDTH // cb
    t5 = t_arr.reshape(nb, T_W, h1, LANES)

    def part(k):
        return pl.BlockSpec((nb, cb, h1, LANES), lambda c: (0, c + k * ncb, 0, 0))

    const = lambda a: pl.BlockSpec(a.shape, lambda c: (0,) * a.ndim)
    return pl.pallas_call(
        _hyena_lat_kernel,
        grid=(ncb,),
        in_specs=[
            pl.BlockSpec(memory_space=pltpu.SMEM),
            part(0), part(1), part(2), part(3),
            pl.BlockSpec((HYENA_ORDER, cb, n1, LANES), lambda c: (0, c, 0, 0)),
            pl.BlockSpec((HYENA_ORDER, cb, n1, LANES), lambda c: (0, c, 0, 0)),
            const(cs["w1b"]), const(cs["w1ib"]), const(cs["twr"]), const(cs["twi"]),
            const(cs["w2"]), const(cs["w2c"]),
        ],
        out_specs=pl.BlockSpec((nb, cb, h1, LANES), lambda c: (0, c, 0, 0)),
        out_shape=jax.ShapeDtypeStruct((nb, B_WIDTH, h1, LANES), F32),
        scratch_shapes=[pltpu.VMEM((s, 2 * h1, LANES), F32)] * 3,
        compiler_params=_cparams(("arbitrary",)),
        name="hyena_lat",
    )(prm.T, t5, t5, t5, t5, hr, hi,
      cs["w1b"], cs["w1ib"], cs["twr"], cs["twi"], cs["w2"], cs["w2c"])


def _merge_kernel(x_ref, ya_ref, ybt_ref, yc_ref, mg_ref, gate_ref, wa_ref, wb_ref, wc_ref, wo_ref,
                  nw_ref, *rest, final):
    if final:
        (o_ref,) = rest
    else:
        sc_ref, sh_ref, o_ref, h_ref = rest
    mg = mg_ref[...]
    tm = mg.shape[0]
    if len(ybt_ref.shape) == 4:
        n_rows = tm // LANES
        row0 = (pl.program_id(0) % (ybt_ref.shape[2] // n_rows)) * n_rows
        yb = jnp.concatenate([ybt_ref[0, :, row0 + r, :].T for r in range(n_rows)], axis=0).astype(BF16)
    else:
        yb = ybt_ref[0].T.astype(BF16)
    m = (_sigmoid(mg[:, :D_MODEL].astype(F32)) * jnp.dot(ya_ref[...], wa_ref[...], preferred_element_type=F32)
         + _sigmoid(mg[:, D_MODEL:2 * D_MODEL].astype(F32)) * jnp.dot(yb, wb_ref[...], preferred_element_type=F32)
         + _sigmoid(mg[:, 2 * D_MODEL:].astype(F32)) * jnp.dot(yc_ref[...], wc_ref[...], preferred_element_type=F32))
    out = jnp.dot(m.astype(BF16), wo_ref[...], preferred_element_type=F32)
    xn = x_ref[...] + gate_ref[0] * out
    if final:
        var = jnp.mean(xn * xn, axis=-1, keepdims=True)
        o_ref[...] = xn * lax.rsqrt(var + EPS) * nw_ref[...]
    else:
        o_ref[...] = xn
        h_ref[...] = _modulated_norm(xn, nw_ref[...], sc_ref[0], sh_ref[0])


def _merge(x2d, ya, ybt, yc, u, gate, wa, wb, wc, wo, nw, next_mod, *, nb, lb, tm):
    rows = nb * lb
    bpb = lb // tm
    per_mod = gate.shape[0] > 1
    final = next_mod is None
    kernel = functools.partial(_merge_kernel, final=final)
    mod_spec = pl.BlockSpec((1, 1, D_MODEL), lambda i: ((i // bpb) if per_mod else 0, 0, 0))
    row_spec = pl.BlockSpec((tm, D_MODEL), lambda i: (i, 0))
    const = lambda a: pl.BlockSpec(a.shape, lambda i: (0,) * a.ndim, pipeline_mode=pl.Buffered(1))
    if ybt.ndim == 4:
        sub = 8 * LANES // tm
        assert tm % LANES == 0 and (8 * LANES) % tm == 0 and bpb % sub == 0
        yb_spec = pl.BlockSpec((1, B_WIDTH, 8, LANES), lambda i: (i // bpb, 0, (i % bpb) // sub, 0))
    else:
        yb_spec = pl.BlockSpec((1, B_WIDTH, tm), lambda i: (i // bpb, 0, i % bpb))
    return pl.pallas_call(
        kernel,
        grid=(rows // tm,),
        in_specs=[
            pl.BlockSpec((tm, D_MODEL), lambda i: (i, 0)),
            pl.BlockSpec((tm, A_WIDTH), lambda i: (i, 0)),
            yb_spec,
            pl.BlockSpec((tm, C_WIDTH), lambda i: (i, 0)),
            pl.BlockSpec((tm, MG_W), lambda i: (i, U_MG // MG_W)),
            mod_spec,
            const(wa), const(wb), const(wc), const(wo),
            pl.BlockSpec((1, D_MODEL), lambda i: (0, 0)),
        ] + ([] if final else [mod_spec, mod_spec]),
        out_specs=row_spec if final else [row_spec, row_spec],
        out_shape=(jax.ShapeDtypeStruct((rows, D_MODEL), F32) if final else
                   [jax.ShapeDtypeStruct((rows, D_MODEL), F32), jax.ShapeDtypeStruct((rows, D_MODEL), BF16)]),
        compiler_params=_cparams(("arbitrary",)),
        name="merge_final" if final else "merge",
    )(x2d, ya.reshape(rows, A_WIDTH), ybt, yc.reshape(rows, C_WIDTH), u, gate, wa, wb, wc, wo,
      nw.reshape(1, D_MODEL), *(() if final else next_mod))


def _rope_tables(L):
    t = np.arange(L)
    row = (t // GRID_W).astype(np.float32)
    col = (t % GRID_W).astype(np.float32)
    nf = HEAD_DIM // 4
    inv = np.power(np.float32(ROPE_BASE), -np.arange(nf, dtype=np.float32) / nf).astype(np.float32)
    ang = np.concatenate([row[:, None] * inv[None], col[:, None] * inv[None]], axis=-1)
    cos, sin = np.cos(ang), np.sin(ang)
    reps = LANES // HEAD_DIM
    cos_t = np.tile(np.concatenate([cos, cos], axis=-1), (1, reps))
    sin_t = np.tile(np.concatenate([-sin, sin], axis=-1), (1, reps))
    return jnp.asarray(cos_t, F32), jnp.asarray(sin_t, F32)


def _perm_heads(w, base, axis):
    return [lax.slice_in_dim(w, base + h * HEAD_DIM, base + (h + 1) * HEAD_DIM, axis=axis) for h in A_PERM]


def _prep_w_in_kernel(w_ref, nat_ref, t_ref):
    def put(dst, src, width):
        nat_ref[0, :, dst:dst + width] = w_ref[0, :, src:src + width].astype(BF16)

    put(U_MG, IN_MG, MG_W)
    for dst, k in ((U_CQ, 0), (U_CK, 1), (U_CV, 2), (U_CG, 3)):
        put(dst, IN_CQ + k * C_WIDTH, C_WIDTH)
    for p, h in enumerate(A_PERM):
        put(U_AQ + p * HEAD_DIM, IN_AQ + h * HEAD_DIM, HEAD_DIM)
        put(U_AG + p * HEAD_DIM, IN_AG + h * HEAD_DIM, HEAD_DIM)
    put(U_AK, IN_AK, A_KV_WIDTH)
    put(U_AV, IN_AV, A_KV_WIDTH)
    t_ref[0] = w_ref[0, :, IN_BU:IN_BU + T_W].T.astype(BF16)


def _prep_w_in(w_in):
    depth, _, in_w = w_in.shape
    rb = 128
    return pl.pallas_call(
        _prep_w_in_kernel,
        grid=(depth, D_MODEL // rb),
        in_specs=[pl.BlockSpec((1, rb, in_w), lambda l, i: (l, i, 0))],
        out_specs=[pl.BlockSpec((1, rb, NAT_W), lambda l, i: (l, i, 0)),
                   pl.BlockSpec((1, T_W, rb), lambda l, i: (l, 0, i))],
        out_shape=[jax.ShapeDtypeStruct((depth, D_MODEL, NAT_W), BF16),
                   jax.ShapeDtypeStruct((depth, T_W, D_MODEL), BF16)],
        compiler_params=_cparams(("arbitrary", "arbitrary")),
        name="prep_w_in",
    )(w_in)


def kernel(x_prompt, x_sample, c, cache_a_k, cache_a_v, cache_c_k, cache_c_v, c_ctx, norm_w, w_ada, b_ada, w_in, a_sink, hy_conv_w, hy_conv_b, hy_w1, hy_b1, hy_w2, hy_b2, hy_freq, hy_w3, hy_decay, hy_skip, c_rpb, w_up_a, w_up_b, w_up_c, w_out, final_norm_w):
    nbc, S, _ = x_prompt.shape
    nbl, L, _ = x_sample.shape
    depth = w_in.shape[0]
    n_ctx = cache_a_k.shape[2]
    assert nbc % 2 == 0 and nbl % 2 == 0 and L % (GRID_W * LANES // 2) == 0

    pad = (-(nbl + 1)) % 8
    cond = jnp.concatenate([c, c_ctx[None], jnp.zeros((pad, D_MODEL), F32)], axis=0)
    mod = _adaln(cond, w_ada, b_ada)

    cos_t, sin_t = _rope_tables(L)
    tm_ctx = 1024 if (nbc * S) % 1024 == 0 else S
    zeros_t = jnp.zeros((tm_ctx, LANES), F32)
    wf_c, wi_c = _dense_dft_consts(S)
    ca_k = cache_a_k.reshape(nbl, depth, n_ctx, A_KV_WIDTH)
    ca_v = cache_a_v.reshape(nbl, depth, n_ctx, A_KV_WIDTH)
    cc_k = cache_c_k.reshape(nbl, depth, n_ctx, C_WIDTH)
    cc_v = cache_c_v.reshape(nbl, depth, n_ctx, C_WIDTH)

    w_nat, w_t = _prep_w_in(w_in)

    def mod_parts(l, rows):
        m = mod[l, rows][:, None, :]
        return 1.0 + m[..., D_MODEL:2 * D_MODEL], m[..., :D_MODEL], m[..., 2 * D_MODEL:]

    ctx_rows, lat_rows = slice(nbl, nbl + 1), slice(0, nbl)
    xp = x_prompt.reshape(nbc * S, D_MODEL)
    xs = x_sample.reshape(nbl * L, D_MODEL)
    hp = hs = None
    aks, avs, cks, cvs = [], [], [], []
    for l in range(depth):
        final = l == depth - 1
        next_nw = final_norm_w if final else norm_w[l + 1]
        wa = jnp.concatenate(_perm_heads(w_up_a[l], 0, 0), axis=0).astype(BF16)
        wb, wc, wo = (w.astype(BF16) for w in (w_up_b[l], w_up_c[l], w_out[l]))
        prm = _short_conv_params(hy_conv_w[l], hy_conv_b[l])
        filt_args = (hy_w1[l], hy_b1[l], hy_w2[l], hy_b2[l], hy_freq[l], hy_w3[l], hy_decay[l])
        t2 = _nbr_bias_table(c_rpb[l])

        sc, sh, gate = mod_parts(l, ctx_rows)
        u, t_arr, kv = _inproj(*((xp, (sc, sh, norm_w[l])) if hp is None else (hp, None)),
                               w_nat, w_t, zeros_t, zeros_t,
                               layer=l, nb=nbc, lb=S, tm=tm_ctx // 2 if hp is None and tm_ctx > S else tm_ctx,
                               rope=False, kv_f32=True)
        cks.append(kv[:, :U_CV - U_CK].reshape(nbc, S, C_HEADS, HEAD_DIM))
        cvs.append(kv[:, U_CV - U_CK:U_AK - U_CK].reshape(nbc, S, C_HEADS, HEAD_DIM))
        aks.append(kv[:, U_AK - U_CK:U_AV - U_CK].reshape(nbc, S, A_KV_HEADS, HEAD_DIM))
        avs.append(kv[:, U_AV - U_CK:].reshape(nbc, S, A_KV_HEADS, HEAD_DIM))
        ya, yc = _ctx_attn(a_sink[l], u, nb=nbc, lb=S)
        hr, hi = _ctx_spectrum(_hyena_filters(S, *filt_args), hy_skip[l], wf_c, S)
        ybt = _hyena_ctx(t_arr, prm, hr, hi, wf_c, wi_c, nb=nbc, S=S)
        res = _merge(xp, ya, ybt, yc, u, gate, wa, wb, wc, wo, next_nw,
                     None if final else mod_parts(l + 1, ctx_rows)[:2], nb=nbc, lb=S, tm=S)
        xp, hp = (res, None) if final else res

        sc, sh, gate = mod_parts(l, lat_rows)
        u, t_arr = _inproj(*((xs, (sc, sh, norm_w[l])) if hs is None else (hs, None)),
                           w_nat, w_t, cos_t, sin_t,
                           layer=l, nb=nbl, lb=L, tm=1024, rope=True, kv_f32=False)
        ya = _lat_win_attn(a_sink[l], u, ca_k, ca_v, l, nb=nbl, lb=L)
        yc = _lat_nbr_attn(u, cc_k, cc_v, l, t2, nb=nbl, lb=L)
        hr, hi = _lat_spectrum(_hyena_filters(L, *filt_args), hy_skip[l], L)
        ybt = _hyena_lat(t_arr, prm, hr, hi, nb=nbl, L=L)
        res = _merge(xs, ya, ybt, yc, u, gate, wa, wb, wc, wo, next_nw,
                     None if final else mod_parts(l + 1, lat_rows)[:2], nb=nbl, lb=L, tm=256)
        xs, hs = (res, None) if final else res

    y_prompt = xp.reshape(nbc, S, D_MODEL)
    y_sample = xs.reshape(nbl, L, D_MODEL)
    return (y_prompt, y_sample, jnp.stack(aks, axis=1), jnp.stack(avs, axis=1),
            jnp.stack(cks, axis=1), jnp.stack(cvs, axis=1))
```

```python
import functools

import numpy as np
import jax
import jax.numpy as jnp
from jax import lax
from jax.experimental import pallas as pl
from jax.experimental.pallas import tpu as pltpu

F32 = jnp.float32
BF16 = jnp.bfloat16
HIGHEST = lax.Precision.HIGHEST

D_MODEL = 2048
HEAD_DIM = 64
A_HEADS = 12
A_KV_HEADS = 4
A_GROUP = A_HEADS // A_KV_HEADS
A_WIDTH = A_HEADS * HEAD_DIM
A_KV_WIDTH = A_KV_HEADS * HEAD_DIM
A_WINDOW = 128
B_WIDTH = 512
HYENA_ORDER = 2
HYENA_BANDS = 16
C_HEADS = 12
C_WIDTH = C_HEADS * HEAD_DIM
GRID_W = 64
NA_ROWS = 8
NA_COLS = 16
ROPE_BASE = 10000.0
EPS = 1e-6
NEG_INF = -1e30
ATTN_SCALE = HEAD_DIM ** -0.5
LOG2E = 1.4426950408889634
QK_SCALE = ATTN_SCALE * LOG2E

LANES = 128
VMEM_LIMIT = 56 * 1024 * 1024

IN_AQ, IN_AK, IN_AV, IN_AG = 0, 768, 1024, 1280
IN_BU, IN_CQ, IN_MG = 2048, 4096, 7168
PROJ_TN = 1024
MG_W = 3 * D_MODEL
U_MG = 0
U_CQ = U_MG + MG_W
U_CG = U_CQ + C_WIDTH
U_AQ = U_CG + C_WIDTH
U_AG = U_AQ + A_WIDTH
U_CK = U_AG + A_WIDTH
U_CV = U_CK + C_WIDTH
U_AK = U_CV + C_WIDTH
U_AV = U_AK + A_KV_WIDTH
NAT_W = U_AV + A_KV_WIDTH
U_KV_W = NAT_W - U_CK
T_W = 4 * B_WIDTH

A_PERM = tuple(2 * A_GROUP * m + A_GROUP * half + j
               for m in range(A_KV_HEADS // 2) for j in range(A_GROUP) for half in range(2))


def _cparams(sem):
    return pltpu.CompilerParams(dimension_semantics=sem, vmem_limit_bytes=VMEM_LIMIT)


def _sigmoid(x):
    return 1.0 / (1.0 + jnp.exp(-x))


def _lane_chunks(off, width):
    assert off % LANES == 0 and width % LANES == 0
    per_blk = PROJ_TN // LANES
    return [divmod(g, per_blk) for g in range(off // LANES, (off + width) // LANES)]


def _adaln_kernel(cond_ref, w_ref, b_ref, o_ref):
    c = cond_ref[...]
    s = (c * _sigmoid(c)).astype(BF16)
    acc = jnp.dot(s, w_ref[0].astype(BF16), preferred_element_type=F32)
    o_ref[0] = acc + b_ref[0]


def _adaln(cond, w_ada, b_ada):
    depth = w_ada.shape[0]
    rows = cond.shape[0]
    tn = 1024
    return pl.pallas_call(
        _adaln_kernel,
        grid=(depth, 3 * D_MODEL // tn),
        in_specs=[
            pl.BlockSpec((rows, D_MODEL), lambda l, j: (0, 0)),
            pl.BlockSpec((1, D_MODEL, tn), lambda l, j: (l, 0, j)),
            pl.BlockSpec((1, 1, tn), lambda l, j: (l, 0, j)),
        ],
        out_specs=pl.BlockSpec((1, rows, tn), lambda l, j: (l, 0, j)),
        out_shape=jax.ShapeDtypeStruct((depth, rows, 3 * D_MODEL), F32),
        compiler_params=_cparams(("arbitrary", "arbitrary")),
        name="adaln",
    )(cond, w_ada, b_ada.reshape(depth, 1, 3 * D_MODEL))


def _modulated_norm(x, nw, sc, sh):
    var = jnp.mean(x * x, axis=-1, keepdims=True)
    return ((x * lax.rsqrt(var + EPS) * nw) * sc + sh).astype(BF16)


def _norm_mod_kernel(x_ref, sc_ref, sh_ref, nw_ref, h_ref):
    h_ref[...] = _modulated_norm(x_ref[...], nw_ref[...], sc_ref[0], sh_ref[0])


def _norm_mod(x2d, sc, sh, norm_w, *, nb, lb):
    tm = 256
    bpb = lb // tm
    per_mod = sc.shape[0] > 1
    mod_spec = pl.BlockSpec((1, 1, D_MODEL), lambda i: ((i // bpb) if per_mod else 0, 0, 0))
    row_spec = pl.BlockSpec((tm, D_MODEL), lambda i: (i, 0))
    return pl.pallas_call(
        _norm_mod_kernel,
        grid=(nb * lb // tm,),
        in_specs=[row_spec, mod_spec, mod_spec, pl.BlockSpec((1, D_MODEL), lambda i: (0, 0))],
        out_specs=row_spec,
        out_shape=jax.ShapeDtypeStruct((nb * lb, D_MODEL), BF16),
        compiler_params=_cparams(("arbitrary",)),
        name="norm_mod",
    )(x2d, sc, sh, norm_w.reshape(1, D_MODEL))


def _inproj_kernel(h_ref, w_ref, wt_ref, cos_ref, sin_ref, u_ref, t_ref, *maybe_kv_ref, rope, tm, lb):
    j = pl.program_id(1)
    tn = PROJ_TN
    n_nat = NAT_W // tn
    kv_blk0 = U_CK // tn

    def rotate(a):
        lane = lax.broadcasted_iota(jnp.int32, (tm, LANES), 1)
        first = (lane % HEAD_DIM) < (HEAD_DIM // 2)
        partner = jnp.where(first, pltpu.roll(a, LANES - HEAD_DIM // 2, 1), pltpu.roll(a, HEAD_DIM // 2, 1))
        return a * cos_ref[...] + partner * sin_ref[...]

    def nat_step(chunk_ops, keep_f32):
        acc = jnp.dot(h_ref[...], w_ref[...], preferred_element_type=F32)
        if keep_f32:
            maybe_kv_ref[0][...] = acc
        if not chunk_ops:
            u_ref[...] = acc.astype(u_ref.dtype)
            return
        for ci in range(tn // LANES):
            a = acc[:, ci * LANES:(ci + 1) * LANES]
            for op in chunk_ops.get(ci, ()):
                a = rotate(a) if op == "rope" else a * QK_SCALE
            u_ref[:, ci * LANES:(ci + 1) * LANES] = a.astype(u_ref.dtype)

    special = {}
    for off, width, ops in ((U_CQ, C_WIDTH, ("scale",)),
                            (U_AQ, A_WIDTH, ("rope", "scale") if rope else ("scale",)),
                            (U_AK, A_KV_WIDTH, ("rope",) if rope else ())):
        for blk, ci in (_lane_chunks(off, width) if ops else ()):
            special.setdefault(blk, {})[ci] = ops
    if maybe_kv_ref:
        assert not rope
        for blk in range(kv_blk0, n_nat):
            special.setdefault(blk, {})

    plain = j < n_nat
    for blk, chunk_ops in special.items():
        plain = plain & (j != blk)
        pl.when(j == blk)(functools.partial(nat_step, chunk_ops, bool(maybe_kv_ref) and blk >= kv_blk0))
    pl.when(plain)(functools.partial(nat_step, None, False))

    @pl.when(j >= n_nat)
    def _():
        acc_t = lax.dot_general(wt_ref[...], h_ref[...], (((1,), (1,)), ((), ())),
                                preferred_element_type=F32)
        if tm <= lb:
            t_ref[0] = acc_t
        else:
            for k in range(tm // lb):
                t_ref[k] = acc_t[:, k * lb:(k + 1) * lb]


def _inproj(h2d, w_nat, w_t, cos_t, sin_t, *, layer, nb, lb, tm, rope, kv_f32):
    rows = nb * lb
    tn = PROJ_TN
    n_nat, n_t = NAT_W // tn, T_W // tn
    kv_blk0, n_kv = U_CK // tn, U_KV_W // tn
    assert U_CK % tn == 0 and kv_blk0 + n_kv == n_nat
    bpb = max(lb // tm, 1)

    if tm <= lb:
        t_spec = pl.BlockSpec((1, tn, tm), lambda i, j: (i // bpb, jnp.clip(j - n_nat, 0, n_t - 1), i % bpb))
    else:
        t_spec = pl.BlockSpec((tm // lb, tn, lb), lambda i, j: (i, jnp.clip(j - n_nat, 0, n_t - 1), 0))

    out_specs = [pl.BlockSpec((tm, tn), lambda i, j: (i, jnp.minimum(j, n_nat - 1))), t_spec]
    out_shape = [jax.ShapeDtypeStruct((rows, NAT_W), BF16), jax.ShapeDtypeStruct((nb, T_W, lb), F32)]
    if kv_f32:
        out_specs.append(pl.BlockSpec((tm, tn), lambda i, j: (i, jnp.clip(j - kv_blk0, 0, n_kv - 1))))
        out_shape.append(jax.ShapeDtypeStruct((rows, U_KV_W), F32))

    kernel = functools.partial(_inproj_kernel, rope=rope, tm=tm, lb=lb)
    return pl.pallas_call(
        kernel,
        grid=(rows // tm, n_nat + n_t),
        in_specs=[
            pl.BlockSpec((tm, D_MODEL), lambda i, j: (i, 0)),
            pl.BlockSpec((None, D_MODEL, tn), lambda i, j: (layer, 0, jnp.minimum(j, n_nat - 1))),
            pl.BlockSpec((None, tn, D_MODEL), lambda i, j: (layer, jnp.maximum(j - n_nat, 0), 0)),
            pl.BlockSpec((tm, LANES), lambda i, j: (i % bpb, 0)),
            pl.BlockSpec((tm, LANES), lambda i, j: (i % bpb, 0)),
        ],
        out_specs=out_specs,
        out_shape=out_shape,
        compiler_params=_cparams(("arbitrary", "arbitrary")),
        name="inproj_rope" if rope else "inproj",
    )(h2d, w_nat, w_t, cos_t, sin_t)


def _softmax_pv(s, v, sink_col):
    m = jnp.max(s, axis=-1, keepdims=True)
    if sink_col is not None:
        m = jnp.maximum(m, sink_col)
    e = jnp.exp2(s - m)
    den = jnp.sum(e, axis=-1, keepdims=True)
    if sink_col is not None:
        den = den + jnp.exp2(sink_col - m)
    o = jnp.dot(e.astype(BF16), v, preferred_element_type=F32)
    return o * (1.0 / den)


def _sink_column(sink_ref, heads, rows_per_head):
    n = len(heads)
    row_head = lax.broadcasted_iota(jnp.int32, (n * rows_per_head, 1), 0) // rows_per_head
    col = jnp.full((n * rows_per_head, 1), sink_ref[heads[-1]], F32)
    for idx in range(n - 1):
        col = jnp.where(row_head == idx, sink_ref[heads[idx]], col)
    return col * LOG2E


def _ctx_attn_kernel(sink_ref, qg_ref, kv0_ref, kv1_ref, ya_ref, yc_ref):
    qg = qg_ref[0]
    kv = jnp.concatenate([kv0_ref[0], kv1_ref[0]], axis=1)
    s_len = qg.shape[0]
    nt = (((1,), (1,)), ((), ()))
    o_cg, o_aq, o_ag = U_CG - U_CQ, U_AQ - U_CQ, U_AG - U_CQ
    o_cv, o_ak, o_av = U_CV - U_CK, U_AK - U_CK, U_AV - U_CK
    head = lambda x, off, h: x[:, off + h * HEAD_DIM:off + (h + 1) * HEAD_DIM]
    a_heads = lambda g: [A_GROUP * g + hh for hh in range(A_GROUP)]

    def scores(job):
        if job < A_KV_HEADS:
            q = jnp.concatenate([head(qg, o_aq, A_PERM.index(h)) for h in a_heads(job)], axis=0)
            k = head(kv, o_ak, job)
        else:
            q = head(qg, 0, job - A_KV_HEADS)
            k = head(kv, 0, job - A_KV_HEADS)
        return lax.dot_general(q, k, nt, preferred_element_type=F32)

    def finish(job, s):
        if job < A_KV_HEADS:
            return _softmax_pv(s, head(kv, o_av, job), _sink_column(sink_ref, a_heads(job), s_len))
        return _softmax_pv(s, head(kv, o_cv, job - A_KV_HEADS), None)

    res = _software_pipeline(A_KV_HEADS + C_HEADS, scores, finish)
    pieces = [None] * A_HEADS
    for g in range(A_KV_HEADS):
        for hh, h in enumerate(a_heads(g)):
            pieces[A_PERM.index(h)] = res[g][hh * s_len:(hh + 1) * s_len]
    ya = jnp.concatenate(pieces, axis=1)
    ag = qg[:, o_ag:o_ag + A_WIDTH].astype(F32)
    ya_ref[0] = (ya * (ag * _sigmoid(ag))).astype(BF16)
    yc = jnp.concatenate(res[A_KV_HEADS:], axis=1)
    cg = qg[:, o_cg:o_cg + C_WIDTH].astype(F32)
    yc_ref[0] = (yc * (cg * _sigmoid(cg))).astype(BF16)


def _ctx_attn(sink, u, *, nb, lb):
    u3 = u.reshape(nb, lb, NAT_W)
    qg_w = U_CK - U_CQ
    assert U_CQ % qg_w == 0 and U_CK % PROJ_TN == 0 and U_KV_W == 2 * PROJ_TN
    return pl.pallas_call(
        _ctx_attn_kernel,
        grid=(nb,),
        in_specs=[
            pl.BlockSpec(memory_space=pltpu.SMEM),
            pl.BlockSpec((1, lb, qg_w), lambda b: (b, 0, U_CQ // qg_w)),
            pl.BlockSpec((1, lb, PROJ_TN), lambda b: (b, 0, U_CK // PROJ_TN)),
            pl.BlockSpec((1, lb, PROJ_TN), lambda b: (b, 0, U_CK // PROJ_TN + 1)),
        ],
        out_specs=[
            pl.BlockSpec((1, lb, A_WIDTH), lambda b: (b, 0, 0)),
            pl.BlockSpec((1, lb, C_WIDTH), lambda b: (b, 0, 0)),
        ],
        out_shape=[
            jax.ShapeDtypeStruct((nb, lb, A_WIDTH), BF16),
            jax.ShapeDtypeStruct((nb, lb, C_WIDTH), BF16),
        ],
        compiler_params=_cparams(("arbitrary",)),
        name="ctx_attn",
    )(sink, u3, u3, u3)


def _with_ones(v_pair):
    return jnp.concatenate([v_pair, jnp.ones_like(v_pair)], axis=1)


def _transpose_keys(k_ref, kt_scr, ck_ref, ckt_scr, cv_ref, cv_scr, chunk):
    n_tok = k_ref.shape[1]

    def body(c, carry):
        start = pl.multiple_of(c * chunk, chunk)
        blk = k_ref[0, pl.ds(start, chunk), :].astype(F32)
        kt_scr[:, pl.ds(start, chunk)] = blk.T.astype(BF16)
        return carry

    lax.fori_loop(0, n_tok // chunk, body, 0)
    ckt_scr[...] = ck_ref[0, 0].T.astype(BF16)
    ctx_v = cv_ref[0, 0].astype(BF16)
    for m in range(ctx_v.shape[1] // LANES):
        cv_scr[m] = _with_ones(ctx_v[:, m * LANES:(m + 1) * LANES])


PIPELINE_DEPTH = 2


def _software_pipeline(n, first_stage, second_stage):
    pending = [first_stage(h) for h in range(min(PIPELINE_DEPTH, n))]
    res = []
    for h in range(n):
        if h + PIPELINE_DEPTH < n:
            pending.append(first_stage(h + PIPELINE_DEPTH))
        res.append(second_stage(h, pending.pop(0)))
    return res


def _pair_scores(q_half, kt_loc, bias_fn, kt_ctx):
    s_loc = bias_fn(jnp.dot(q_half, kt_loc, preferred_element_type=F32))
    s_ctx = jnp.dot(q_half, kt_ctx, preferred_element_type=F32)
    return s_loc, s_ctx


def _pair_finish(scores, v_loc, v_ctx, sink_col):
    s_loc, s_ctx = scores
    m = jnp.maximum(jnp.max(s_loc, axis=-1, keepdims=True), jnp.max(s_ctx, axis=-1, keepdims=True))
    if sink_col is not None:
        m = jnp.maximum(m, sink_col)
    e_loc = jnp.exp2(s_loc - m).astype(BF16)
    e_ctx = jnp.exp2(s_ctx - m).astype(BF16)
    o = (jnp.dot(e_loc, v_loc, preferred_element_type=F32)
         + jnp.dot(e_ctx, v_ctx, preferred_element_type=F32))
    den = o[:, LANES:]
    if sink_col is not None:
        den = den + jnp.exp2(sink_col - m)
    return o[:, :LANES] * (1.0 / den)


def _lat_win_kernel(sink_ref, q_ref, k_ref, v_ref, ag_ref, ck_ref, cv_ref, o_ref,
                    kt_scr, ckt_scr, cv_scr):
    i = pl.program_id(1)
    nblk = pl.num_programs(1)
    blk = A_WINDOW
    span = 3 * blk

    @pl.when(i == 0)
    def _():
        _transpose_keys(k_ref, kt_scr, ck_ref, ckt_scr, cv_ref, cv_scr, blk)

    start = pl.multiple_of(jnp.clip(i - 1, 0, nblk - 3) * blk, blk)
    q_pos = i * blk + lax.broadcasted_iota(jnp.int32, (blk, span), 0)
    k_pos = start + lax.broadcasted_iota(jnp.int32, (blk, span), 1)
    valid = jnp.abs(q_pos - k_pos) <= A_WINDOW
    mask = lambda s: jnp.where(valid, s, NEG_INF)

    q = q_ref[0]
    lane = lax.broadcasted_iota(jnp.int32, (blk, LANES), 1)
    low = lane < HEAD_DIM

    def scores(job):
        pair, half = divmod(job, 2)
        rows_m = slice(pair // A_GROUP * LANES, (pair // A_GROUP + 1) * LANES)
        qp = q[:, pair * LANES:(pair + 1) * LANES]
        q_half = jnp.where(low if half == 0 else ~low, qp, jnp.zeros_like(qp))
        return _pair_scores(q_half, kt_scr[rows_m, pl.ds(start, span)], mask, ckt_scr[rows_m, :])

    def finish(job, sc):
        pair, half = divmod(job, 2)
        rows_m = slice(pair // A_GROUP * LANES, (pair // A_GROUP + 1) * LANES)
        v_loc = v_ref[0, pl.ds(start, span), rows_m]
        return _pair_finish(sc, _with_ones(v_loc), cv_scr[pair // A_GROUP],
                            sink_ref[A_PERM[job]] * LOG2E)

    res = _software_pipeline(A_HEADS, scores, finish)
    ya = jnp.concatenate([jnp.where(low, res[2 * p], res[2 * p + 1]) for p in range(A_HEADS // 2)], axis=1)
    ag = ag_ref[0].astype(F32)
    o_ref[0] = (ya * (ag * _sigmoid(ag))).astype(BF16)


def _lat_win_attn(sink, u, cache_k, cache_v, layer, *, nb, lb):
    blk = A_WINDOW
    nblk = lb // blk
    assert nblk >= 3
    n_ctx = cache_k.shape[2]
    u3 = u.reshape(nb, lb, NAT_W)
    ctx_spec = pl.BlockSpec((1, 1, n_ctx, A_KV_WIDTH), lambda b, i: (b, layer, 0, 0))
    return pl.pallas_call(
        _lat_win_kernel,
        grid=(nb, nblk),
        in_specs=[
            pl.BlockSpec(memory_space=pltpu.SMEM),
            pl.BlockSpec((1, blk, A_WIDTH), lambda b, i: (b, i, U_AQ // A_WIDTH)),
            pl.BlockSpec((1, lb, A_KV_WIDTH), lambda b, i: (b, 0, U_AK // A_KV_WIDTH)),
            pl.BlockSpec((1, lb, A_KV_WIDTH), lambda b, i: (b, 0, U_AV // A_KV_WIDTH)),
            pl.BlockSpec((1, blk, A_WIDTH), lambda b, i: (b, i, U_AG // A_WIDTH)),
            ctx_spec, ctx_spec,
        ],
        out_specs=pl.BlockSpec((1, blk, A_WIDTH), lambda b, i: (b, i, 0)),
        out_shape=jax.ShapeDtypeStruct((nb, lb, A_WIDTH), BF16),
        scratch_shapes=[pltpu.VMEM((A_KV_WIDTH, lb), BF16),
                        pltpu.VMEM((A_KV_WIDTH, n_ctx), BF16),
                        pltpu.VMEM((A_KV_WIDTH // LANES, n_ctx, 2 * LANES), BF16)],
        compiler_params=_cparams(("arbitrary", "arbitrary")),
        name="lat_win_attn",
    )(sink, u3, u3, u3, u3, cache_k, cache_v)


NBR_QROWS = 2
NBR_BAND = NA_ROWS + NBR_QROWS


def _lat_nbr_kernel(q_ref, k_ref, v_ref, cg_ref, ck_ref, cv_ref, t2_ref, o_ref,
                    kt_scr, ckt_scr, cv_scr, *, grid_rows):
    i = pl.program_id(1)

    @pl.when(i == 0)
    def _():
        _transpose_keys(k_ref, kt_scr, ck_ref, ckt_scr, cv_ref, cv_scr, LANES)

    r0 = NBR_QROWS * i
    nq = NBR_QROWS * GRID_W
    nk = NBR_BAND * GRID_W
    bs = jnp.clip(r0 - NA_ROWS // 2, 0, grid_rows - NBR_BAND)
    start = pl.multiple_of(bs * GRID_W, LANES)

    q_row = r0 + lax.broadcasted_iota(jnp.int32, (nq, nk), 0) // GRID_W
    rstart = jnp.clip(q_row - NA_ROWS // 2, 0, grid_rows - NA_ROWS)
    k_row = bs + lax.broadcasted_iota(jnp.int32, (nq, nk), 1) // GRID_W
    valid = (k_row >= rstart) & (k_row < rstart + NA_ROWS)

    q = q_ref[0]
    lane = lax.broadcasted_iota(jnp.int32, (nq, LANES), 1)
    low = lane < HEAD_DIM
    def scores(h):
        m, half = divmod(h, 2)
        rows_m = slice(m * LANES, (m + 1) * LANES)
        bias_rows = []
        for a in range(NBR_QROWS):
            tiles = []
            for p in range(NBR_BAND // 2):
                d = (bs + 2 * p) - (r0 + a) + (NA_ROWS - 1)
                idx = jnp.clip(d, -1, 2 * NA_ROWS - 2) + 1
                tiles.append(t2_ref[h, idx])
            bias_rows.append(jnp.concatenate(tiles, axis=1))
        bias = jnp.concatenate(bias_rows, axis=0)
        qp = q[:, rows_m]
        q_half = jnp.where(low if half == 0 else ~low, qp, jnp.zeros_like(qp))
        return _pair_scores(q_half, kt_scr[rows_m, pl.ds(start, nk)],
                            lambda s: jnp.where(valid, s + bias, NEG_INF), ckt_scr[rows_m, :])

    def finish(h, sc):
        m, half = divmod(h, 2)
        rows_m = slice(m * LANES, (m + 1) * LANES)
        v_loc = v_ref[0, pl.ds(start, nk), rows_m]
        return _pair_finish(sc, _with_ones(v_loc), cv_scr[m], None)

    res = _software_pipeline(C_HEADS, scores, finish)
    outs = [jnp.where(low, res[2 * m], res[2 * m + 1]) for m in range(C_HEADS // 2)]
    yc = jnp.concatenate(outs, axis=1)
    cg = cg_ref[0].astype(F32)
    o_ref[0] = (yc * (cg * _sigmoid(cg))).astype(BF16)


def _nbr_bias_table(rpb):
    w = np.arange(GRID_W)
    cstart = np.clip(w - NA_COLS // 2, 0, GRID_W - NA_COLS)
    j = np.arange(GRID_W)
    in_win = (j[None, :] >= cstart[:, None]) & (j[None, :] < cstart[:, None] + NA_COLS)
    n_dcol = 2 * NA_COLS - 1
    n_drow = 2 * NA_ROWS - 1
    dcol = j[None, :] - w[:, None] + NA_COLS - 1
    onehot = (np.arange(n_dcol)[:, None, None] == dcol[None]) & in_win[None]
    rp = jnp.pad(rpb.astype(F32) * LOG2E, ((0, 0), (1, 1), (0, 0)))
    r2 = jnp.concatenate([rp[:, :-1], rp[:, 1:]], axis=-1)
    sel = np.zeros((2, n_dcol, GRID_W, 2, GRID_W), np.float32)
    for half in range(2):
        sel[half, :, :, half, :] = onehot
    t = jnp.einsum('hek,kwsj->hewsj', r2, jnp.asarray(sel.reshape(2 * n_dcol, GRID_W, 2, GRID_W)),
                   precision=HIGHEST)
    d = np.arange(2 * NA_ROWS)[:, None] - 1 + np.arange(2)[None, :]
    ok = ((d >= 0) & (d < n_drow))[:, None, :, None] & in_win[None, :, None, :]
    t = jnp.where(jnp.asarray(ok)[None], t, NEG_INF)
    return t.reshape(rpb.shape[0], 2 * NA_ROWS, GRID_W, 2 * GRID_W)


def _lat_nbr_attn(u, cache_k, cache_v, layer, t2, *, nb, lb):
    grid_rows = lb // GRID_W
    assert grid_rows >= NBR_BAND and grid_rows % NBR_QROWS == 0 and NBR_QROWS * GRID_W == LANES
    nq = NBR_QROWS * GRID_W
    n_ctx = cache_k.shape[2]
    u3 = u.reshape(nb, lb, NAT_W)
    ctx_spec = pl.BlockSpec((1, 1, n_ctx, C_WIDTH), lambda b, i: (b, layer, 0, 0))
    kernel = functools.partial(_lat_nbr_kernel, grid_rows=grid_rows)
    once = pl.Buffered(1)
    return pl.pallas_call(
        kernel,
        grid=(nb, grid_rows // NBR_QROWS),
        in_specs=[
            pl.BlockSpec((1, nq, C_WIDTH), lambda b, i: (b, i, U_CQ // C_WIDTH)),
            pl.BlockSpec((1, lb, C_WIDTH), lambda b, i: (b, 0, U_CK // C_WIDTH), pipeline_mode=once),
            pl.BlockSpec((1, lb, C_WIDTH), lambda b, i: (b, 0, U_CV // C_WIDTH)),
            pl.BlockSpec((1, nq, C_WIDTH), lambda b, i: (b, i, U_CG // C_WIDTH)),
            ctx_spec, ctx_spec,
            pl.BlockSpec(t2.shape, lambda b, i: (0, 0, 0, 0), pipeline_mode=once),
        ],
        out_specs=pl.BlockSpec((1, nq, C_WIDTH), lambda b, i: (b, i, 0)),
        out_shape=jax.ShapeDtypeStruct((nb, lb, C_WIDTH), BF16),
        scratch_shapes=[pltpu.VMEM((C_WIDTH, lb), BF16),
                        pltpu.VMEM((C_WIDTH, n_ctx), BF16),
                        pltpu.VMEM((C_WIDTH // LANES, n_ctx, 2 * LANES), BF16)],
        compiler_params=_cparams(("arbitrary", "arbitrary")),
        name="lat_nbr_attn",
    )(u3, u3, u3, u3, cache_k, cache_v, t2)


HYENA_EMB_PAD = 40


def _filter_kernel(w1t_ref, b1_ref, fr_ref, w2t_ref, b2_ref, w3t_ref, decay_ref, o_ref, z_scr):
    L = o_ref.shape[1]

    @pl.when(pl.program_id(0) == 0)
    def _():
        t = lax.broadcasted_iota(jnp.int32, (HYENA_EMB_PAD, L), 1).astype(F32) / L
        r = lax.broadcasted_iota(jnp.int32, (HYENA_EMB_PAD, L), 0)
        band = ((r - 1) % HYENA_BANDS + 1).astype(F32) * (2.0 * np.pi)
        ang = t * band
        feats = jnp.where(r == 0, t, jnp.where(r <= HYENA_BANDS, jnp.sin(ang), jnp.cos(ang)))
        z = jnp.dot(w1t_ref[...], feats, precision=HIGHEST, preferred_element_type=F32) + b1_ref[...]
        z = jnp.sin(fr_ref[0] * z)
        z = jnp.dot(w2t_ref[...], z, precision=HIGHEST, preferred_element_type=F32) + b2_ref[...]
        z_scr[...] = jnp.sin(fr_ref[1] * z)

    h = jnp.dot(w3t_ref[...], z_scr[...], precision=HIGHEST, preferred_element_type=F32)
    t_row = lax.broadcasted_iota(jnp.int32, (1, L), 1).astype(F32) / L
    o_ref[...] = h * jnp.exp(-jnp.abs(decay_ref[...]) * t_row)


def _hyena_filters(L, hy_w1, hy_b1, hy_w2, hy_b2, hy_freq, hy_w3, hy_decay):
    hidden = hy_w1.shape[1]
    n_out = hy_w3.shape[1]
    rb = 512
    w1t = jnp.pad(hy_w1.T, ((0, 0), (0, HYENA_EMB_PAD - hy_w1.shape[0])))
    const = lambda a: pl.BlockSpec(a.shape, lambda i: (0,) * a.ndim)
    args = (w1t, hy_b1.reshape(hidden, 1), hy_freq.reshape(2, hidden, 1), hy_w2.T, hy_b2.reshape(hidden, 1))
    out = pl.pallas_call(
        _filter_kernel,
        grid=(n_out // rb,),
        in_specs=[const(a) for a in args] + [
            pl.BlockSpec((rb, hidden), lambda i: (i, 0)),
            pl.BlockSpec((rb, 1), lambda i: (i, 0)),
        ],
        out_specs=pl.BlockSpec((rb, L), lambda i: (i, 0)),
        out_shape=jax.ShapeDtypeStruct((n_out, L), F32),
        scratch_shapes=[pltpu.VMEM((hidden, L), F32)],
        compiler_params=_cparams(("arbitrary",)),
        name="hyena_filter",
    )(*args, hy_w3.T, hy_decay.reshape(n_out, 1))
    return out.reshape(HYENA_ORDER, 2, B_WIDTH, L)


def _short_conv_params(hy_conv_w, hy_conv_b):
    return jnp.concatenate([hy_conv_w.T, hy_conv_b[:, None]], axis=1)


def _dense_dft_consts(S):
    n = 2 * S
    t = np.arange(S)[:, None]
    k = np.arange(n)[None, :]
    ang = -2.0 * np.pi * t * k / n
    cr, ci = np.cos(ang), np.sin(ang)
    wf = np.block([[cr, ci], [-ci, cr]])
    er, ei = cr.T, -ci.T
    wi = np.block([[er, ei], [-ei, er]])
    return jnp.asarray(wf, F32), jnp.asarray(wi, F32)


def _ctx_spec_kernel(f_ref, skip_ref, wf_ref, hr_ref, hi_ref, *, S):
    n = 2 * S
    fwd = f_ref[0, 0]
    bwd = f_ref[0, 1]
    lane = lax.broadcasted_iota(jnp.int32, bwd.shape, 1)
    bwd = jnp.where(lane == 0, 0.0, bwd)
    w = wf_ref[0:S, :]
    ff = jnp.dot(fwd, w, precision=HIGHEST, preferred_element_type=F32)
    fb = jnp.dot(bwd, w, precision=HIGHEST, preferred_element_type=F32)
    skip = skip_ref[0]
    hr_ref[0] = (ff[:, :n] + fb[:, :n] + skip) * (1.0 / n)
    hi_ref[0] = (ff[:, n:] - fb[:, n:]) * (1.0 / n)


def _ctx_spectrum(filt, skip, wf, S):
    n = 2 * S
    kernel = functools.partial(_ctx_spec_kernel, S=S)
    return pl.pallas_call(
        kernel,
        grid=(HYENA_ORDER,),
        in_specs=[
            pl.BlockSpec((1, 2, B_WIDTH, S), lambda o: (o, 0, 0, 0)),
            pl.BlockSpec((1, B_WIDTH, 1), lambda o: (o, 0, 0)),
            pl.BlockSpec(wf.shape, lambda o: (0, 0)),
        ],
        out_specs=[pl.BlockSpec((1, B_WIDTH, n), lambda o: (o, 0, 0))] * 2,
        out_shape=[jax.ShapeDtypeStruct((HYENA_ORDER, B_WIDTH, n), F32)] * 2,
        compiler_params=_cparams(("arbitrary",)),
        name="hyena_ctx_spectrum",
    )(filt, skip.reshape(HYENA_ORDER, B_WIDTH, 1), wf)


def _hyena_ctx_kernel(prm_ref, v_ref, x1_ref, x2_ref, bg_ref, hr_ref, hi_ref, wf_ref, wi_ref, o_ref, *, S):
    nb, cb, _ = v_ref.shape
    half = nb // 2
    n = 2 * S
    lane = lax.broadcasted_iota(jnp.int32, (nb, cb, S), 2)

    def short_conv(u, p):
        prev = jnp.where(lane == 0, 0.0, pltpu.roll(u, 1, 2))
        nxt = jnp.where(lane == S - 1, 0.0, pltpu.roll(u, S - 1, 2))
        return p[:, 3:4] + prev * p[:, 0:1] + u * p[:, 1:2] + nxt * p[:, 2:3]

    def stack(u):
        return jnp.concatenate([u[:half], u[half:]], axis=-1)

    def conv(xs, o):
        spec = jnp.dot(xs.reshape(half * cb, 2 * S).astype(BF16), wf_ref[...],
                       preferred_element_type=F32).reshape(half, cb, 2 * n)
        xr, xi = spec[..., :n], spec[..., n:]
        hr, hi = hr_ref[o], hi_ref[o]
        y = jnp.concatenate([xr * hr - xi * hi, xr * hi + xi * hr], axis=-1)
        return jnp.dot(y.reshape(half * cb, 2 * n).astype(BF16), wi_ref[...],
                       preferred_element_type=F32).reshape(half, cb, 2 * S)

    v = stack(short_conv(v_ref[...], prm_ref[0]))
    x1 = stack(short_conv(x1_ref[...], prm_ref[1]))
    x2 = stack(short_conv(x2_ref[...], prm_ref[2]))
    z = x1 * conv(v, 0)
    y = x2 * conv(z, 1)
    bg = stack(bg_ref[...])
    y = y * (bg * _sigmoid(bg))
    o_ref[0:half] = y[..., :S]
    o_ref[half:nb] = y[..., S:]


def _hyena_ctx(t_arr, prm, hr, hi, wf, wi, *, nb, S):
    cb = 16
    n = 2 * S
    ncb = B_WIDTH // cb
    kernel = functools.partial(_hyena_ctx_kernel, S=S)

    def part(k):
        return pl.BlockSpec((nb, cb, S), lambda c: (0, c + k * ncb, 0))

    return pl.pallas_call(
        kernel,
        grid=(ncb,),
        in_specs=[
            pl.BlockSpec((3, cb, 4), lambda c: (0, c, 0)),
            part(0), part(1), part(2), part(3),
            pl.BlockSpec((HYENA_ORDER, cb, n), lambda c: (0, c, 0)),
            pl.BlockSpec((HYENA_ORDER, cb, n), lambda c: (0, c, 0)),
            pl.BlockSpec(wf.shape, lambda c: (0, 0)),
            pl.BlockSpec(wi.shape, lambda c: (0, 0)),
        ],
        out_specs=pl.BlockSpec((nb, cb, S), lambda c: (0, c, 0)),
        out_shape=jax.ShapeDtypeStruct((nb, B_WIDTH, S), F32),
        compiler_params=_cparams(("arbitrary",)),
        name="hyena_ctx",
    )(prm.reshape(3, B_WIDTH, 4), t_arr, t_arr, t_arr, t_arr, hr, hi, wf.astype(BF16), wi.astype(BF16))


def _two_stage_consts(L, nseq, dtype):
    n2 = LANES
    n = 2 * L
    n1 = n // n2
    h1 = n1 // 2
    k1 = np.arange(n1)[:, None]
    a = -2.0 * np.pi * k1 * np.arange(h1)[None, :] / n1
    w1r, w1i = np.cos(a), np.sin(a)
    w1big = np.block([[w1r, -w1i], [w1i, w1r]])
    vr, vi = w1r.T, -w1i.T
    w1inv = np.block([[vr, -vi], [vi, vr]])
    a = -2.0 * np.pi * k1 * np.arange(n2)[None, :] / n
    twr, twi = np.cos(a), np.sin(a)
    a = -2.0 * np.pi * np.arange(n2)[:, None] * np.arange(n2)[None, :] / n2
    w2r, w2i = np.cos(a), np.sin(a)
    w2big = np.block([[w2r, w2i], [-w2i, w2r]])
    w2conj = np.block([[w2r, -w2i], [w2i, w2r]])
    f = lambda x: jnp.asarray(x, F32)
    m = lambda x: jnp.asarray(x, F32).astype(dtype)
    return dict(
        w1b=jnp.broadcast_to(m(w1big)[None], (nseq,) + w1big.shape),
        w1ib=jnp.broadcast_to(m(w1inv)[None], (nseq,) + w1inv.shape),
        w2=m(w2big), w2c=m(w2conj), twr=f(twr), twi=f(twi), n1=n1, h1=h1)


def _dft_dot(spec, a, b):
    if a.dtype == BF16 or b.dtype == BF16:
        return jnp.einsum(spec, a.astype(BF16), b.astype(BF16), preferred_element_type=F32)
    a_hi, b_hi = a.astype(BF16), b.astype(BF16)
    a_lo = (a - a_hi.astype(F32)).astype(BF16)
    b_lo = (b - b_hi.astype(F32)).astype(BF16)
    mm = lambda x, y: jnp.einsum(spec, x, y, preferred_element_type=F32)
    return mm(a_hi, b_hi) + (mm(a_hi, b_lo) + mm(a_lo, b_hi))


def _cmul(ar, ai, br, bi):
    return ar * br - ai * bi, ar * bi + ai * br


def _stage_fwd(xs, w1b, twr, twi, w2):
    s = xs.shape[0]
    n1 = twr.shape[0]
    a = _dft_dot('smk,skn->smn', w1b, xs)
    p = jnp.concatenate(_cmul(a[:, :n1], a[:, n1:], twr, twi), axis=-1)
    return _dft_dot('mk,kn->mn', p.reshape(s * n1, 2 * LANES), w2)


def _stage_inv(cm, w1ib, twr, twi, w2c, s):
    n1 = twr.shape[0]
    dm = _dft_dot('mk,kn->mn', cm, w2c).reshape(s, n1, 2 * LANES)
    r = jnp.concatenate(_cmul(dm[..., :LANES], dm[..., LANES:], twr, -twi), axis=1)
    return _dft_dot('smk,skn->smn', w1ib, r)


def _lat_spec_kernel(skip_ref, f_ref, w1b_ref, twr_ref, twi_ref, w2_ref, hr_ref, hi_ref, *, n_fft):
    cb, h1 = f_ref.shape[2], f_ref.shape[3]
    n1 = twr_ref.shape[0]
    fwd = f_ref[0, 0]
    bwd = f_ref[0, 1]
    first = (lax.broadcasted_iota(jnp.int32, bwd.shape, 1) == 0) & \
            (lax.broadcasted_iota(jnp.int32, bwd.shape, 2) == 0)
    bwd = jnp.where(first, 0.0, bwd)
    xs = jnp.concatenate([fwd, bwd], axis=0)
    w1_real = w1b_ref[:, :, 0:h1]
    sp = _stage_fwd(xs, w1_real, twr_ref[...], twi_ref[...], w2_ref[...]).reshape(2, cb, n1, 2 * LANES)
    inv = 1.0 / n_fft
    for c in range(cb):
        skip = skip_ref[pl.program_id(0), pl.program_id(1) * cb + c]
        hr_ref[0, c] = (sp[0, c, :, :LANES] + sp[1, c, :, :LANES] + skip) * inv
        hi_ref[0, c] = (sp[0, c, :, LANES:] - sp[1, c, :, LANES:]) * inv


def _lat_spectrum(filt, skip, L):
    cb = 16
    cs = _two_stage_consts(L, 2 * cb, F32)
    n1, h1 = cs["n1"], cs["h1"]
    filt5 = filt.reshape(HYENA_ORDER, 2, B_WIDTH, h1, LANES)
    kernel = functools.partial(_lat_spec_kernel, n_fft=2 * L)
    const = lambda a: pl.BlockSpec(a.shape, lambda o, c: (0,) * a.ndim)
    return pl.pallas_call(
        kernel,
        grid=(HYENA_ORDER, B_WIDTH // cb),
        in_specs=[
            pl.BlockSpec(memory_space=pltpu.SMEM),
            pl.BlockSpec((1, 2, cb, h1, LANES), lambda o, c: (o, 0, c, 0, 0)),
            const(cs["w1b"]), const(cs["twr"]), const(cs["twi"]), const(cs["w2"]),
        ],
        out_specs=[pl.BlockSpec((1, cb, n1, LANES), lambda o, c: (o, c, 0, 0))] * 2,
        out_shape=[jax.ShapeDtypeStruct((HYENA_ORDER, B_WIDTH, n1, LANES), F32)] * 2,
        compiler_params=_cparams(("arbitrary", "arbitrary")),
        name="hyena_lat_spectrum",
    )(skip, filt5, cs["w1b"], cs["twr"], cs["twi"], cs["w2"])


def _hyena_lat_kernel(prm_ref, v_ref, x1_ref, x2_ref, bg_ref, hr_ref, hi_ref,
                      w1b_ref, w1ib_ref, twr_ref, twi_ref, w2_ref, w2c_ref, o_ref,
                      vs_ref, x1s_ref, x2s_ref):
    nb, cb, h1, _ = v_ref.shape
    half = nb // 2
    s = cb * half
    n1 = twr_ref.shape[0]
    c0 = pl.program_id(0) * cb
    row = lax.broadcasted_iota(jnp.int32, (nb, h1, LANES), 1)
    lane = lax.broadcasted_iota(jnp.int32, (nb, h1, LANES), 2)

    def short_conv(u, part, c):
        r = pltpu.roll(u, 1, 2)
        prev = jnp.where(lane == 0, pltpu.roll(r, 1, 1), r)
        prev = jnp.where((lane == 0) & (row == 0), 0.0, prev)
        r = pltpu.roll(u, LANES - 1, 2)
        nxt = jnp.where(lane == LANES - 1, pltpu.roll(r, h1 - 1, 1), r)
        nxt = jnp.where((lane == LANES - 1) & (row == h1 - 1), 0.0, nxt)
        ch = part * B_WIDTH + c0 + c
        return prm_ref[3, ch] + prev * prm_ref[0, ch] + u * prm_ref[1, ch] + nxt * prm_ref[2, ch]

    def stack_into(dst_ref, u, c):
        dst_ref[c * half:(c + 1) * half, 0:h1, :] = u[:half]
        dst_ref[c * half:(c + 1) * half, h1:2 * h1, :] = u[half:]

    for c in range(cb):
        stack_into(vs_ref, short_conv(v_ref[:, c], 0, c), c)
        stack_into(x1s_ref, short_conv(x1_ref[:, c], 1, c), c)
        bg = bg_ref[:, c]
        stack_into(x2s_ref, short_conv(x2_ref[:, c], 2, c) * (bg * _sigmoid(bg)), c)

    twr, twi = twr_ref[...], twi_ref[...]

    def conv(xs, o):
        sp = _stage_fwd(xs, w1b_ref[...], twr, twi, w2_ref[...]).reshape(cb, half, n1, 2 * LANES)
        hr = hr_ref[o][:, None]
        hi = hi_ref[o][:, None]
        cm = jnp.concatenate(_cmul(sp[..., :LANES], sp[..., LANES:], hr, hi), axis=-1)
        return _stage_inv(cm.reshape(s * n1, 2 * LANES), w1ib_ref[...], twr, twi, w2c_ref[...], s)

    z = x1s_ref[...] * conv(vs_ref[...], 0)
    y = x2s_ref[...] * conv(z, 1)
    for c in range(cb):
        o_ref[0:half, c] = y[c * half:(c + 1) * half, 0:h1]
        o_ref[half:nb, c] = y[c * half:(c + 1) * half, h1:2 * h1]


def _hyena_lat(t_arr, prm, hr, hi, *, nb, L):
    cb = 8
    half = nb // 2
    s = cb * half
    cs = _two_stage_consts(L, s, BF16)
    n1, h1 = cs["n1"], cs["h1"]
    ncb = B_WIDTH // cb
    t5 = t_arr.reshape(nb, T_W, h1, LANES)

    def part(k):
        return pl.BlockSpec((nb, cb, h1, LANES), lambda c: (0, c + k * ncb, 0, 0))

    const = lambda a: pl.BlockSpec(a.shape, lambda c: (0,) * a.ndim)
    return pl.pallas_call(
        _hyena_lat_kernel,
        grid=(ncb,),
        in_specs=[
            pl.BlockSpec(memory_space=pltpu.SMEM),
            part(0), part(1), part(2), part(3),
            pl.BlockSpec((HYENA_ORDER, cb, n1, LANES), lambda c: (0, c, 0, 0)),
            pl.BlockSpec((HYENA_ORDER, cb, n1, LANES), lambda c: (0, c, 0, 0)),
            const(cs["w1b"]), const(cs["w1ib"]), const(cs["twr"]), const(cs["twi"]),
            const(cs["w2"]), const(cs["w2c"]),
        ],
        out_specs=pl.BlockSpec((nb, cb, h1, LANES), lambda c: (0, c, 0, 0)),
        out_shape=jax.ShapeDtypeStruct((nb, B_WIDTH, h1, LANES), F32),
        scratch_shapes=[pltpu.VMEM((s, 2 * h1, LANES), F32)] * 3,
        compiler_params=_cparams(("arbitrary",)),
        name="hyena_lat",
    )(prm.T, t5, t5, t5, t5, hr, hi,
      cs["w1b"], cs["w1ib"], cs["twr"], cs["twi"], cs["w2"], cs["w2c"])


def _merge_kernel(x_ref, ya_ref, ybt_ref, yc_ref, mg_ref, gate_ref, wa_ref, wb_ref, wc_ref, wo_ref,
                  nw_ref, *rest, final):
    if final:
        (o_ref,) = rest
    else:
        sc_ref, sh_ref, o_ref, h_ref = rest
    mg = mg_ref[...]
    tm = mg.shape[0]
    if len(ybt_ref.shape) == 4:
        n_rows = tm // LANES
        row0 = (pl.program_id(0) % (ybt_ref.shape[2] // n_rows)) * n_rows
        yb = jnp.concatenate([ybt_ref[0, :, row0 + r, :].T for r in range(n_rows)], axis=0).astype(BF16)
    else:
        yb = ybt_ref[0].T.astype(BF16)
    m = (_sigmoid(mg[:, :D_MODEL].astype(F32)) * jnp.dot(ya_ref[...], wa_ref[...], preferred_element_type=F32)
         + _sigmoid(mg[:, D_MODEL:2 * D_MODEL].astype(F32)) * jnp.dot(yb, wb_ref[...], preferred_element_type=F32)
         + _sigmoid(mg[:, 2 * D_MODEL:].astype(F32)) * jnp.dot(yc_ref[...], wc_ref[...], preferred_element_type=F32))
    out = jnp.dot(m.astype(BF16), wo_ref[...], preferred_element_type=F32)
    xn = x_ref[...] + gate_ref[0] * out
    if final:
        var = jnp.mean(xn * xn, axis=-1, keepdims=True)
        o_ref[...] = xn * lax.rsqrt(var + EPS) * nw_ref[...]
    else:
        o_ref[...] = xn
        h_ref[...] = _modulated_norm(xn, nw_ref[...], sc_ref[0], sh_ref[0])


def _merge(x2d, ya, ybt, yc, u, gate, wa, wb, wc, wo, nw, next_mod, *, nb, lb, tm):
    rows = nb * lb
    bpb = lb // tm
    per_mod = gate.shape[0] > 1
    final = next_mod is None
    kernel = functools.partial(_merge_kernel, final=final)
    mod_spec = pl.BlockSpec((1, 1, D_MODEL), lambda i: ((i // bpb) if per_mod else 0, 0, 0))
    row_spec = pl.BlockSpec((tm, D_MODEL), lambda i: (i, 0))
    const = lambda a: pl.BlockSpec(a.shape, lambda i: (0,) * a.ndim, pipeline_mode=pl.Buffered(1))
    if ybt.ndim == 4:
        sub = 8 * LANES // tm
        assert tm % LANES == 0 and (8 * LANES) % tm == 0 and bpb % sub == 0
        yb_spec = pl.BlockSpec((1, B_WIDTH, 8, LANES), lambda i: (i // bpb, 0, (i % bpb) // sub, 0))
    else:
        yb_spec = pl.BlockSpec((1, B_WIDTH, tm), lambda i: (i // bpb, 0, i % bpb))
    return pl.pallas_call(
        kernel,
        grid=(rows // tm,),
        in_specs=[
            pl.BlockSpec((tm, D_MODEL), lambda i: (i, 0)),
            pl.BlockSpec((tm, A_WIDTH), lambda i: (i, 0)),
            yb_spec,
            pl.BlockSpec((tm, C_WIDTH), lambda i: (i, 0)),
            pl.BlockSpec((tm, MG_W), lambda i: (i, U_MG // MG_W)),
            mod_spec,
            const(wa), const(wb), const(wc), const(wo),
            pl.BlockSpec((1, D_MODEL), lambda i: (0, 0)),
        ] + ([] if final else [mod_spec, mod_spec]),
        out_specs=row_spec if final else [row_spec, row_spec],
        out_shape=(jax.ShapeDtypeStruct((rows, D_MODEL), F32) if final else
                   [jax.ShapeDtypeStruct((rows, D_MODEL), F32), jax.ShapeDtypeStruct((rows, D_MODEL), BF16)]),
        compiler_params=_cparams(("arbitrary",)),
        name="merge_final" if final else "merge",
    )(x2d, ya.reshape(rows, A_WIDTH), ybt, yc.reshape(rows, C_WIDTH), u, gate, wa, wb, wc, wo,
      nw.reshape(1, D_MODEL), *(() if final else next_mod))


def _rope_tables(L):
    t = np.arange(L)
    row = (t // GRID_W).astype(np.float32)
    col = (t % GRID_W).astype(np.float32)
    nf = HEAD_DIM // 4
    inv = np.power(np.float32(ROPE_BASE), -np.arange(nf, dtype=np.float32) / nf).astype(np.float32)
    ang = np.concatenate([row[:, None] * inv[None], col[:, None] * inv[None]], axis=-1)
    cos, sin = np.cos(ang), np.sin(ang)
    reps = LANES // HEAD_DIM
    cos_t = np.tile(np.concatenate([cos, cos], axis=-1), (1, reps))
    sin_t = np.tile(np.concatenate([-sin, sin], axis=-1), (1, reps))
    return jnp.asarray(cos_t, F32), jnp.asarray(sin_t, F32)


def _perm_heads(w, base, axis):
    return [lax.slice_in_dim(w, base + h * HEAD_DIM, base + (h + 1) * HEAD_DIM, axis=axis) for h in A_PERM]


def _prep_w_in_kernel(w_ref, nat_ref, t_ref):
    def put(dst, src, width):
        nat_ref[0, :, dst:dst + width] = w_ref[0, :, src:src + width].astype(BF16)

    put(U_MG, IN_MG, MG_W)
    for dst, k in ((U_CQ, 0), (U_CK, 1), (U_CV, 2), (U_CG, 3)):
        put(dst, IN_CQ + k * C_WIDTH, C_WIDTH)
    for p, h in enumerate(A_PERM):
        put(U_AQ + p * HEAD_DIM, IN_AQ + h * HEAD_DIM, HEAD_DIM)
        put(U_AG + p * HEAD_DIM, IN_AG + h * HEAD_DIM, HEAD_DIM)
    put(U_AK, IN_AK, A_KV_WIDTH)
    put(U_AV, IN_AV, A_KV_WIDTH)
    t_ref[0] = w_ref[0, :, IN_BU:IN_BU + T_W].T.astype(BF16)


def _prep_w_in(w_in):
    depth, _, in_w = w_in.shape
    rb = 128
    return pl.pallas_call(
        _prep_w_in_kernel,
        grid=(depth, D_MODEL // rb),
        in_specs=[pl.BlockSpec((1, rb, in_w), lambda l, i: (l, i, 0))],
        out_specs=[pl.BlockSpec((1, rb, NAT_W), lambda l, i: (l, i, 0)),
                   pl.BlockSpec((1, T_W, rb), lambda l, i: (l, 0, i))],
        out_shape=[jax.ShapeDtypeStruct((depth, D_MODEL, NAT_W), BF16),
                   jax.ShapeDtypeStruct((depth, T_W, D_MODEL), BF16)],
        compiler_params=_cparams(("arbitrary", "arbitrary")),
        name="prep_w_in",
    )(w_in)


def kernel(x_prompt, x_sample, c, cache_a_k, cache_a_v, cache_c_k, cache_c_v, c_ctx, norm_w, w_ada, b_ada, w_in, a_sink, hy_conv_w, hy_conv_b, hy_w1, hy_b1, hy_w2, hy_b2, hy_freq, hy_w3, hy_decay, hy_skip, c_rpb, w_up_a, w_up_b, w_up_c, w_out, final_norm_w):
    nbc, S, _ = x_prompt.shape
    nbl, L, _ = x_sample.shape
    depth = w_in.shape[0]
    n_ctx = cache_a_k.shape[2]
    assert nbc % 2 == 0 and nbl % 2 == 0 and L % (GRID_W * LANES // 2) == 0

    pad = (-(nbl + 1)) % 8
    cond = jnp.concatenate([c, c_ctx[None], jnp.zeros((pad, D_MODEL), F32)], axis=0)
    mod = _adaln(cond, w_ada, b_ada)

    cos_t, sin_t = _rope_tables(L)
    tm_ctx = 1024 if (nbc * S) % 1024 == 0 else S
    zeros_t = jnp.zeros((tm_ctx, LANES), F32)
    wf_c, wi_c = _dense_dft_consts(S)
    ca_k = cache_a_k.reshape(nbl, depth, n_ctx, A_KV_WIDTH)
    ca_v = cache_a_v.reshape(nbl, depth, n_ctx, A_KV_WIDTH)
    cc_k = cache_c_k.reshape(nbl, depth, n_ctx, C_WIDTH)
    cc_v = cache_c_v.reshape(nbl, depth, n_ctx, C_WIDTH)

    w_nat, w_t = _prep_w_in(w_in)

    def mod_parts(l, rows):
        m = mod[l, rows][:, None, :]
        return 1.0 + m[..., D_MODEL:2 * D_MODEL], m[..., :D_MODEL], m[..., 2 * D_MODEL:]

    ctx_rows, lat_rows = slice(nbl, nbl + 1), slice(0, nbl)
    xp = x_prompt.reshape(nbc * S, D_MODEL)
    xs = x_sample.reshape(nbl * L, D_MODEL)
    hp = _norm_mod(xp, *mod_parts(0, ctx_rows)[:2], norm_w[0], nb=nbc, lb=S)
    hs = _norm_mod(xs, *mod_parts(0, lat_rows)[:2], norm_w[0], nb=nbl, lb=L)
    aks, avs, cks, cvs = [], [], [], []
    for l in range(depth):
        final = l == depth - 1
        next_nw = final_norm_w if final else norm_w[l + 1]
        wa = jnp.concatenate(_perm_heads(w_up_a[l], 0, 0), axis=0).astype(BF16)
        wb, wc, wo = (w.astype(BF16) for w in (w_up_b[l], w_up_c[l], w_out[l]))
        prm = _short_conv_params(hy_conv_w[l], hy_conv_b[l])
        filt_args = (hy_w1[l], hy_b1[l], hy_w2[l], hy_b2[l], hy_freq[l], hy_w3[l], hy_decay[l])
        t2 = _nbr_bias_table(c_rpb[l])

        gate = mod_parts(l, ctx_rows)[2]
        u, t_arr, kv = _inproj(hp, w_nat, w_t, zeros_t, zeros_t,
                               layer=l, nb=nbc, lb=S, tm=tm_ctx, rope=False, kv_f32=True)
        cks.append(kv[:, :U_CV - U_CK].reshape(nbc, S, C_HEADS, HEAD_DIM))
        cvs.append(kv[:, U_CV - U_CK:U_AK - U_CK].reshape(nbc, S, C_HEADS, HEAD_DIM))
        aks.append(kv[:, U_AK - U_CK:U_AV - U_CK].reshape(nbc, S, A_KV_HEADS, HEAD_DIM))
        avs.append(kv[:, U_AV - U_CK:].reshape(nbc, S, A_KV_HEADS, HEAD_DIM))
        ya, yc = _ctx_attn(a_sink[l], u, nb=nbc, lb=S)
        hr, hi = _ctx_spectrum(_hyena_filters(S, *filt_args), hy_skip[l], wf_c, S)
        ybt = _hyena_ctx(t_arr, prm, hr, hi, wf_c, wi_c, nb=nbc, S=S)
        res = _merge(xp, ya, ybt, yc, u, gate, wa, wb, wc, wo, next_nw,
                     None if final else mod_parts(l + 1, ctx_rows)[:2], nb=nbc, lb=S, tm=S)
        xp, hp = (res, None) if final else res

        gate = mod_parts(l, lat_rows)[2]
        u, t_arr = _inproj(hs, w_nat, w_t, cos_t, sin_t,
                           layer=l, nb=nbl, lb=L, tm=1024, rope=True, kv_f32=False)
        ya = _lat_win_attn(a_sink[l], u, ca_k, ca_v, l, nb=nbl, lb=L)
        yc = _lat_nbr_attn(u, cc_k, cc_v, l, t2, nb=nbl, lb=L)
        hr, hi = _lat_spectrum(_hyena_filters(L, *filt_args), hy_skip[l], L)
        ybt = _hyena_lat(t_arr, prm, hr, hi, nb=nbl, L=L)
        res = _merge(xs, ya, ybt, yc, u, gate, wa, wb, wc, wo, next_nw,
                     None if final else mod_parts(l + 1, lat_rows)[:2], nb=nbl, lb=L, tm=256)
        xs, hs = (res, None) if final else res

    y_prompt = xp.reshape(nbc, S, D_MODEL)
    y_sample = xs.reshape(nbl, L, D_MODEL)
    return (y_prompt, y_sample, jnp.stack(aks, axis=1), jnp.stack(avs, axis=1),
            jnp.stack(cks, axis=1), jnp.stack(cvs, axis=1))
```

```python
import functools

import numpy as np
import jax
import jax.numpy as jnp
from jax import lax
from jax.experimental import pallas as pl
from jax.experimental.pallas import tpu as pltpu

F32 = jnp.float32
BF16 = jnp.bfloat16
HIGHEST = lax.Precision.HIGHEST

D_MODEL = 2048
HEAD_DIM = 64
A_HEADS = 12
A_KV_HEADS = 4
A_GROUP = A_HEADS // A_KV_HEADS
A_WIDTH = A_HEADS * HEAD_DIM
A_KV_WIDTH = A_KV_HEADS * HEAD_DIM
A_WINDOW = 128
B_WIDTH = 512
HYENA_ORDER = 2
HYENA_BANDS = 16
C_HEADS = 12
C_WIDTH = C_HEADS * HEAD_DIM
GRID_W = 64
NA_ROWS = 8
NA_COLS = 16
ROPE_BASE = 10000.0
EPS = 1e-6
NEG_INF = -1e30
ATTN_SCALE = HEAD_DIM ** -0.5
LOG2E = 1.4426950408889634
QK_SCALE = ATTN_SCALE * LOG2E

LANES = 128
VMEM_LIMIT = 56 * 1024 * 1024

IN_AQ, IN_AK, IN_AV, IN_AG = 0, 768, 1024, 1280
IN_BU, IN_CQ, IN_MG = 2048, 4096, 7168
PROJ_TN = 1024
MG_W = 3 * D_MODEL
U_MG = 0
U_CQ = U_MG + MG_W
U_CG = U_CQ + C_WIDTH
U_AQ = U_CG + C_WIDTH
U_AG = U_AQ + A_WIDTH
U_CK = U_AG + A_WIDTH
U_CV = U_CK + C_WIDTH
U_AK = U_CV + C_WIDTH
U_AV = U_AK + A_KV_WIDTH
NAT_W = U_AV + A_KV_WIDTH
U_KV_W = NAT_W - U_CK
T_W = 4 * B_WIDTH

A_PERM = tuple(2 * A_GROUP * m + A_GROUP * half + j
               for m in range(A_KV_HEADS // 2) for j in range(A_GROUP) for half in range(2))


def _cparams(sem):
    return pltpu.CompilerParams(dimension_semantics=sem, vmem_limit_bytes=VMEM_LIMIT)


def _sigmoid(x):
    return 1.0 / (1.0 + jnp.exp(-x))


def _lane_chunks(off, width):
    assert off % LANES == 0 and width % LANES == 0
    per_blk = PROJ_TN // LANES
    return [divmod(g, per_blk) for g in range(off // LANES, (off + width) // LANES)]


def _adaln_kernel(cond_ref, w_ref, b_ref, o_ref):
    c = cond_ref[...]
    s = (c * _sigmoid(c)).astype(BF16)
    acc = jnp.dot(s, w_ref[0].astype(BF16), preferred_element_type=F32)
    o_ref[0] = acc + b_ref[0]


def _adaln(cond, w_ada, b_ada):
    depth = w_ada.shape[0]
    rows = cond.shape[0]
    tn = 1024
    return pl.pallas_call(
        _adaln_kernel,
        grid=(depth, 3 * D_MODEL // tn),
        in_specs=[
            pl.BlockSpec((rows, D_MODEL), lambda l, j: (0, 0)),
            pl.BlockSpec((1, D_MODEL, tn), lambda l, j: (l, 0, j)),
            pl.BlockSpec((1, 1, tn), lambda l, j: (l, 0, j)),
        ],
        out_specs=pl.BlockSpec((1, rows, tn), lambda l, j: (l, 0, j)),
        out_shape=jax.ShapeDtypeStruct((depth, rows, 3 * D_MODEL), F32),
        compiler_params=_cparams(("arbitrary", "arbitrary")),
        name="adaln",
    )(cond, w_ada, b_ada.reshape(depth, 1, 3 * D_MODEL))


def _modulated_norm(x, nw, sc, sh):
    var = jnp.mean(x * x, axis=-1, keepdims=True)
    return ((x * lax.rsqrt(var + EPS) * nw) * sc + sh).astype(BF16)


def _norm_mod_kernel(x_ref, sc_ref, sh_ref, nw_ref, h_ref):
    h_ref[...] = _modulated_norm(x_ref[...], nw_ref[...], sc_ref[0], sh_ref[0])


def _norm_mod(x2d, sc, sh, norm_w, *, nb, lb):
    per_mod = sc.shape[0] > 1
    tm = 1024 if (lb if per_mod else nb * lb) % 1024 == 0 else lb
    bpb = max(lb // tm, 1)
    mod_spec = pl.BlockSpec((1, 1, D_MODEL), lambda i: ((i // bpb) if per_mod else 0, 0, 0))
    row_spec = pl.BlockSpec((tm, D_MODEL), lambda i: (i, 0))
    return pl.pallas_call(
        _norm_mod_kernel,
        grid=(nb * lb // tm,),
        in_specs=[row_spec, mod_spec, mod_spec, pl.BlockSpec((1, D_MODEL), lambda i: (0, 0))],
        out_specs=row_spec,
        out_shape=jax.ShapeDtypeStruct((nb * lb, D_MODEL), BF16),
        compiler_params=_cparams(("arbitrary",)),
        name="norm_mod",
    )(x2d, sc, sh, norm_w.reshape(1, D_MODEL))


def _inproj_kernel(h_ref, w_ref, wt_ref, cos_ref, sin_ref, u_ref, t_ref, *maybe_kv_ref, rope, tm, lb):
    j = pl.program_id(1)
    tn = PROJ_TN
    n_nat = NAT_W // tn
    kv_blk0 = U_CK // tn

    def rotate(a):
        lane = lax.broadcasted_iota(jnp.int32, (tm, LANES), 1)
        first = (lane % HEAD_DIM) < (HEAD_DIM // 2)
        partner = jnp.where(first, pltpu.roll(a, LANES - HEAD_DIM // 2, 1), pltpu.roll(a, HEAD_DIM // 2, 1))
        return a * cos_ref[...] + partner * sin_ref[...]

    def nat_step(chunk_ops, keep_f32):
        acc = jnp.dot(h_ref[...], w_ref[...], preferred_element_type=F32)
        if keep_f32:
            maybe_kv_ref[0][...] = acc
        if not chunk_ops:
            u_ref[...] = acc.astype(u_ref.dtype)
            return
        for ci in range(tn // LANES):
            a = acc[:, ci * LANES:(ci + 1) * LANES]
            for op in chunk_ops.get(ci, ()):
                a = rotate(a) if op == "rope" else a * QK_SCALE
            u_ref[:, ci * LANES:(ci + 1) * LANES] = a.astype(u_ref.dtype)

    special = {}
    for off, width, ops in ((U_CQ, C_WIDTH, ("scale",)),
                            (U_AQ, A_WIDTH, ("rope", "scale") if rope else ("scale",)),
                            (U_AK, A_KV_WIDTH, ("rope",) if rope else ())):
        for blk, ci in (_lane_chunks(off, width) if ops else ()):
            special.setdefault(blk, {})[ci] = ops
    if maybe_kv_ref:
        assert not rope
        for blk in range(kv_blk0, n_nat):
            special.setdefault(blk, {})

    plain = j < n_nat
    for blk, chunk_ops in special.items():
        plain = plain & (j != blk)
        pl.when(j == blk)(functools.partial(nat_step, chunk_ops, bool(maybe_kv_ref) and blk >= kv_blk0))
    pl.when(plain)(functools.partial(nat_step, None, False))

    @pl.when(j >= n_nat)
    def _():
        acc_t = lax.dot_general(wt_ref[...], h_ref[...], (((1,), (1,)), ((), ())),
                                preferred_element_type=F32)
        if tm <= lb:
            t_ref[0] = acc_t
        else:
            for k in range(tm // lb):
                t_ref[k] = acc_t[:, k * lb:(k + 1) * lb]


def _inproj(h2d, w_nat, w_t, cos_t, sin_t, *, layer, nb, lb, tm, rope, kv_f32):
    rows = nb * lb
    tn = PROJ_TN
    n_nat, n_t = NAT_W // tn, T_W // tn
    kv_blk0, n_kv = U_CK // tn, U_KV_W // tn
    assert U_CK % tn == 0 and kv_blk0 + n_kv == n_nat
    bpb = max(lb // tm, 1)

    if tm <= lb:
        t_spec = pl.BlockSpec((1, tn, tm), lambda i, j: (i // bpb, jnp.clip(j - n_nat, 0, n_t - 1), i % bpb))
    else:
        t_spec = pl.BlockSpec((tm // lb, tn, lb), lambda i, j: (i, jnp.clip(j - n_nat, 0, n_t - 1), 0))

    out_specs = [pl.BlockSpec((tm, tn), lambda i, j: (i, jnp.minimum(j, n_nat - 1))), t_spec]
    out_shape = [jax.ShapeDtypeStruct((rows, NAT_W), BF16), jax.ShapeDtypeStruct((nb, T_W, lb), F32)]
    if kv_f32:
        out_specs.append(pl.BlockSpec((tm, tn), lambda i, j: (i, jnp.clip(j - kv_blk0, 0, n_kv - 1))))
        out_shape.append(jax.ShapeDtypeStruct((rows, U_KV_W), F32))

    kernel = functools.partial(_inproj_kernel, rope=rope, tm=tm, lb=lb)
    return pl.pallas_call(
        kernel,
        grid=(rows // tm, n_nat + n_t),
        in_specs=[
            pl.BlockSpec((tm, D_MODEL), lambda i, j: (i, 0)),
            pl.BlockSpec((None, D_MODEL, tn), lambda i, j: (layer, 0, jnp.minimum(j, n_nat - 1))),
            pl.BlockSpec((None, tn, D_MODEL), lambda i, j: (layer, jnp.maximum(j - n_nat, 0), 0)),
            pl.BlockSpec((tm, LANES), lambda i, j: (i % bpb, 0)),
            pl.BlockSpec((tm, LANES), lambda i, j: (i % bpb, 0)),
        ],
        out_specs=out_specs,
        out_shape=out_shape,
        compiler_params=_cparams(("arbitrary", "arbitrary")),
        name="inproj_rope" if rope else "inproj",
    )(h2d, w_nat, w_t, cos_t, sin_t)


def _softmax_pv(s, v, sink_col):
    m = jnp.max(s, axis=-1, keepdims=True)
    if sink_col is not None:
        m = jnp.maximum(m, sink_col)
    e = jnp.exp2(s - m)
    den = jnp.sum(e, axis=-1, keepdims=True)
    if sink_col is not None:
        den = den + jnp.exp2(sink_col - m)
    o = jnp.dot(e.astype(BF16), v, preferred_element_type=F32)
    return o * (1.0 / den)


def _sink_column(sink_ref, heads, rows_per_head):
    n = len(heads)
    row_head = lax.broadcasted_iota(jnp.int32, (n * rows_per_head, 1), 0) // rows_per_head
    col = jnp.full((n * rows_per_head, 1), sink_ref[heads[-1]], F32)
    for idx in range(n - 1):
        col = jnp.where(row_head == idx, sink_ref[heads[idx]], col)
    return col * LOG2E


def _ctx_attn_kernel(sink_ref, qg_ref, kv0_ref, kv1_ref, ya_ref, yc_ref):
    qg = qg_ref[0]
    kv = jnp.concatenate([kv0_ref[0], kv1_ref[0]], axis=1)
    s_len = qg.shape[0]
    nt = (((1,), (1,)), ((), ()))
    o_cg, o_aq, o_ag = U_CG - U_CQ, U_AQ - U_CQ, U_AG - U_CQ
    o_cv, o_ak, o_av = U_CV - U_CK, U_AK - U_CK, U_AV - U_CK
    head = lambda x, off, h: x[:, off + h * HEAD_DIM:off + (h + 1) * HEAD_DIM]
    a_heads = lambda g: [A_GROUP * g + hh for hh in range(A_GROUP)]

    def scores(job):
        if job < A_KV_HEADS:
            q = jnp.concatenate([head(qg, o_aq, A_PERM.index(h)) for h in a_heads(job)], axis=0)
            k = head(kv, o_ak, job)
        else:
            q = head(qg, 0, job - A_KV_HEADS)
            k = head(kv, 0, job - A_KV_HEADS)
        return lax.dot_general(q, k, nt, preferred_element_type=F32)

    def finish(job, s):
        if job < A_KV_HEADS:
            return _softmax_pv(s, head(kv, o_av, job), _sink_column(sink_ref, a_heads(job), s_len))
        return _softmax_pv(s, head(kv, o_cv, job - A_KV_HEADS), None)

    res = _software_pipeline(A_KV_HEADS + C_HEADS, scores, finish)
    pieces = [None] * A_HEADS
    for g in range(A_KV_HEADS):
        for hh, h in enumerate(a_heads(g)):
            pieces[A_PERM.index(h)] = res[g][hh * s_len:(hh + 1) * s_len]
    ya = jnp.concatenate(pieces, axis=1)
    ag = qg[:, o_ag:o_ag + A_WIDTH].astype(F32)
    ya_ref[0] = (ya * (ag * _sigmoid(ag))).astype(BF16)
    yc = jnp.concatenate(res[A_KV_HEADS:], axis=1)
    cg = qg[:, o_cg:o_cg + C_WIDTH].astype(F32)
    yc_ref[0] = (yc * (cg * _sigmoid(cg))).astype(BF16)


def _ctx_attn(sink, u, *, nb, lb):
    u3 = u.reshape(nb, lb, NAT_W)
    qg_w = U_CK - U_CQ
    assert U_CQ % qg_w == 0 and U_CK % PROJ_TN == 0 and U_KV_W == 2 * PROJ_TN
    return pl.pallas_call(
        _ctx_attn_kernel,
        grid=(nb,),
        in_specs=[
            pl.BlockSpec(memory_space=pltpu.SMEM),
            pl.BlockSpec((1, lb, qg_w), lambda b: (b, 0, U_CQ // qg_w)),
            pl.BlockSpec((1, lb, PROJ_TN), lambda b: (b, 0, U_CK // PROJ_TN)),
            pl.BlockSpec((1, lb, PROJ_TN), lambda b: (b, 0, U_CK // PROJ_TN + 1)),
        ],
        out_specs=[
            pl.BlockSpec((1, lb, A_WIDTH), lambda b: (b, 0, 0)),
            pl.BlockSpec((1, lb, C_WIDTH), lambda b: (b, 0, 0)),
        ],
        out_shape=[
            jax.ShapeDtypeStruct((nb, lb, A_WIDTH), BF16),
            jax.ShapeDtypeStruct((nb, lb, C_WIDTH), BF16),
        ],
        compiler_params=_cparams(("arbitrary",)),
        name="ctx_attn",
    )(sink, u3, u3, u3)


def _own_half(v, half):
    lane = lax.broadcasted_iota(jnp.int32, v.shape, v.ndim - 1) % LANES
    keep = (lane < HEAD_DIM) if half == 0 else (lane >= HEAD_DIM)
    return jnp.where(keep, v, jnp.ones_like(v))


def _stage_request(k_ref, kt_scr, v_ref, v_scr, ck_ref, ckt_scr, cv_ref, cv_scr, chunk):
    n_tok = k_ref.shape[1]

    def body(c, carry):
        start = pl.multiple_of(c * chunk, chunk)
        blk = k_ref[0, pl.ds(start, chunk), :].astype(F32)
        kt_scr[:, pl.ds(start, chunk)] = blk.T.astype(BF16)
        v_blk = v_ref[0, pl.ds(start, chunk), :]
        for half in range(2):
            v_scr[half, pl.ds(start, chunk), :] = _own_half(v_blk, half)
        return carry

    lax.fori_loop(0, n_tok // chunk, body, 0)
    ckt_scr[...] = ck_ref[0, 0].T.astype(BF16)
    ctx_v = cv_ref[0, 0].astype(BF16)
    for half in range(2):
        cv_scr[half] = _own_half(ctx_v, half)


PIPELINE_DEPTH = 2


def _software_pipeline(n, first_stage, second_stage):
    pending = [first_stage(h) for h in range(min(PIPELINE_DEPTH, n))]
    res = []
    for h in range(n):
        if h + PIPELINE_DEPTH < n:
            pending.append(first_stage(h + PIPELINE_DEPTH))
        res.append(second_stage(h, pending.pop(0)))
    return res


def _pair_scores(q_half, kt_loc, bias_fn, kt_ctx):
    s_loc = bias_fn(jnp.dot(q_half, kt_loc, preferred_element_type=F32))
    s_ctx = jnp.dot(q_half, kt_ctx, preferred_element_type=F32)
    return s_loc, s_ctx


def _pair_finish(scores, v_loc, v_ctx, sink_col):
    s_loc, s_ctx = scores
    m = jnp.maximum(jnp.max(s_loc, axis=-1, keepdims=True), jnp.max(s_ctx, axis=-1, keepdims=True))
    if sink_col is not None:
        m = jnp.maximum(m, sink_col)
    e_loc = jnp.exp2(s_loc - m).astype(BF16)
    e_ctx = jnp.exp2(s_ctx - m).astype(BF16)
    o = (jnp.dot(e_loc, v_loc, preferred_element_type=F32)
         + jnp.dot(e_ctx, v_ctx, preferred_element_type=F32))
    den = pltpu.roll(o, HEAD_DIM, 1)
    if sink_col is not None:
        den = den + jnp.exp2(sink_col - m)
    return o * (1.0 / den)


def _lat_win_kernel(sink_ref, q_ref, k_ref, v_ref, ag_ref, ck_ref, cv_ref, o_ref,
                    kt_scr, v_scr, ckt_scr, cv_scr):
    i = pl.program_id(1)
    nblk = pl.num_programs(1)
    blk = A_WINDOW
    span = 3 * blk

    @pl.when(i == 0)
    def _():
        _stage_request(k_ref, kt_scr, v_ref, v_scr, ck_ref, ckt_scr, cv_ref, cv_scr, blk)

    start = pl.multiple_of(jnp.clip(i - 1, 0, nblk - 3) * blk, blk)
    q_pos = i * blk + lax.broadcasted_iota(jnp.int32, (blk, span), 0)
    k_pos = start + lax.broadcasted_iota(jnp.int32, (blk, span), 1)
    valid = jnp.abs(q_pos - k_pos) <= A_WINDOW
    mask = lambda s: jnp.where(valid, s, NEG_INF)

    q = q_ref[0]
    lane = lax.broadcasted_iota(jnp.int32, (blk, LANES), 1)
    low = lane < HEAD_DIM

    def scores(job):
        pair, half = divmod(job, 2)
        rows_m = slice(pair // A_GROUP * LANES, (pair // A_GROUP + 1) * LANES)
        qp = q[:, pair * LANES:(pair + 1) * LANES]
        q_half = jnp.where(low if half == 0 else ~low, qp, jnp.zeros_like(qp))
        return _pair_scores(q_half, kt_scr[rows_m, pl.ds(start, span)], mask, ckt_scr[rows_m, :])

    def finish(job, sc):
        half = job % 2
        rows_m = slice(job // (2 * A_GROUP) * LANES, (job // (2 * A_GROUP) + 1) * LANES)
        return _pair_finish(sc, v_scr[half, pl.ds(start, span), rows_m], cv_scr[half, :, rows_m],
                            sink_ref[A_PERM[job]] * LOG2E)

    res = _software_pipeline(A_HEADS, scores, finish)
    ya = jnp.concatenate([jnp.where(low, res[2 * p], res[2 * p + 1]) for p in range(A_HEADS // 2)], axis=1)
    ag = ag_ref[0].astype(F32)
    o_ref[0] = (ya * (ag * _sigmoid(ag))).astype(BF16)


def _lat_win_attn(sink, u, cache_k, cache_v, layer, *, nb, lb):
    blk = A_WINDOW
    nblk = lb // blk
    assert nblk >= 3
    n_ctx = cache_k.shape[2]
    u3 = u.reshape(nb, lb, NAT_W)
    ctx_spec = pl.BlockSpec((1, 1, n_ctx, A_KV_WIDTH), lambda b, i: (b, layer, 0, 0))
    once = pl.Buffered(1)
    return pl.pallas_call(
        _lat_win_kernel,
        grid=(nb, nblk),
        in_specs=[
            pl.BlockSpec(memory_space=pltpu.SMEM),
            pl.BlockSpec((1, blk, A_WIDTH), lambda b, i: (b, i, U_AQ // A_WIDTH)),
            pl.BlockSpec((1, lb, A_KV_WIDTH), lambda b, i: (b, 0, U_AK // A_KV_WIDTH), pipeline_mode=once),
            pl.BlockSpec((1, lb, A_KV_WIDTH), lambda b, i: (b, 0, U_AV // A_KV_WIDTH), pipeline_mode=once),
            pl.BlockSpec((1, blk, A_WIDTH), lambda b, i: (b, i, U_AG // A_WIDTH)),
            ctx_spec, ctx_spec,
        ],
        out_specs=pl.BlockSpec((1, blk, A_WIDTH), lambda b, i: (b, i, 0)),
        out_shape=jax.ShapeDtypeStruct((nb, lb, A_WIDTH), BF16),
        scratch_shapes=[pltpu.VMEM((A_KV_WIDTH, lb), BF16),
                        pltpu.VMEM((2, lb, A_KV_WIDTH), BF16),
                        pltpu.VMEM((A_KV_WIDTH, n_ctx), BF16),
                        pltpu.VMEM((2, n_ctx, A_KV_WIDTH), BF16)],
        compiler_params=_cparams(("arbitrary", "arbitrary")),
        name="lat_win_attn",
    )(sink, u3, u3, u3, u3, cache_k, cache_v)


NBR_QROWS = 2
NBR_BAND = NA_ROWS + NBR_QROWS


def _lat_nbr_kernel(q_ref, k_ref, v_ref, cg_ref, ck_ref, cv_ref, t2_ref, o_ref,
                    kt_scr, v_scr, ckt_scr, cv_scr, *, grid_rows):
    i = pl.program_id(1)

    @pl.when(i == 0)
    def _():
        _stage_request(k_ref, kt_scr, v_ref, v_scr, ck_ref, ckt_scr, cv_ref, cv_scr, LANES)

    r0 = NBR_QROWS * i
    nq = NBR_QROWS * GRID_W
    nk = NBR_BAND * GRID_W
    bs = jnp.clip(r0 - NA_ROWS // 2, 0, grid_rows - NBR_BAND)
    start = pl.multiple_of(bs * GRID_W, LANES)

    q_row = r0 + lax.broadcasted_iota(jnp.int32, (nq, nk), 0) // GRID_W
    rstart = jnp.clip(q_row - NA_ROWS // 2, 0, grid_rows - NA_ROWS)
    k_row = bs + lax.broadcasted_iota(jnp.int32, (nq, nk), 1) // GRID_W
    valid = (k_row >= rstart) & (k_row < rstart + NA_ROWS)

    q = q_ref[0]
    lane = lax.broadcasted_iota(jnp.int32, (nq, LANES), 1)
    low = lane < HEAD_DIM
    def scores(h):
        m, half = divmod(h, 2)
        rows_m = slice(m * LANES, (m + 1) * LANES)
        bias_rows = []
        for a in range(NBR_QROWS):
            tiles = []
            for p in range(NBR_BAND // 2):
                d = (bs + 2 * p) - (r0 + a) + (NA_ROWS - 1)
                idx = jnp.clip(d, -1, 2 * NA_ROWS - 2) + 1
                tiles.append(t2_ref[h, idx])
            bias_rows.append(jnp.concatenate(tiles, axis=1))
        bias = jnp.concatenate(bias_rows, axis=0)
        qp = q[:, rows_m]
        q_half = jnp.where(low if half == 0 else ~low, qp, jnp.zeros_like(qp))
        return _pair_scores(q_half, kt_scr[rows_m, pl.ds(start, nk)],
                            lambda s: jnp.where(valid, s + bias, NEG_INF), ckt_scr[rows_m, :])

    def finish(h, sc):
        m, half = divmod(h, 2)
        rows_m = slice(m * LANES, (m + 1) * LANES)
        return _pair_finish(sc, v_scr[half, pl.ds(start, nk), rows_m], cv_scr[half, :, rows_m], None)

    res = _software_pipeline(C_HEADS, scores, finish)
    outs = [jnp.where(low, res[2 * m], res[2 * m + 1]) for m in range(C_HEADS // 2)]
    yc = jnp.concatenate(outs, axis=1)
    cg = cg_ref[0].astype(F32)
    o_ref[0] = (yc * (cg * _sigmoid(cg))).astype(BF16)


def _nbr_bias_table(rpb):
    w = np.arange(GRID_W)
    cstart = np.clip(w - NA_COLS // 2, 0, GRID_W - NA_COLS)
    j = np.arange(GRID_W)
    in_win = (j[None, :] >= cstart[:, None]) & (j[None, :] < cstart[:, None] + NA_COLS)
    n_dcol = 2 * NA_COLS - 1
    n_drow = 2 * NA_ROWS - 1
    dcol = j[None, :] - w[:, None] + NA_COLS - 1
    onehot = (np.arange(n_dcol)[:, None, None] == dcol[None]) & in_win[None]
    rp = jnp.pad(rpb.astype(F32) * LOG2E, ((0, 0), (1, 1), (0, 0)))
    r2 = jnp.concatenate([rp[:, :-1], rp[:, 1:]], axis=-1)
    sel = np.zeros((2, n_dcol, GRID_W, 2, GRID_W), np.float32)
    for half in range(2):
        sel[half, :, :, half, :] = onehot
    t = jnp.einsum('hek,kwsj->hewsj', r2, jnp.asarray(sel.reshape(2 * n_dcol, GRID_W, 2, GRID_W)),
                   precision=HIGHEST)
    d = np.arange(2 * NA_ROWS)[:, None] - 1 + np.arange(2)[None, :]
    ok = ((d >= 0) & (d < n_drow))[:, None, :, None] & in_win[None, :, None, :]
    t = jnp.where(jnp.asarray(ok)[None], t, NEG_INF)
    return t.reshape(rpb.shape[0], 2 * NA_ROWS, GRID_W, 2 * GRID_W)


def _lat_nbr_attn(u, cache_k, cache_v, layer, t2, *, nb, lb):
    grid_rows = lb // GRID_W
    assert grid_rows >= NBR_BAND and grid_rows % NBR_QROWS == 0 and NBR_QROWS * GRID_W == LANES
    nq = NBR_QROWS * GRID_W
    n_ctx = cache_k.shape[2]
    u3 = u.reshape(nb, lb, NAT_W)
    ctx_spec = pl.BlockSpec((1, 1, n_ctx, C_WIDTH), lambda b, i: (b, layer, 0, 0))
    kernel = functools.partial(_lat_nbr_kernel, grid_rows=grid_rows)
    once = pl.Buffered(1)
    return pl.pallas_call(
        kernel,
        grid=(nb, grid_rows // NBR_QROWS),
        in_specs=[
            pl.BlockSpec((1, nq, C_WIDTH), lambda b, i: (b, i, U_CQ // C_WIDTH)),
            pl.BlockSpec((1, lb, C_WIDTH), lambda b, i: (b, 0, U_CK // C_WIDTH), pipeline_mode=once),
            pl.BlockSpec((1, lb, C_WIDTH), lambda b, i: (b, 0, U_CV // C_WIDTH), pipeline_mode=once),
            pl.BlockSpec((1, nq, C_WIDTH), lambda b, i: (b, i, U_CG // C_WIDTH)),
            ctx_spec, ctx_spec,
            pl.BlockSpec(t2.shape, lambda b, i: (0, 0, 0, 0), pipeline_mode=once),
        ],
        out_specs=pl.BlockSpec((1, nq, C_WIDTH), lambda b, i: (b, i, 0)),
        out_shape=jax.ShapeDtypeStruct((nb, lb, C_WIDTH), BF16),
        scratch_shapes=[pltpu.VMEM((C_WIDTH, lb), BF16),
                        pltpu.VMEM((2, lb, C_WIDTH), BF16),
                        pltpu.VMEM((C_WIDTH, n_ctx), BF16),
                        pltpu.VMEM((2, n_ctx, C_WIDTH), BF16)],
        compiler_params=_cparams(("arbitrary", "arbitrary")),
        name="lat_nbr_attn",
    )(u3, u3, u3, u3, cache_k, cache_v, t2)


HYENA_EMB_PAD = 40


def _filter_kernel(w1t_ref, b1_ref, fr_ref, w2t_ref, b2_ref, w3t_ref, decay_ref, o_ref, z_scr):
    L = o_ref.shape[1]

    @pl.when(pl.program_id(0) == 0)
    def _():
        t = lax.broadcasted_iota(jnp.int32, (HYENA_EMB_PAD, L), 1).astype(F32) / L
        r = lax.broadcasted_iota(jnp.int32, (HYENA_EMB_PAD, L), 0)
        band = ((r - 1) % HYENA_BANDS + 1).astype(F32) * (2.0 * np.pi)
        ang = t * band
        feats = jnp.where(r == 0, t, jnp.where(r <= HYENA_BANDS, jnp.sin(ang), jnp.cos(ang)))
        z = jnp.dot(w1t_ref[...], feats, precision=HIGHEST, preferred_element_type=F32) + b1_ref[...]
        z = jnp.sin(fr_ref[0] * z)
        z = jnp.dot(w2t_ref[...], z, precision=HIGHEST, preferred_element_type=F32) + b2_ref[...]
        z_scr[...] = jnp.sin(fr_ref[1] * z)

    h = jnp.dot(w3t_ref[...], z_scr[...], precision=HIGHEST, preferred_element_type=F32)
    t_row = lax.broadcasted_iota(jnp.int32, (1, L), 1).astype(F32) / L
    o_ref[...] = h * jnp.exp(-jnp.abs(decay_ref[...]) * t_row)


def _hyena_filters(L, hy_w1, hy_b1, hy_w2, hy_b2, hy_freq, hy_w3, hy_decay):
    hidden = hy_w1.shape[1]
    n_out = hy_w3.shape[1]
    rb = 512
    w1t = jnp.pad(hy_w1.T, ((0, 0), (0, HYENA_EMB_PAD - hy_w1.shape[0])))
    const = lambda a: pl.BlockSpec(a.shape, lambda i: (0,) * a.ndim)
    args = (w1t, hy_b1.reshape(hidden, 1), hy_freq.reshape(2, hidden, 1), hy_w2.T, hy_b2.reshape(hidden, 1))
    out = pl.pallas_call(
        _filter_kernel,
        grid=(n_out // rb,),
        in_specs=[const(a) for a in args] + [
            pl.BlockSpec((rb, hidden), lambda i: (i, 0)),
            pl.BlockSpec((rb, 1), lambda i: (i, 0)),
        ],
        out_specs=pl.BlockSpec((rb, L), lambda i: (i, 0)),
        out_shape=jax.ShapeDtypeStruct((n_out, L), F32),
        scratch_shapes=[pltpu.VMEM((hidden, L), F32)],
        compiler_params=_cparams(("arbitrary",)),
        name="hyena_filter",
    )(*args, hy_w3.T, hy_decay.reshape(n_out, 1))
    return out.reshape(HYENA_ORDER, 2, B_WIDTH, L)


def _short_conv_params(hy_conv_w, hy_conv_b):
    return jnp.concatenate([hy_conv_w.T, hy_conv_b[:, None]], axis=1)


def _dense_dft_consts(S):
    n = 2 * S
    t = np.arange(S)[:, None]
    k = np.arange(n)[None, :]
    ang = -2.0 * np.pi * t * k / n
    cr, ci = np.cos(ang), np.sin(ang)
    wf = np.block([[cr, ci], [-ci, cr]])
    er, ei = cr.T, -ci.T
    wi = np.block([[er, ei], [-ei, er]])
    return jnp.asarray(wf, F32), jnp.asarray(wi, F32)


def _ctx_spec_kernel(f_ref, skip_ref, wf_ref, hr_ref, hi_ref, *, S):
    n = 2 * S
    fwd = f_ref[0, 0]
    bwd = f_ref[0, 1]
    lane = lax.broadcasted_iota(jnp.int32, bwd.shape, 1)
    bwd = jnp.where(lane == 0, 0.0, bwd)
    w = wf_ref[0:S, :]
    ff = jnp.dot(fwd, w, precision=HIGHEST, preferred_element_type=F32)
    fb = jnp.dot(bwd, w, precision=HIGHEST, preferred_element_type=F32)
    skip = skip_ref[0]
    hr_ref[0] = (ff[:, :n] + fb[:, :n] + skip) * (1.0 / n)
    hi_ref[0] = (ff[:, n:] - fb[:, n:]) * (1.0 / n)


def _ctx_spectrum(filt, skip, wf, S):
    n = 2 * S
    kernel = functools.partial(_ctx_spec_kernel, S=S)
    return pl.pallas_call(
        kernel,
        grid=(HYENA_ORDER,),
        in_specs=[
            pl.BlockSpec((1, 2, B_WIDTH, S), lambda o: (o, 0, 0, 0)),
            pl.BlockSpec((1, B_WIDTH, 1), lambda o: (o, 0, 0)),
            pl.BlockSpec(wf.shape, lambda o: (0, 0)),
        ],
        out_specs=[pl.BlockSpec((1, B_WIDTH, n), lambda o: (o, 0, 0))] * 2,
        out_shape=[jax.ShapeDtypeStruct((HYENA_ORDER, B_WIDTH, n), F32)] * 2,
        compiler_params=_cparams(("arbitrary",)),
        name="hyena_ctx_spectrum",
    )(filt, skip.reshape(HYENA_ORDER, B_WIDTH, 1), wf)


def _hyena_ctx_kernel(prm_ref, v_ref, x1_ref, x2_ref, bg_ref, hr_ref, hi_ref, wf_ref, wi_ref, o_ref, *, S):
    nb, cb, _ = v_ref.shape
    half = nb // 2
    n = 2 * S
    lane = lax.broadcasted_iota(jnp.int32, (nb, cb, S), 2)

    def short_conv(u, p):
        prev = jnp.where(lane == 0, 0.0, pltpu.roll(u, 1, 2))
        nxt = jnp.where(lane == S - 1, 0.0, pltpu.roll(u, S - 1, 2))
        return p[:, 3:4] + prev * p[:, 0:1] + u * p[:, 1:2] + nxt * p[:, 2:3]

    def stack(u):
        return jnp.concatenate([u[:half], u[half:]], axis=-1)

    def conv(xs, o):
        spec = jnp.dot(xs.reshape(half * cb, 2 * S).astype(BF16), wf_ref[...],
                       preferred_element_type=F32).reshape(half, cb, 2 * n)
        xr, xi = spec[..., :n], spec[..., n:]
        hr, hi = hr_ref[o], hi_ref[o]
        y = jnp.concatenate([xr * hr - xi * hi, xr * hi + xi * hr], axis=-1)
        return jnp.dot(y.reshape(half * cb, 2 * n).astype(BF16), wi_ref[...],
                       preferred_element_type=F32).reshape(half, cb, 2 * S)

    v = stack(short_conv(v_ref[...], prm_ref[0]))
    x1 = stack(short_conv(x1_ref[...], prm_ref[1]))
    x2 = stack(short_conv(x2_ref[...], prm_ref[2]))
    z = x1 * conv(v, 0)
    y = x2 * conv(z, 1)
    bg = stack(bg_ref[...])
    y = y * (bg * _sigmoid(bg))
    o_ref[0:half] = y[..., :S]
    o_ref[half:nb] = y[..., S:]


def _hyena_ctx(t_arr, prm, hr, hi, wf, wi, *, nb, S):
    cb = 16
    n = 2 * S
    ncb = B_WIDTH // cb
    kernel = functools.partial(_hyena_ctx_kernel, S=S)

    def part(k):
        return pl.BlockSpec((nb, cb, S), lambda c: (0, c + k * ncb, 0))

    return pl.pallas_call(
        kernel,
        grid=(ncb,),
        in_specs=[
            pl.BlockSpec((3, cb, 4), lambda c: (0, c, 0)),
            part(0), part(1), part(2), part(3),
            pl.BlockSpec((HYENA_ORDER, cb, n), lambda c: (0, c, 0)),
            pl.BlockSpec((HYENA_ORDER, cb, n), lambda c: (0, c, 0)),
            pl.BlockSpec(wf.shape, lambda c: (0, 0)),
            pl.BlockSpec(wi.shape, lambda c: (0, 0)),
        ],
        out_specs=pl.BlockSpec((nb, cb, S), lambda c: (0, c, 0)),
        out_shape=jax.ShapeDtypeStruct((nb, B_WIDTH, S), F32),
        compiler_params=_cparams(("arbitrary",)),
        name="hyena_ctx",
    )(prm.reshape(3, B_WIDTH, 4), t_arr, t_arr, t_arr, t_arr, hr, hi, wf.astype(BF16), wi.astype(BF16))


def _two_stage_consts(L, nseq, dtype):
    n2 = LANES
    n = 2 * L
    n1 = n // n2
    h1 = n1 // 2
    k1 = np.arange(n1)[:, None]
    a = -2.0 * np.pi * k1 * np.arange(h1)[None, :] / n1
    w1r, w1i = np.cos(a), np.sin(a)
    w1big = np.block([[w1r, -w1i], [w1i, w1r]])
    vr, vi = w1r.T, -w1i.T
    w1inv = np.block([[vr, -vi], [vi, vr]])
    a = -2.0 * np.pi * k1 * np.arange(n2)[None, :] / n
    twr, twi = np.cos(a), np.sin(a)
    a = -2.0 * np.pi * np.arange(n2)[:, None] * np.arange(n2)[None, :] / n2
    w2r, w2i = np.cos(a), np.sin(a)
    w2big = np.block([[w2r, w2i], [-w2i, w2r]])
    w2conj = np.block([[w2r, -w2i], [w2i, w2r]])
    f = lambda x: jnp.asarray(x, F32)
    m = lambda x: jnp.asarray(x, F32).astype(dtype)
    return dict(
        w1b=jnp.broadcast_to(m(w1big)[None], (nseq,) + w1big.shape),
        w1ib=jnp.broadcast_to(m(w1inv)[None], (nseq,) + w1inv.shape),
        w2=m(w2big), w2c=m(w2conj), twr=f(twr), twi=f(twi), n1=n1, h1=h1)


def _dft_dot(spec, a, b):
    if a.dtype == BF16 or b.dtype == BF16:
        return jnp.einsum(spec, a.astype(BF16), b.astype(BF16), preferred_element_type=F32)
    a_hi, b_hi = a.astype(BF16), b.astype(BF16)
    a_lo = (a - a_hi.astype(F32)).astype(BF16)
    b_lo = (b - b_hi.astype(F32)).astype(BF16)
    mm = lambda x, y: jnp.einsum(spec, x, y, preferred_element_type=F32)
    return mm(a_hi, b_hi) + (mm(a_hi, b_lo) + mm(a_lo, b_hi))


def _cmul(ar, ai, br, bi):
    return ar * br - ai * bi, ar * bi + ai * br


def _stage_fwd(xs, w1b, twr, twi, w2):
    s = xs.shape[0]
    n1 = twr.shape[0]
    a = _dft_dot('smk,skn->smn', w1b, xs)
    p = jnp.concatenate(_cmul(a[:, :n1], a[:, n1:], twr, twi), axis=-1)
    return _dft_dot('mk,kn->mn', p.reshape(s * n1, 2 * LANES), w2)


def _stage_inv(cm, w1ib, twr, twi, w2c, s):
    n1 = twr.shape[0]
    dm = _dft_dot('mk,kn->mn', cm, w2c).reshape(s, n1, 2 * LANES)
    r = jnp.concatenate(_cmul(dm[..., :LANES], dm[..., LANES:], twr, -twi), axis=1)
    return _dft_dot('smk,skn->smn', w1ib, r)


def _lat_spec_kernel(skip_ref, f_ref, w1b_ref, twr_ref, twi_ref, w2_ref, hr_ref, hi_ref, *, n_fft):
    cb, h1 = f_ref.shape[2], f_ref.shape[3]
    n1 = twr_ref.shape[0]
    fwd = f_ref[0, 0]
    bwd = f_ref[0, 1]
    first = (lax.broadcasted_iota(jnp.int32, bwd.shape, 1) == 0) & \
            (lax.broadcasted_iota(jnp.int32, bwd.shape, 2) == 0)
    bwd = jnp.where(first, 0.0, bwd)
    xs = jnp.concatenate([fwd, bwd], axis=0)
    w1_real = w1b_ref[:, :, 0:h1]
    sp = _stage_fwd(xs, w1_real, twr_ref[...], twi_ref[...], w2_ref[...]).reshape(2, cb, n1, 2 * LANES)
    inv = 1.0 / n_fft
    for c in range(cb):
        skip = skip_ref[pl.program_id(0), pl.program_id(1) * cb + c]
        hr_ref[0, c] = (sp[0, c, :, :LANES] + sp[1, c, :, :LANES] + skip) * inv
        hi_ref[0, c] = (sp[0, c, :, LANES:] - sp[1, c, :, LANES:]) * inv


def _lat_spectrum(filt, skip, L):
    cb = 16
    cs = _two_stage_consts(L, 2 * cb, F32)
    n1, h1 = cs["n1"], cs["h1"]
    filt5 = filt.reshape(HYENA_ORDER, 2, B_WIDTH, h1, LANES)
    kernel = functools.partial(_lat_spec_kernel, n_fft=2 * L)
    const = lambda a: pl.BlockSpec(a.shape, lambda o, c: (0,) * a.ndim)
    return pl.pallas_call(
        kernel,
        grid=(HYENA_ORDER, B_WIDTH // cb),
        in_specs=[
            pl.BlockSpec(memory_space=pltpu.SMEM),
            pl.BlockSpec((1, 2, cb, h1, LANES), lambda o, c: (o, 0, c, 0, 0)),
            const(cs["w1b"]), const(cs["twr"]), const(cs["twi"]), const(cs["w2"]),
        ],
        out_specs=[pl.BlockSpec((1, cb, n1, LANES), lambda o, c: (o, c, 0, 0))] * 2,
        out_shape=[jax.ShapeDtypeStruct((HYENA_ORDER, B_WIDTH, n1, LANES), F32)] * 2,
        compiler_params=_cparams(("arbitrary", "arbitrary")),
        name="hyena_lat_spectrum",
    )(skip, filt5, cs["w1b"], cs["twr"], cs["twi"], cs["w2"])


def _hyena_lat_kernel(prm_ref, v_ref, x1_ref, x2_ref, bg_ref, hr_ref, hi_ref,
                      w1b_ref, w1ib_ref, twr_ref, twi_ref, w2_ref, w2c_ref, o_ref,
                      vs_ref, x1s_ref, x2s_ref):
    nb, cb, h1, _ = v_ref.shape
    half = nb // 2
    s = cb * half
    n1 = twr_ref.shape[0]
    c0 = pl.program_id(0) * cb
    row = lax.broadcasted_iota(jnp.int32, (nb, h1, LANES), 1)
    lane = lax.broadcasted_iota(jnp.int32, (nb, h1, LANES), 2)

    def short_conv(u, part, c):
        r = pltpu.roll(u, 1, 2)
        prev = jnp.where(lane == 0, pltpu.roll(r, 1, 1), r)
        prev = jnp.where((lane == 0) & (row == 0), 0.0, prev)
        r = pltpu.roll(u, LANES - 1, 2)
        nxt = jnp.where(lane == LANES - 1, pltpu.roll(r, h1 - 1, 1), r)
        nxt = jnp.where((lane == LANES - 1) & (row == h1 - 1), 0.0, nxt)
        ch = part * B_WIDTH + c0 + c
        return prm_ref[3, ch] + prev * prm_ref[0, ch] + u * prm_ref[1, ch] + nxt * prm_ref[2, ch]

    def stack_into(dst_ref, u, c):
        dst_ref[c * half:(c + 1) * half, 0:h1, :] = u[:half]
        dst_ref[c * half:(c + 1) * half, h1:2 * h1, :] = u[half:]

    for c in range(cb):
        stack_into(vs_ref, short_conv(v_ref[:, c], 0, c), c)
        stack_into(x1s_ref, short_conv(x1_ref[:, c], 1, c), c)
        bg = bg_ref[:, c]
        stack_into(x2s_ref, short_conv(x2_ref[:, c], 2, c) * (bg * _sigmoid(bg)), c)

    twr, twi = twr_ref[...], twi_ref[...]

    def conv(xs, o):
        sp = _stage_fwd(xs, w1b_ref[...], twr, twi, w2_ref[...]).reshape(cb, half, n1, 2 * LANES)
        hr = hr_ref[o][:, None]
        hi = hi_ref[o][:, None]
        cm = jnp.concatenate(_cmul(sp[..., :LANES], sp[..., LANES:], hr, hi), axis=-1)
        return _stage_inv(cm.reshape(s * n1, 2 * LANES), w1ib_ref[...], twr, twi, w2c_ref[...], s)

    z = x1s_ref[...] * conv(vs_ref[...], 0)
    y = x2s_ref[...] * conv(z, 1)
    for c in range(cb):
        o_ref[0:half, c] = y[c * half:(c + 1) * half, 0:h1]
        o_ref[half:nb, c] = y[c * half:(c + 1) * half, h1:2 * h1]


def _hyena_lat(t_arr, prm, hr, hi, *, nb, L):
    cb = 8
    half = nb // 2
    s = cb * half
    cs = _two_stage_consts(L, s, BF16)
    n1, h1 = cs["n1"], cs["h1"]
    ncb = B_WIDTH // cb
    t5 = t_arr.reshape(nb, T_W, h1, LANES)

    def part(k):
        return pl.BlockSpec((nb, cb, h1, LANES), lambda c: (0, c + k * ncb, 0, 0))

    const = lambda a: pl.BlockSpec(a.shape, lambda c: (0,) * a.ndim)
    return pl.pallas_call(
        _hyena_lat_kernel,
        grid=(ncb,),
        in_specs=[
            pl.BlockSpec(memory_space=pltpu.SMEM),
            part(0), part(1), part(2), part(3),
            pl.BlockSpec((HYENA_ORDER, cb, n1, LANES), lambda c: (0, c, 0, 0)),
            pl.BlockSpec((HYENA_ORDER, cb, n1, LANES), lambda c: (0, c, 0, 0)),
            const(cs["w1b"]), const(cs["w1ib"]), const(cs["twr"]), const(cs["twi"]),
            const(cs["w2"]), const(cs["w2c"]),
        ],
        out_specs=pl.BlockSpec((nb, cb, h1, LANES), lambda c: (0, c, 0, 0)),
        out_shape=jax.ShapeDtypeStruct((nb, B_WIDTH, h1, LANES), F32),
        scratch_shapes=[pltpu.VMEM((s, 2 * h1, LANES), F32)] * 3,
        compiler_params=_cparams(("arbitrary",)),
        name="hyena_lat",
    )(prm.T, t5, t5, t5, t5, hr, hi,
      cs["w1b"], cs["w1ib"], cs["twr"], cs["twi"], cs["w2"], cs["w2c"])


def _merge_kernel(x_ref, ya_ref, ybt_ref, yc_ref, mg_ref, gate_ref, wa_ref, wb_ref, wc_ref, wo_ref,
                  nw_ref, *rest, final):
    if final:
        (o_ref,) = rest
    else:
        sc_ref, sh_ref, o_ref, h_ref = rest
    mg = mg_ref[...]
    tm = mg.shape[0]
    if len(ybt_ref.shape) == 4:
        n_rows = tm // LANES
        row0 = (pl.program_id(0) % (ybt_ref.shape[2] // n_rows)) * n_rows
        yb = jnp.concatenate([ybt_ref[0, :, row0 + r, :].T for r in range(n_rows)], axis=0).astype(BF16)
    else:
        yb = ybt_ref[0].T.astype(BF16)
    m = (_sigmoid(mg[:, :D_MODEL].astype(F32)) * jnp.dot(ya_ref[...], wa_ref[...], preferred_element_type=F32)
         + _sigmoid(mg[:, D_MODEL:2 * D_MODEL].astype(F32)) * jnp.dot(yb, wb_ref[...], preferred_element_type=F32)
         + _sigmoid(mg[:, 2 * D_MODEL:].astype(F32)) * jnp.dot(yc_ref[...], wc_ref[...], preferred_element_type=F32))
    out = jnp.dot(m.astype(BF16), wo_ref[...], preferred_element_type=F32)
    xn = x_ref[...] + gate_ref[0] * out
    if final:
        var = jnp.mean(xn * xn, axis=-1, keepdims=True)
        o_ref[...] = xn * lax.rsqrt(var + EPS) * nw_ref[...]
    else:
        o_ref[...] = xn
        h_ref[...] = _modulated_norm(xn, nw_ref[...], sc_ref[0], sh_ref[0])


def _merge(x2d, ya, ybt, yc, u, gate, wa, wb, wc, wo, nw, next_mod, *, nb, lb, tm):
    rows = nb * lb
    bpb = lb // tm
    per_mod = gate.shape[0] > 1
    final = next_mod is None
    kernel = functools.partial(_merge_kernel, final=final)
    mod_spec = pl.BlockSpec((1, 1, D_MODEL), lambda i: ((i // bpb) if per_mod else 0, 0, 0))
    row_spec = pl.BlockSpec((tm, D_MODEL), lambda i: (i, 0))
    const = lambda a: pl.BlockSpec(a.shape, lambda i: (0,) * a.ndim, pipeline_mode=pl.Buffered(1))
    if ybt.ndim == 4:
        sub = 8 * LANES // tm
        assert tm % LANES == 0 and (8 * LANES) % tm == 0 and bpb % sub == 0
        yb_spec = pl.BlockSpec((1, B_WIDTH, 8, LANES), lambda i: (i // bpb, 0, (i % bpb) // sub, 0))
    else:
        yb_spec = pl.BlockSpec((1, B_WIDTH, tm), lambda i: (i // bpb, 0, i % bpb))
    return pl.pallas_call(
        kernel,
        grid=(rows // tm,),
        in_specs=[
            pl.BlockSpec((tm, D_MODEL), lambda i: (i, 0)),
            pl.BlockSpec((tm, A_WIDTH), lambda i: (i, 0)),
            yb_spec,
            pl.BlockSpec((tm, C_WIDTH), lambda i: (i, 0)),
            pl.BlockSpec((tm, MG_W), lambda i: (i, U_MG // MG_W)),
            mod_spec,
            const(wa), const(wb), const(wc), const(wo),
            pl.BlockSpec((1, D_MODEL), lambda i: (0, 0)),
        ] + ([] if final else [mod_spec, mod_spec]),
        out_specs=row_spec if final else [row_spec, row_spec],
        out_shape=(jax.ShapeDtypeStruct((rows, D_MODEL), F32) if final else
                   [jax.ShapeDtypeStruct((rows, D_MODEL), F32), jax.ShapeDtypeStruct((rows, D_MODEL), BF16)]),
        compiler_params=_cparams(("arbitrary",)),
        name="merge_final" if final else "merge",
    )(x2d, ya.reshape(rows, A_WIDTH), ybt, yc.reshape(rows, C_WIDTH), u, gate, wa, wb, wc, wo,
      nw.reshape(1, D_MODEL), *(() if final else next_mod))


def _rope_tables(L):
    t = np.arange(L)
    row = (t // GRID_W).astype(np.float32)
    col = (t % GRID_W).astype(np.float32)
    nf = HEAD_DIM // 4
    inv = np.power(np.float32(ROPE_BASE), -np.arange(nf, dtype=np.float32) / nf).astype(np.float32)
    ang = np.concatenate([row[:, None] * inv[None], col[:, None] * inv[None]], axis=-1)
    cos, sin = np.cos(ang), np.sin(ang)
    reps = LANES // HEAD_DIM
    cos_t = np.tile(np.concatenate([cos, cos], axis=-1), (1, reps))
    sin_t = np.tile(np.concatenate([-sin, sin], axis=-1), (1, reps))
    return jnp.asarray(cos_t, F32), jnp.asarray(sin_t, F32)


def _perm_heads(w, base, axis):
    return [lax.slice_in_dim(w, base + h * HEAD_DIM, base + (h + 1) * HEAD_DIM, axis=axis) for h in A_PERM]


def _prep_w_in_kernel(w_ref, nat_ref, t_ref):
    def put(dst, src, width):
        nat_ref[0, :, dst:dst + width] = w_ref[0, :, src:src + width].astype(BF16)

    put(U_MG, IN_MG, MG_W)
    for dst, k in ((U_CQ, 0), (U_CK, 1), (U_CV, 2), (U_CG, 3)):
        put(dst, IN_CQ + k * C_WIDTH, C_WIDTH)
    for p, h in enumerate(A_PERM):
        put(U_AQ + p * HEAD_DIM, IN_AQ + h * HEAD_DIM, HEAD_DIM)
        put(U_AG + p * HEAD_DIM, IN_AG + h * HEAD_DIM, HEAD_DIM)
    put(U_AK, IN_AK, A_KV_WIDTH)
    put(U_AV, IN_AV, A_KV_WIDTH)
    t_ref[0] = w_ref[0, :, IN_BU:IN_BU + T_W].T.astype(BF16)


def _prep_w_in(w_in):
    depth, _, in_w = w_in.shape
    rb = 128
    return pl.pallas_call(
        _prep_w_in_kernel,
        grid=(depth, D_MODEL // rb),
        in_specs=[pl.BlockSpec((1, rb, in_w), lambda l, i: (l, i, 0))],
        out_specs=[pl.BlockSpec((1, rb, NAT_W), lambda l, i: (l, i, 0)),
                   pl.BlockSpec((1, T_W, rb), lambda l, i: (l, 0, i))],
        out_shape=[jax.ShapeDtypeStruct((depth, D_MODEL, NAT_W), BF16),
                   jax.ShapeDtypeStruct((depth, T_W, D_MODEL), BF16)],
        compiler_params=_cparams(("arbitrary", "arbitrary")),
        name="prep_w_in",
    )(w_in)


def kernel(x_prompt, x_sample, c, cache_a_k, cache_a_v, cache_c_k, cache_c_v, c_ctx, norm_w, w_ada, b_ada, w_in, a_sink, hy_conv_w, hy_conv_b, hy_w1, hy_b1, hy_w2, hy_b2, hy_freq, hy_w3, hy_decay, hy_skip, c_rpb, w_up_a, w_up_b, w_up_c, w_out, final_norm_w):
    nbc, S, _ = x_prompt.shape
    nbl, L, _ = x_sample.shape
    depth = w_in.shape[0]
    n_ctx = cache_a_k.shape[2]
    assert nbc % 2 == 0 and nbl % 2 == 0 and L % (GRID_W * LANES // 2) == 0

    pad = (-(nbl + 1)) % 8
    cond = jnp.concatenate([c, c_ctx[None], jnp.zeros((pad, D_MODEL), F32)], axis=0)
    mod = _adaln(cond, w_ada, b_ada)

    cos_t, sin_t = _rope_tables(L)
    tm_ctx = 1024 if (nbc * S) % 1024 == 0 else S
    zeros_t = jnp.zeros((tm_ctx, LANES), F32)
    wf_c, wi_c = _dense_dft_consts(S)
    ca_k = cache_a_k.reshape(nbl, depth, n_ctx, A_KV_WIDTH)
    ca_v = cache_a_v.reshape(nbl, depth, n_ctx, A_KV_WIDTH)
    cc_k = cache_c_k.reshape(nbl, depth, n_ctx, C_WIDTH)
    cc_v = cache_c_v.reshape(nbl, depth, n_ctx, C_WIDTH)

    w_nat, w_t = _prep_w_in(w_in)

    def mod_parts(l, rows):
        m = mod[l, rows][:, None, :]
        return 1.0 + m[..., D_MODEL:2 * D_MODEL], m[..., :D_MODEL], m[..., 2 * D_MODEL:]

    ctx_rows, lat_rows = slice(nbl, nbl + 1), slice(0, nbl)
    xp = x_prompt.reshape(nbc * S, D_MODEL)
    xs = x_sample.reshape(nbl * L, D_MODEL)
    hp = _norm_mod(xp, *mod_parts(0, ctx_rows)[:2], norm_w[0], nb=nbc, lb=S)
    hs = _norm_mod(xs, *mod_parts(0, lat_rows)[:2], norm_w[0], nb=nbl, lb=L)
    aks, avs, cks, cvs = [], [], [], []
    for l in range(depth):
        final = l == depth - 1
        next_nw = final_norm_w if final else norm_w[l + 1]
        wa = jnp.concatenate(_perm_heads(w_up_a[l], 0, 0), axis=0).astype(BF16)
        wb, wc, wo = (w.astype(BF16) for w in (w_up_b[l], w_up_c[l], w_out[l]))
        prm = _short_conv_params(hy_conv_w[l], hy_conv_b[l])
        filt_args = (hy_w1[l], hy_b1[l], hy_w2[l], hy_b2[l], hy_freq[l], hy_w3[l], hy_decay[l])
        t2 = _nbr_bias_table(c_rpb[l])

        gate = mod_parts(l, ctx_rows)[2]
        u, t_arr, kv = _inproj(hp, w_nat, w_t, zeros_t, zeros_t,
                               layer=l, nb=nbc, lb=S, tm=tm_ctx, rope=False, kv_f32=True)
        cks.append(kv[:, :U_CV - U_CK].reshape(nbc, S, C_HEADS, HEAD_DIM))
        cvs.append(kv[:, U_CV - U_CK:U_AK - U_CK].reshape(nbc, S, C_HEADS, HEAD_DIM))
        aks.append(kv[:, U_AK - U_CK:U_AV - U_CK].reshape(nbc, S, A_KV_HEADS, HEAD_DIM))
        avs.append(kv[:, U_AV - U_CK:].reshape(nbc, S, A_KV_HEADS, HEAD_DIM))
        ya, yc = _ctx_attn(a_sink[l], u, nb=nbc, lb=S)
        hr, hi = _ctx_spectrum(_hyena_filters(S, *filt_args), hy_skip[l], wf_c, S)
        ybt = _hyena_ctx(t_arr, prm, hr, hi, wf_c, wi_c, nb=nbc, S=S)
        res = _merge(xp, ya, ybt, yc, u, gate, wa, wb, wc, wo, next_nw,
                     None if final else mod_parts(l + 1, ctx_rows)[:2], nb=nbc, lb=S, tm=S)
        xp, hp = (res, None) if final else res

        gate = mod_parts(l, lat_rows)[2]
        u, t_arr = _inproj(hs, w_nat, w_t, cos_t, sin_t,
                           layer=l, nb=nbl, lb=L, tm=1024, rope=True, kv_f32=False)
        ya = _lat_win_attn(a_sink[l], u, ca_k, ca_v, l, nb=nbl, lb=L)
        yc = _lat_nbr_attn(u, cc_k, cc_v, l, t2, nb=nbl, lb=L)
        hr, hi = _lat_spectrum(_hyena_filters(L, *filt_args), hy_skip[l], L)
        ybt = _hyena_lat(t_arr, prm, hr, hi, nb=nbl, L=L)
        res = _merge(xs, ya, ybt, yc, u, gate, wa, wb, wc, wo, next_nw,
                     None if final else mod_parts(l + 1, lat_rows)[:2], nb=nbl, lb=L, tm=256)
        xs, hs = (res, None) if final else res

    y_prompt = xp.reshape(nbc, S, D_MODEL)
    y_sample = xs.reshape(nbl, L, D_MODEL)
    return (y_prompt, y_sample, jnp.stack(aks, axis=1), jnp.stack(avs, axis=1),
            jnp.stack(cks, axis=1), jnp.stack(cvs, axis=1))
```

```python
import functools

import numpy as np
import jax
import jax.numpy as jnp
from jax import lax
from jax.experimental import pallas as pl
from jax.experimental.pallas import tpu as pltpu

F32 = jnp.float32
BF16 = jnp.bfloat16
HIGHEST = lax.Precision.HIGHEST

D_MODEL = 2048
HEAD_DIM = 64
A_HEADS = 12
A_KV_HEADS = 4
A_GROUP = A_HEADS // A_KV_HEADS
A_WIDTH = A_HEADS * HEAD_DIM
A_KV_WIDTH = A_KV_HEADS * HEAD_DIM
A_WINDOW = 128
B_WIDTH = 512
HYENA_ORDER = 2
HYENA_BANDS = 16
C_HEADS = 12
C_WIDTH = C_HEADS * HEAD_DIM
GRID_W = 64
NA_ROWS = 8
NA_COLS = 16
ROPE_BASE = 10000.0
EPS = 1e-6
NEG_INF = -1e30
ATTN_SCALE = HEAD_DIM ** -0.5
LOG2E = 1.4426950408889634
QK_SCALE = ATTN_SCALE * LOG2E

LANES = 128
VMEM_LIMIT = 56 * 1024 * 1024

IN_AQ, IN_AK, IN_AV, IN_AG = 0, 768, 1024, 1280
IN_BU, IN_CQ, IN_MG = 2048, 4096, 7168
PROJ_TN = 1024
MG_W = 3 * D_MODEL
U_MG = 0
U_CQ = U_MG + MG_W
U_CG = U_CQ + C_WIDTH
U_AQ = U_CG + C_WIDTH
U_AG = U_AQ + A_WIDTH
U_CK = U_AG + A_WIDTH
U_CV = U_CK + C_WIDTH
U_AK = U_CV + C_WIDTH
U_AV = U_AK + A_KV_WIDTH
NAT_W = U_AV + A_KV_WIDTH
U_KV_W = NAT_W - U_CK
T_W = 4 * B_WIDTH

A_PERM = tuple(2 * A_GROUP * m + A_GROUP * half + j
               for m in range(A_KV_HEADS // 2) for j in range(A_GROUP) for half in range(2))


def _cparams(sem):
    return pltpu.CompilerParams(dimension_semantics=sem, vmem_limit_bytes=VMEM_LIMIT)


def _sigmoid(x):
    return 1.0 / (1.0 + jnp.exp(-x))


def _lane_chunks(off, width):
    assert off % LANES == 0 and width % LANES == 0
    per_blk = PROJ_TN // LANES
    return [divmod(g, per_blk) for g in range(off // LANES, (off + width) // LANES)]


def _adaln_kernel(cond_ref, w_ref, b_ref, o_ref):
    c = cond_ref[...]
    s = (c * _sigmoid(c)).astype(BF16)
    acc = jnp.dot(s, w_ref[0].astype(BF16), preferred_element_type=F32)
    o_ref[0] = acc + b_ref[0]


def _adaln(cond, w_ada, b_ada):
    depth = w_ada.shape[0]
    rows = cond.shape[0]
    tn = 1024
    return pl.pallas_call(
        _adaln_kernel,
        grid=(depth, 3 * D_MODEL // tn),
        in_specs=[
            pl.BlockSpec((rows, D_MODEL), lambda l, j: (0, 0)),
            pl.BlockSpec((1, D_MODEL, tn), lambda l, j: (l, 0, j)),
            pl.BlockSpec((1, 1, tn), lambda l, j: (l, 0, j)),
        ],
        out_specs=pl.BlockSpec((1, rows, tn), lambda l, j: (l, 0, j)),
        out_shape=jax.ShapeDtypeStruct((depth, rows, 3 * D_MODEL), F32),
        compiler_params=_cparams(("arbitrary", "arbitrary")),
        name="adaln",
    )(cond, w_ada, b_ada.reshape(depth, 1, 3 * D_MODEL))


def _modulated_norm(x, nw, sc, sh):
    var = jnp.mean(x * x, axis=-1, keepdims=True)
    return ((x * lax.rsqrt(var + EPS) * nw) * sc + sh).astype(BF16)


def _norm_mod_kernel(x_ref, sc_ref, sh_ref, nw_ref, h_ref):
    h_ref[...] = _modulated_norm(x_ref[...], nw_ref[...], sc_ref[0], sh_ref[0])


def _norm_mod(x2d, sc, sh, norm_w, *, nb, lb):
    per_mod = sc.shape[0] > 1
    tm = 1024 if (lb if per_mod else nb * lb) % 1024 == 0 else lb
    bpb = max(lb // tm, 1)
    mod_spec = pl.BlockSpec((1, 1, D_MODEL), lambda i: ((i // bpb) if per_mod else 0, 0, 0))
    row_spec = pl.BlockSpec((tm, D_MODEL), lambda i: (i, 0))
    return pl.pallas_call(
        _norm_mod_kernel,
        grid=(nb * lb // tm,),
        in_specs=[row_spec, mod_spec, mod_spec, pl.BlockSpec((1, D_MODEL), lambda i: (0, 0))],
        out_specs=row_spec,
        out_shape=jax.ShapeDtypeStruct((nb * lb, D_MODEL), BF16),
        compiler_params=_cparams(("arbitrary",)),
        name="norm_mod",
    )(x2d, sc, sh, norm_w.reshape(1, D_MODEL))


def _inproj_kernel(h_ref, w_ref, wt_ref, cos_ref, sin_ref, u_ref, t_ref, *maybe_kv_ref, rope, tm, lb):
    j = pl.program_id(1)
    tn = PROJ_TN
    n_nat = NAT_W // tn
    kv_blk0 = U_CK // tn

    def rotate(a):
        lane = lax.broadcasted_iota(jnp.int32, (tm, LANES), 1)
        first = (lane % HEAD_DIM) < (HEAD_DIM // 2)
        partner = jnp.where(first, pltpu.roll(a, LANES - HEAD_DIM // 2, 1), pltpu.roll(a, HEAD_DIM // 2, 1))
        return a * cos_ref[...] + partner * sin_ref[...]

    def nat_step(chunk_ops, keep_f32):
        acc = jnp.dot(h_ref[...], w_ref[...], preferred_element_type=F32)
        if keep_f32:
            maybe_kv_ref[0][...] = acc
        if not chunk_ops:
            u_ref[...] = acc.astype(u_ref.dtype)
            return
        for ci in range(tn // LANES):
            a = acc[:, ci * LANES:(ci + 1) * LANES]
            for op in chunk_ops.get(ci, ()):
                a = rotate(a) if op == "rope" else a * QK_SCALE
            u_ref[:, ci * LANES:(ci + 1) * LANES] = a.astype(u_ref.dtype)

    special = {}
    for off, width, ops in ((U_CQ, C_WIDTH, ("scale",)),
                            (U_AQ, A_WIDTH, ("rope", "scale") if rope else ("scale",)),
                            (U_AK, A_KV_WIDTH, ("rope",) if rope else ())):
        for blk, ci in (_lane_chunks(off, width) if ops else ()):
            special.setdefault(blk, {})[ci] = ops
    if maybe_kv_ref:
        assert not rope
        for blk in range(kv_blk0, n_nat):
            special.setdefault(blk, {})

    plain = j < n_nat
    for blk, chunk_ops in special.items():
        plain = plain & (j != blk)
        pl.when(j == blk)(functools.partial(nat_step, chunk_ops, bool(maybe_kv_ref) and blk >= kv_blk0))
    pl.when(plain)(functools.partial(nat_step, None, False))

    @pl.when(j >= n_nat)
    def _():
        acc_t = lax.dot_general(wt_ref[...], h_ref[...], (((1,), (1,)), ((), ())),
                                preferred_element_type=F32)
        if tm <= lb:
            t_ref[0] = acc_t
        else:
            for k in range(tm // lb):
                t_ref[k] = acc_t[:, k * lb:(k + 1) * lb]


def _inproj(h2d, w_nat, w_t, cos_t, sin_t, *, layer, nb, lb, tm, rope, kv_f32):
    rows = nb * lb
    tn = PROJ_TN
    n_nat, n_t = NAT_W // tn, T_W // tn
    kv_blk0, n_kv = U_CK // tn, U_KV_W // tn
    assert U_CK % tn == 0 and kv_blk0 + n_kv == n_nat
    bpb = max(lb // tm, 1)

    if tm <= lb:
        t_spec = pl.BlockSpec((1, tn, tm), lambda i, j: (i // bpb, jnp.clip(j - n_nat, 0, n_t - 1), i % bpb))
    else:
        t_spec = pl.BlockSpec((tm // lb, tn, lb), lambda i, j: (i, jnp.clip(j - n_nat, 0, n_t - 1), 0))

    out_specs = [pl.BlockSpec((tm, tn), lambda i, j: (i, jnp.minimum(j, n_nat - 1))), t_spec]
    out_shape = [jax.ShapeDtypeStruct((rows, NAT_W), BF16), jax.ShapeDtypeStruct((nb, T_W, lb), F32)]
    if kv_f32:
        out_specs.append(pl.BlockSpec((tm, tn), lambda i, j: (i, jnp.clip(j - kv_blk0, 0, n_kv - 1))))
        out_shape.append(jax.ShapeDtypeStruct((rows, U_KV_W), F32))

    kernel = functools.partial(_inproj_kernel, rope=rope, tm=tm, lb=lb)
    return pl.pallas_call(
        kernel,
        grid=(rows // tm, n_nat + n_t),
        in_specs=[
            pl.BlockSpec((tm, D_MODEL), lambda i, j: (i, 0)),
            pl.BlockSpec((None, D_MODEL, tn), lambda i, j: (layer, 0, jnp.minimum(j, n_nat - 1))),
            pl.BlockSpec((None, tn, D_MODEL), lambda i, j: (layer, jnp.maximum(j - n_nat, 0), 0)),
            pl.BlockSpec((tm, LANES), lambda i, j: (i % bpb, 0)),
            pl.BlockSpec((tm, LANES), lambda i, j: (i % bpb, 0)),
        ],
        out_specs=out_specs,
        out_shape=out_shape,
        compiler_params=_cparams(("arbitrary", "arbitrary")),
        name="inproj_rope" if rope else "inproj",
    )(h2d, w_nat, w_t, cos_t, sin_t)


def _softmax_pv(s, v, sink_col):
    m = jnp.max(s, axis=-1, keepdims=True)
    if sink_col is not None:
        m = jnp.maximum(m, sink_col)
    e = jnp.exp2(s - m)
    den = jnp.sum(e, axis=-1, keepdims=True)
    if sink_col is not None:
        den = den + jnp.exp2(sink_col - m)
    o = jnp.dot(e.astype(BF16), v, preferred_element_type=F32)
    return o * (1.0 / den)


def _sink_column(sink_ref, heads, rows_per_head):
    n = len(heads)
    row_head = lax.broadcasted_iota(jnp.int32, (n * rows_per_head, 1), 0) // rows_per_head
    col = jnp.full((n * rows_per_head, 1), sink_ref[heads[-1]], F32)
    for idx in range(n - 1):
        col = jnp.where(row_head == idx, sink_ref[heads[idx]], col)
    return col * LOG2E


def _ctx_attn_kernel(sink_ref, qg_ref, kv0_ref, kv1_ref, ya_ref, yc_ref):
    qg = qg_ref[0]
    kv = jnp.concatenate([kv0_ref[0], kv1_ref[0]], axis=1)
    s_len = qg.shape[0]
    nt = (((1,), (1,)), ((), ()))
    o_cg, o_aq, o_ag = U_CG - U_CQ, U_AQ - U_CQ, U_AG - U_CQ
    o_cv, o_ak, o_av = U_CV - U_CK, U_AK - U_CK, U_AV - U_CK
    head = lambda x, off, h: x[:, off + h * HEAD_DIM:off + (h + 1) * HEAD_DIM]
    a_heads = lambda g: [A_GROUP * g + hh for hh in range(A_GROUP)]

    def scores(job):
        if job < A_KV_HEADS:
            q = jnp.concatenate([head(qg, o_aq, A_PERM.index(h)) for h in a_heads(job)], axis=0)
            k = head(kv, o_ak, job)
        else:
            q = head(qg, 0, job - A_KV_HEADS)
            k = head(kv, 0, job - A_KV_HEADS)
        return lax.dot_general(q, k, nt, preferred_element_type=F32)

    def finish(job, s):
        if job < A_KV_HEADS:
            return _softmax_pv(s, head(kv, o_av, job), _sink_column(sink_ref, a_heads(job), s_len))
        return _softmax_pv(s, head(kv, o_cv, job - A_KV_HEADS), None)

    res = _software_pipeline(A_KV_HEADS + C_HEADS, scores, finish)
    pieces = [None] * A_HEADS
    for g in range(A_KV_HEADS):
        for hh, h in enumerate(a_heads(g)):
            pieces[A_PERM.index(h)] = res[g][hh * s_len:(hh + 1) * s_len]
    ya = jnp.concatenate(pieces, axis=1)
    ag = qg[:, o_ag:o_ag + A_WIDTH].astype(F32)
    ya_ref[0] = (ya * (ag * _sigmoid(ag))).astype(BF16)
    yc = jnp.concatenate(res[A_KV_HEADS:], axis=1)
    cg = qg[:, o_cg:o_cg + C_WIDTH].astype(F32)
    yc_ref[0] = (yc * (cg * _sigmoid(cg))).astype(BF16)


def _ctx_attn(sink, u, *, nb, lb):
    u3 = u.reshape(nb, lb, NAT_W)
    qg_w = U_CK - U_CQ
    assert U_CQ % qg_w == 0 and U_CK % PROJ_TN == 0 and U_KV_W == 2 * PROJ_TN
    return pl.pallas_call(
        _ctx_attn_kernel,
        grid=(nb,),
        in_specs=[
            pl.BlockSpec(memory_space=pltpu.SMEM),
            pl.BlockSpec((1, lb, qg_w), lambda b: (b, 0, U_CQ // qg_w)),
            pl.BlockSpec((1, lb, PROJ_TN), lambda b: (b, 0, U_CK // PROJ_TN)),
            pl.BlockSpec((1, lb, PROJ_TN), lambda b: (b, 0, U_CK // PROJ_TN + 1)),
        ],
        out_specs=[
            pl.BlockSpec((1, lb, A_WIDTH), lambda b: (b, 0, 0)),
            pl.BlockSpec((1, lb, C_WIDTH), lambda b: (b, 0, 0)),
        ],
        out_shape=[
            jax.ShapeDtypeStruct((nb, lb, A_WIDTH), BF16),
            jax.ShapeDtypeStruct((nb, lb, C_WIDTH), BF16),
        ],
        compiler_params=_cparams(("arbitrary",)),
        name="ctx_attn",
    )(sink, u3, u3, u3)


def _own_half(v, half):
    lane = lax.broadcasted_iota(jnp.int32, v.shape, v.ndim - 1) % LANES
    keep = (lane < HEAD_DIM) if half == 0 else (lane >= HEAD_DIM)
    return jnp.where(keep, v, jnp.ones_like(v))


def _stage_request(k_ref, kt_scr, v_ref, v_scr, ck_ref, ckt_scr, cv_ref, cv_scr, chunk):
    n_tok = k_ref.shape[1]

    def body(c, carry):
        start = pl.multiple_of(c * chunk, chunk)
        blk = k_ref[0, pl.ds(start, chunk), :].astype(F32)
        kt_scr[:, pl.ds(start, chunk)] = blk.T.astype(BF16)
        v_blk = v_ref[0, pl.ds(start, chunk), :]
        for half in range(2):
            v_scr[half, pl.ds(start, chunk), :] = _own_half(v_blk, half)
        return carry

    lax.fori_loop(0, n_tok // chunk, body, 0)
    ckt_scr[...] = ck_ref[0, 0].T.astype(BF16)
    ctx_v = cv_ref[0, 0].astype(BF16)
    for half in range(2):
        cv_scr[half] = _own_half(ctx_v, half)


PIPELINE_DEPTH = 2


def _software_pipeline(n, first_stage, second_stage):
    pending = [first_stage(h) for h in range(min(PIPELINE_DEPTH, n))]
    res = []
    for h in range(n):
        if h + PIPELINE_DEPTH < n:
            pending.append(first_stage(h + PIPELINE_DEPTH))
        res.append(second_stage(h, pending.pop(0)))
    return res


def _pair_scores(q_half, kt_loc, bias_fn, kt_ctx):
    s_loc = bias_fn(jnp.dot(q_half, kt_loc, preferred_element_type=F32))
    s_ctx = jnp.dot(q_half, kt_ctx, preferred_element_type=F32)
    return s_loc, s_ctx


def _pair_finish(scores, v_loc, v_ctx, sink_col):
    s_loc, s_ctx = scores
    m = jnp.maximum(jnp.max(s_loc, axis=-1, keepdims=True), jnp.max(s_ctx, axis=-1, keepdims=True))
    if sink_col is not None:
        m = jnp.maximum(m, sink_col)
    e_loc = jnp.exp2(s_loc - m).astype(BF16)
    e_ctx = jnp.exp2(s_ctx - m).astype(BF16)
    o = (jnp.dot(e_loc, v_loc, preferred_element_type=F32)
         + jnp.dot(e_ctx, v_ctx, preferred_element_type=F32))
    den = pltpu.roll(o, HEAD_DIM, 1)
    if sink_col is not None:
        den = den + jnp.exp2(sink_col - m)
    return o * (1.0 / den)


WIN_QBLOCKS = 2


def _lat_win_kernel(sink_ref, q_ref, k_ref, v_ref, ag_ref, ck_ref, cv_ref, o_ref,
                    kt_scr, v_scr, ckt_scr, cv_scr):
    i = pl.program_id(1)
    nblk = pl.num_programs(1) * WIN_QBLOCKS
    blk = A_WINDOW
    span = 3 * blk

    @pl.when(i == 0)
    def _():
        _stage_request(k_ref, kt_scr, v_ref, v_scr, ck_ref, ckt_scr, cv_ref, cv_scr, blk)

    lane = lax.broadcasted_iota(jnp.int32, (blk, LANES), 1)
    low = lane < HEAD_DIM
    starts, masks = [], []
    for sb in range(WIN_QBLOCKS):
        ib = WIN_QBLOCKS * i + sb
        start = pl.multiple_of(jnp.clip(ib - 1, 0, nblk - 3) * blk, blk)
        q_pos = ib * blk + lax.broadcasted_iota(jnp.int32, (blk, span), 0)
        k_pos = start + lax.broadcasted_iota(jnp.int32, (blk, span), 1)
        valid = jnp.abs(q_pos - k_pos) <= A_WINDOW
        starts.append(start)
        masks.append(functools.partial(lambda s, valid: jnp.where(valid, s, NEG_INF), valid=valid))

    def scores(job):
        sb, head = divmod(job, A_HEADS)
        pair, half = divmod(head, 2)
        rows_m = slice(pair // A_GROUP * LANES, (pair // A_GROUP + 1) * LANES)
        qp = q_ref[0, sb * blk:(sb + 1) * blk, pair * LANES:(pair + 1) * LANES]
        q_half = jnp.where(low if half == 0 else ~low, qp, jnp.zeros_like(qp))
        return _pair_scores(q_half, kt_scr[rows_m, pl.ds(starts[sb], span)], masks[sb], ckt_scr[rows_m, :])

    def finish(job, sc):
        sb, head = divmod(job, A_HEADS)
        half = head % 2
        rows_m = slice(head // (2 * A_GROUP) * LANES, (head // (2 * A_GROUP) + 1) * LANES)
        return _pair_finish(sc, v_scr[half, pl.ds(starts[sb], span), rows_m], cv_scr[half, :, rows_m],
                            sink_ref[A_PERM[head]] * LOG2E)

    res = _software_pipeline(WIN_QBLOCKS * A_HEADS, scores, finish)
    for sb in range(WIN_QBLOCKS):
        r = res[sb * A_HEADS:(sb + 1) * A_HEADS]
        ya = jnp.concatenate([jnp.where(low, r[2 * p], r[2 * p + 1]) for p in range(A_HEADS // 2)], axis=1)
        ag = ag_ref[0, sb * blk:(sb + 1) * blk, :].astype(F32)
        o_ref[0, sb * blk:(sb + 1) * blk, :] = (ya * (ag * _sigmoid(ag))).astype(BF16)


def _lat_win_attn(sink, u, cache_k, cache_v, layer, *, nb, lb):
    blk = WIN_QBLOCKS * A_WINDOW
    nblk = lb // blk
    assert lb % blk == 0 and lb // A_WINDOW >= 3
    n_ctx = cache_k.shape[2]
    u3 = u.reshape(nb, lb, NAT_W)
    ctx_spec = pl.BlockSpec((1, 1, n_ctx, A_KV_WIDTH), lambda b, i: (b, layer, 0, 0))
    once = pl.Buffered(1)
    return pl.pallas_call(
        _lat_win_kernel,
        grid=(nb, nblk),
        in_specs=[
            pl.BlockSpec(memory_space=pltpu.SMEM),
            pl.BlockSpec((1, blk, A_WIDTH), lambda b, i: (b, i, U_AQ // A_WIDTH)),
            pl.BlockSpec((1, lb, A_KV_WIDTH), lambda b, i: (b, 0, U_AK // A_KV_WIDTH), pipeline_mode=once),
            pl.BlockSpec((1, lb, A_KV_WIDTH), lambda b, i: (b, 0, U_AV // A_KV_WIDTH), pipeline_mode=once),
            pl.BlockSpec((1, blk, A_WIDTH), lambda b, i: (b, i, U_AG // A_WIDTH)),
            ctx_spec, ctx_spec,
        ],
        out_specs=pl.BlockSpec((1, blk, A_WIDTH), lambda b, i: (b, i, 0)),
        out_shape=jax.ShapeDtypeStruct((nb, lb, A_WIDTH), BF16),
        scratch_shapes=[pltpu.VMEM((A_KV_WIDTH, lb), BF16),
                        pltpu.VMEM((2, lb, A_KV_WIDTH), BF16),
                        pltpu.VMEM((A_KV_WIDTH, n_ctx), BF16),
                        pltpu.VMEM((2, n_ctx, A_KV_WIDTH), BF16)],
        compiler_params=_cparams(("arbitrary", "arbitrary")),
        name="lat_win_attn",
    )(sink, u3, u3, u3, u3, cache_k, cache_v)


NBR_QROWS = 2
NBR_BAND = NA_ROWS + NBR_QROWS
NBR_QBLOCKS = 2


def _lat_nbr_kernel(q_ref, k_ref, v_ref, cg_ref, ck_ref, cv_ref, t2_ref, o_ref,
                    kt_scr, v_scr, ckt_scr, cv_scr, *, grid_rows):
    i = pl.program_id(1)

    @pl.when(i == 0)
    def _():
        _stage_request(k_ref, kt_scr, v_ref, v_scr, ck_ref, ckt_scr, cv_ref, cv_scr, LANES)

    nq = NBR_QROWS * GRID_W
    nk = NBR_BAND * GRID_W
    lane = lax.broadcasted_iota(jnp.int32, (nq, LANES), 1)
    low = lane < HEAD_DIM
    r0s, bss, starts, valids = [], [], [], []
    for sb in range(NBR_QBLOCKS):
        r0 = NBR_QROWS * (NBR_QBLOCKS * i + sb)
        bs = jnp.clip(r0 - NA_ROWS // 2, 0, grid_rows - NBR_BAND)
        q_row = r0 + lax.broadcasted_iota(jnp.int32, (nq, nk), 0) // GRID_W
        rstart = jnp.clip(q_row - NA_ROWS // 2, 0, grid_rows - NA_ROWS)
        k_row = bs + lax.broadcasted_iota(jnp.int32, (nq, nk), 1) // GRID_W
        r0s.append(r0)
        bss.append(bs)
        starts.append(pl.multiple_of(bs * GRID_W, LANES))
        valids.append((k_row >= rstart) & (k_row < rstart + NA_ROWS))

    def scores(job):
        sb, h = divmod(job, C_HEADS)
        m, half = divmod(h, 2)
        rows_m = slice(m * LANES, (m + 1) * LANES)
        bias_rows = []
        for a in range(NBR_QROWS):
            tiles = []
            for p in range(NBR_BAND // 2):
                d = (bss[sb] + 2 * p) - (r0s[sb] + a) + (NA_ROWS - 1)
                idx = jnp.clip(d, -1, 2 * NA_ROWS - 2) + 1
                tiles.append(t2_ref[h, idx])
            bias_rows.append(jnp.concatenate(tiles, axis=1))
        bias = jnp.concatenate(bias_rows, axis=0)
        valid = valids[sb]
        qp = q_ref[0, sb * nq:(sb + 1) * nq, rows_m]
        q_half = jnp.where(low if half == 0 else ~low, qp, jnp.zeros_like(qp))
        return _pair_scores(q_half, kt_scr[rows_m, pl.ds(starts[sb], nk)],
                            lambda s: jnp.where(valid, s + bias, NEG_INF), ckt_scr[rows_m, :])

    def finish(job, sc):
        sb, h = divmod(job, C_HEADS)
        m, half = divmod(h, 2)
        rows_m = slice(m * LANES, (m + 1) * LANES)
        return _pair_finish(sc, v_scr[half, pl.ds(starts[sb], nk), rows_m], cv_scr[half, :, rows_m], None)

    res = _software_pipeline(NBR_QBLOCKS * C_HEADS, scores, finish)
    for sb in range(NBR_QBLOCKS):
        r = res[sb * C_HEADS:(sb + 1) * C_HEADS]
        yc = jnp.concatenate([jnp.where(low, r[2 * m], r[2 * m + 1]) for m in range(C_HEADS // 2)], axis=1)
        cg = cg_ref[0, sb * nq:(sb + 1) * nq, :].astype(F32)
        o_ref[0, sb * nq:(sb + 1) * nq, :] = (yc * (cg * _sigmoid(cg))).astype(BF16)


def _nbr_bias_table(rpb):
    w = np.arange(GRID_W)
    cstart = np.clip(w - NA_COLS // 2, 0, GRID_W - NA_COLS)
    j = np.arange(GRID_W)
    in_win = (j[None, :] >= cstart[:, None]) & (j[None, :] < cstart[:, None] + NA_COLS)
    n_dcol = 2 * NA_COLS - 1
    n_drow = 2 * NA_ROWS - 1
    dcol = j[None, :] - w[:, None] + NA_COLS - 1
    onehot = (np.arange(n_dcol)[:, None, None] == dcol[None]) & in_win[None]
    rp = jnp.pad(rpb.astype(F32) * LOG2E, ((0, 0), (1, 1), (0, 0)))
    r2 = jnp.concatenate([rp[:, :-1], rp[:, 1:]], axis=-1)
    sel = np.zeros((2, n_dcol, GRID_W, 2, GRID_W), np.float32)
    for half in range(2):
        sel[half, :, :, half, :] = onehot
    t = jnp.einsum('hek,kwsj->hewsj', r2, jnp.asarray(sel.reshape(2 * n_dcol, GRID_W, 2, GRID_W)),
                   precision=HIGHEST)
    d = np.arange(2 * NA_ROWS)[:, None] - 1 + np.arange(2)[None, :]
    ok = ((d >= 0) & (d < n_drow))[:, None, :, None] & in_win[None, :, None, :]
    t = jnp.where(jnp.asarray(ok)[None], t, NEG_INF)
    return t.reshape(rpb.shape[0], 2 * NA_ROWS, GRID_W, 2 * GRID_W)


def _lat_nbr_attn(u, cache_k, cache_v, layer, t2, *, nb, lb):
    grid_rows = lb // GRID_W
    step_rows = NBR_QBLOCKS * NBR_QROWS
    assert grid_rows >= NBR_BAND and grid_rows % step_rows == 0 and NBR_QROWS * GRID_W == LANES
    nq = step_rows * GRID_W
    n_ctx = cache_k.shape[2]
    u3 = u.reshape(nb, lb, NAT_W)
    ctx_spec = pl.BlockSpec((1, 1, n_ctx, C_WIDTH), lambda b, i: (b, layer, 0, 0))
    kernel = functools.partial(_lat_nbr_kernel, grid_rows=grid_rows)
    once = pl.Buffered(1)
    return pl.pallas_call(
        kernel,
        grid=(nb, grid_rows // step_rows),
        in_specs=[
            pl.BlockSpec((1, nq, C_WIDTH), lambda b, i: (b, i, U_CQ // C_WIDTH)),
            pl.BlockSpec((1, lb, C_WIDTH), lambda b, i: (b, 0, U_CK // C_WIDTH), pipeline_mode=once),
            pl.BlockSpec((1, lb, C_WIDTH), lambda b, i: (b, 0, U_CV // C_WIDTH), pipeline_mode=once),
            pl.BlockSpec((1, nq, C_WIDTH), lambda b, i: (b, i, U_CG // C_WIDTH)),
            ctx_spec, ctx_spec,
            pl.BlockSpec(t2.shape, lambda b, i: (0, 0, 0, 0), pipeline_mode=once),
        ],
        out_specs=pl.BlockSpec((1, nq, C_WIDTH), lambda b, i: (b, i, 0)),
        out_shape=jax.ShapeDtypeStruct((nb, lb, C_WIDTH), BF16),
        scratch_shapes=[pltpu.VMEM((C_WIDTH, lb), BF16),
                        pltpu.VMEM((2, lb, C_WIDTH), BF16),
                        pltpu.VMEM((C_WIDTH, n_ctx), BF16),
                        pltpu.VMEM((2, n_ctx, C_WIDTH), BF16)],
        compiler_params=_cparams(("arbitrary", "arbitrary")),
        name="lat_nbr_attn",
    )(u3, u3, u3, u3, cache_k, cache_v, t2)


HYENA_EMB_PAD = 40


def _filter_kernel(w1t_ref, b1_ref, fr_ref, w2t_ref, b2_ref, w3t_ref, decay_ref, o_ref, z_scr):
    L = o_ref.shape[1]

    @pl.when(pl.program_id(0) == 0)
    def _():
        t = lax.broadcasted_iota(jnp.int32, (HYENA_EMB_PAD, L), 1).astype(F32) / L
        r = lax.broadcasted_iota(jnp.int32, (HYENA_EMB_PAD, L), 0)
        band = ((r - 1) % HYENA_BANDS + 1).astype(F32) * (2.0 * np.pi)
        ang = t * band
        feats = jnp.where(r == 0, t, jnp.where(r <= HYENA_BANDS, jnp.sin(ang), jnp.cos(ang)))
        z = jnp.dot(w1t_ref[...], feats, precision=HIGHEST, preferred_element_type=F32) + b1_ref[...]
        z = jnp.sin(fr_ref[0] * z)
        z = jnp.dot(w2t_ref[...], z, precision=HIGHEST, preferred_element_type=F32) + b2_ref[...]
        z_scr[...] = jnp.sin(fr_ref[1] * z)

    h = jnp.dot(w3t_ref[...], z_scr[...], precision=HIGHEST, preferred_element_type=F32)
    t_row = lax.broadcasted_iota(jnp.int32, (1, L), 1).astype(F32) / L
    o_ref[...] = h * jnp.exp(-jnp.abs(decay_ref[...]) * t_row)


def _hyena_filters(L, hy_w1, hy_b1, hy_w2, hy_b2, hy_freq, hy_w3, hy_decay):
    hidden = hy_w1.shape[1]
    n_out = hy_w3.shape[1]
    rb = 512
    w1t = jnp.pad(hy_w1.T, ((0, 0), (0, HYENA_EMB_PAD - hy_w1.shape[0])))
    const = lambda a: pl.BlockSpec(a.shape, lambda i: (0,) * a.ndim)
    args = (w1t, hy_b1.reshape(hidden, 1), hy_freq.reshape(2, hidden, 1), hy_w2.T, hy_b2.reshape(hidden, 1))
    out = pl.pallas_call(
        _filter_kernel,
        grid=(n_out // rb,),
        in_specs=[const(a) for a in args] + [
            pl.BlockSpec((rb, hidden), lambda i: (i, 0)),
            pl.BlockSpec((rb, 1), lambda i: (i, 0)),
        ],
        out_specs=pl.BlockSpec((rb, L), lambda i: (i, 0)),
        out_shape=jax.ShapeDtypeStruct((n_out, L), F32),
        scratch_shapes=[pltpu.VMEM((hidden, L), F32)],
        compiler_params=_cparams(("arbitrary",)),
        name="hyena_filter",
    )(*args, hy_w3.T, hy_decay.reshape(n_out, 1))
    return out.reshape(HYENA_ORDER, 2, B_WIDTH, L)


def _short_conv_params(hy_conv_w, hy_conv_b):
    return jnp.concatenate([hy_conv_w.T, hy_conv_b[:, None]], axis=1)


def _dense_dft_consts(S):
    n = 2 * S
    t = np.arange(S)[:, None]
    k = np.arange(n)[None, :]
    ang = -2.0 * np.pi * t * k / n
    cr, ci = np.cos(ang), np.sin(ang)
    wf = np.block([[cr, ci], [-ci, cr]])
    er, ei = cr.T, -ci.T
    wi = np.block([[er, ei], [-ei, er]])
    return jnp.asarray(wf, F32), jnp.asarray(wi, F32)


def _ctx_spec_kernel(f_ref, skip_ref, wf_ref, hr_ref, hi_ref, *, S):
    n = 2 * S
    fwd = f_ref[0, 0]
    bwd = f_ref[0, 1]
    lane = lax.broadcasted_iota(jnp.int32, bwd.shape, 1)
    bwd = jnp.where(lane == 0, 0.0, bwd)
    w = wf_ref[0:S, :]
    ff = jnp.dot(fwd, w, precision=HIGHEST, preferred_element_type=F32)
    fb = jnp.dot(bwd, w, precision=HIGHEST, preferred_element_type=F32)
    skip = skip_ref[0]
    hr_ref[0] = (ff[:, :n] + fb[:, :n] + skip) * (1.0 / n)
    hi_ref[0] = (ff[:, n:] - fb[:, n:]) * (1.0 / n)


def _ctx_spectrum(filt, skip, wf, S):
    n = 2 * S
    kernel = functools.partial(_ctx_spec_kernel, S=S)
    return pl.pallas_call(
        kernel,
        grid=(HYENA_ORDER,),
        in_specs=[
            pl.BlockSpec((1, 2, B_WIDTH, S), lambda o: (o, 0, 0, 0)),
            pl.BlockSpec((1, B_WIDTH, 1), lambda o: (o, 0, 0)),
            pl.BlockSpec(wf.shape, lambda o: (0, 0)),
        ],
        out_specs=[pl.BlockSpec((1, B_WIDTH, n), lambda o: (o, 0, 0))] * 2,
        out_shape=[jax.ShapeDtypeStruct((HYENA_ORDER, B_WIDTH, n), F32)] * 2,
        compiler_params=_cparams(("arbitrary",)),
        name="hyena_ctx_spectrum",
    )(filt, skip.reshape(HYENA_ORDER, B_WIDTH, 1), wf)


def _hyena_ctx_kernel(prm_ref, v_ref, x1_ref, x2_ref, bg_ref, hr_ref, hi_ref, wf_ref, wi_ref, o_ref, *, S):
    nb, cb, _ = v_ref.shape
    half = nb // 2
    n = 2 * S
    lane = lax.broadcasted_iota(jnp.int32, (nb, cb, S), 2)

    def short_conv(u, p):
        prev = jnp.where(lane == 0, 0.0, pltpu.roll(u, 1, 2))
        nxt = jnp.where(lane == S - 1, 0.0, pltpu.roll(u, S - 1, 2))
        return p[:, 3:4] + prev * p[:, 0:1] + u * p[:, 1:2] + nxt * p[:, 2:3]

    def stack(u):
        return jnp.concatenate([u[:half], u[half:]], axis=-1)

    def conv(xs, o):
        spec = jnp.dot(xs.reshape(half * cb, 2 * S).astype(BF16), wf_ref[...],
                       preferred_element_type=F32).reshape(half, cb, 2 * n)
        xr, xi = spec[..., :n], spec[..., n:]
        hr, hi = hr_ref[o], hi_ref[o]
        y = jnp.concatenate([xr * hr - xi * hi, xr * hi + xi * hr], axis=-1)
        return jnp.dot(y.reshape(half * cb, 2 * n).astype(BF16), wi_ref[...],
                       preferred_element_type=F32).reshape(half, cb, 2 * S)

    v = stack(short_conv(v_ref[...], prm_ref[0]))
    x1 = stack(short_conv(x1_ref[...], prm_ref[1]))
    x2 = stack(short_conv(x2_ref[...], prm_ref[2]))
    z = x1 * conv(v, 0)
    y = x2 * conv(z, 1)
    bg = stack(bg_ref[...])
    y = y * (bg * _sigmoid(bg))
    o_ref[0:half] = y[..., :S]
    o_ref[half:nb] = y[..., S:]


def _hyena_ctx(t_arr, prm, hr, hi, wf, wi, *, nb, S):
    cb = 16
    n = 2 * S
    ncb = B_WIDTH // cb
    kernel = functools.partial(_hyena_ctx_kernel, S=S)

    def part(k):
        return pl.BlockSpec((nb, cb, S), lambda c: (0, c + k * ncb, 0))

    return pl.pallas_call(
        kernel,
        grid=(ncb,),
        in_specs=[
            pl.BlockSpec((3, cb, 4), lambda c: (0, c, 0)),
            part(0), part(1), part(2), part(3),
            pl.BlockSpec((HYENA_ORDER, cb, n), lambda c: (0, c, 0)),
            pl.BlockSpec((HYENA_ORDER, cb, n), lambda c: (0, c, 0)),
            pl.BlockSpec(wf.shape, lambda c: (0, 0)),
            pl.BlockSpec(wi.shape, lambda c: (0, 0)),
        ],
        out_specs=pl.BlockSpec((nb, cb, S), lambda c: (0, c, 0)),
        out_shape=jax.ShapeDtypeStruct((nb, B_WIDTH, S), F32),
        compiler_params=_cparams(("arbitrary",)),
        name="hyena_ctx",
    )(prm.reshape(3, B_WIDTH, 4), t_arr, t_arr, t_arr, t_arr, hr, hi, wf.astype(BF16), wi.astype(BF16))


def _two_stage_consts(L, nseq, dtype):
    n2 = LANES
    n = 2 * L
    n1 = n // n2
    h1 = n1 // 2
    k1 = np.arange(n1)[:, None]
    a = -2.0 * np.pi * k1 * np.arange(h1)[None, :] / n1
    w1r, w1i = np.cos(a), np.sin(a)
    w1big = np.block([[w1r, -w1i], [w1i, w1r]])
    vr, vi = w1r.T, -w1i.T
    w1inv = np.block([[vr, -vi], [vi, vr]])
    a = -2.0 * np.pi * k1 * np.arange(n2)[None, :] / n
    twr, twi = np.cos(a), np.sin(a)
    a = -2.0 * np.pi * np.arange(n2)[:, None] * np.arange(n2)[None, :] / n2
    w2r, w2i = np.cos(a), np.sin(a)
    w2big = np.block([[w2r, w2i], [-w2i, w2r]])
    w2conj = np.block([[w2r, -w2i], [w2i, w2r]])
    f = lambda x: jnp.asarray(x, F32)
    m = lambda x: jnp.asarray(x, F32).astype(dtype)
    return dict(
        w1b=jnp.broadcast_to(m(w1big)[None], (nseq,) + w1big.shape),
        w1ib=jnp.broadcast_to(m(w1inv)[None], (nseq,) + w1inv.shape),
        w2=m(w2big), w2c=m(w2conj), twr=f(twr), twi=f(twi), n1=n1, h1=h1)


def _dft_dot(spec, a, b):
    if a.dtype == BF16 or b.dtype == BF16:
        return jnp.einsum(spec, a.astype(BF16), b.astype(BF16), preferred_element_type=F32)
    a_hi, b_hi = a.astype(BF16), b.astype(BF16)
    a_lo = (a - a_hi.astype(F32)).astype(BF16)
    b_lo = (b - b_hi.astype(F32)).astype(BF16)
    mm = lambda x, y: jnp.einsum(spec, x, y, preferred_element_type=F32)
    return mm(a_hi, b_hi) + (mm(a_hi, b_lo) + mm(a_lo, b_hi))


def _cmul(ar, ai, br, bi):
    return ar * br - ai * bi, ar * bi + ai * br


def _stage_fwd(xs, w1b, twr, twi, w2):
    s = xs.shape[0]
    n1 = twr.shape[0]
    a = _dft_dot('smk,skn->smn', w1b, xs)
    p = jnp.concatenate(_cmul(a[:, :n1], a[:, n1:], twr, twi), axis=-1)
    return _dft_dot('mk,kn->mn', p.reshape(s * n1, 2 * LANES), w2)


def _stage_inv(cm, w1ib, twr, twi, w2c, s):
    n1 = twr.shape[0]
    dm = _dft_dot('mk,kn->mn', cm, w2c).reshape(s, n1, 2 * LANES)
    r = jnp.concatenate(_cmul(dm[..., :LANES], dm[..., LANES:], twr, -twi), axis=1)
    return _dft_dot('smk,skn->smn', w1ib, r)


def _lat_spec_kernel(skip_ref, f_ref, w1b_ref, twr_ref, twi_ref, w2_ref, hr_ref, hi_ref, *, n_fft):
    cb, h1 = f_ref.shape[2], f_ref.shape[3]
    n1 = twr_ref.shape[0]
    fwd = f_ref[0, 0]
    bwd = f_ref[0, 1]
    first = (lax.broadcasted_iota(jnp.int32, bwd.shape, 1) == 0) & \
            (lax.broadcasted_iota(jnp.int32, bwd.shape, 2) == 0)
    bwd = jnp.where(first, 0.0, bwd)
    xs = jnp.concatenate([fwd, bwd], axis=0)
    w1_real = w1b_ref[:, :, 0:h1]
    sp = _stage_fwd(xs, w1_real, twr_ref[...], twi_ref[...], w2_ref[...]).reshape(2, cb, n1, 2 * LANES)
    inv = 1.0 / n_fft
    for c in range(cb):
        skip = skip_ref[pl.program_id(0), pl.program_id(1) * cb + c]
        hr_ref[0, c] = (sp[0, c, :, :LANES] + sp[1, c, :, :LANES] + skip) * inv
        hi_ref[0, c] = (sp[0, c, :, LANES:] - sp[1, c, :, LANES:]) * inv


def _lat_spectrum(filt, skip, L):
    cb = 16
    cs = _two_stage_consts(L, 2 * cb, F32)
    n1, h1 = cs["n1"], cs["h1"]
    filt5 = filt.reshape(HYENA_ORDER, 2, B_WIDTH, h1, LANES)
    kernel = functools.partial(_lat_spec_kernel, n_fft=2 * L)
    const = lambda a: pl.BlockSpec(a.shape, lambda o, c: (0,) * a.ndim)
    return pl.pallas_call(
        kernel,
        grid=(HYENA_ORDER, B_WIDTH // cb),
        in_specs=[
            pl.BlockSpec(memory_space=pltpu.SMEM),
            pl.BlockSpec((1, 2, cb, h1, LANES), lambda o, c: (o, 0, c, 0, 0)),
            const(cs["w1b"]), const(cs["twr"]), const(cs["twi"]), const(cs["w2"]),
        ],
        out_specs=[pl.BlockSpec((1, cb, n1, LANES), lambda o, c: (o, c, 0, 0))] * 2,
        out_shape=[jax.ShapeDtypeStruct((HYENA_ORDER, B_WIDTH, n1, LANES), F32)] * 2,
        compiler_params=_cparams(("arbitrary", "arbitrary")),
        name="hyena_lat_spectrum",
    )(skip, filt5, cs["w1b"], cs["twr"], cs["twi"], cs["w2"])


def _hyena_lat_kernel(prm_ref, v_ref, x1_ref, x2_ref, bg_ref, hr_ref, hi_ref,
                      w1b_ref, w1ib_ref, twr_ref, twi_ref, w2_ref, w2c_ref, o_ref,
                      vs_ref, x1s_ref, x2s_ref):
    nb, cb, h1, _ = v_ref.shape
    half = nb // 2
    s = cb * half
    n1 = twr_ref.shape[0]
    c0 = pl.program_id(0) * cb
    row = lax.broadcasted_iota(jnp.int32, (nb, h1, LANES), 1)
    lane = lax.broadcasted_iota(jnp.int32, (nb, h1, LANES), 2)

    def short_conv(u, part, c):
        r = pltpu.roll(u, 1, 2)
        prev = jnp.where(lane == 0, pltpu.roll(r, 1, 1), r)
        prev = jnp.where((lane == 0) & (row == 0), 0.0, prev)
        r = pltpu.roll(u, LANES - 1, 2)
        nxt = jnp.where(lane == LANES - 1, pltpu.roll(r, h1 - 1, 1), r)
        nxt = jnp.where((lane == LANES - 1) & (row == h1 - 1), 0.0, nxt)
        ch = part * B_WIDTH + c0 + c
        return prm_ref[3, ch] + prev * prm_ref[0, ch] + u * prm_ref[1, ch] + nxt * prm_ref[2, ch]

    def stack_into(dst_ref, u, c):
        dst_ref[c * half:(c + 1) * half, 0:h1, :] = u[:half]
        dst_ref[c * half:(c + 1) * half, h1:2 * h1, :] = u[half:]

    for c in range(cb):
        stack_into(vs_ref, short_conv(v_ref[:, c], 0, c), c)
        stack_into(x1s_ref, short_conv(x1_ref[:, c], 1, c), c)
        bg = bg_ref[:, c]
        stack_into(x2s_ref, short_conv(x2_ref[:, c], 2, c) * (bg * _sigmoid(bg)), c)

    twr, twi = twr_ref[...], twi_ref[...]

    def conv(xs, o):
        sp = _stage_fwd(xs, w1b_ref[...], twr, twi, w2_ref[...]).reshape(cb, half, n1, 2 * LANES)
        hr = hr_ref[o][:, None]
        hi = hi_ref[o][:, None]
        cm = jnp.concatenate(_cmul(sp[..., :LANES], sp[..., LANES:], hr, hi), axis=-1)
        return _stage_inv(cm.reshape(s * n1, 2 * LANES), w1ib_ref[...], twr, twi, w2c_ref[...], s)

    z = x1s_ref[...] * conv(vs_ref[...], 0)
    y = x2s_ref[...] * conv(z, 1)
    for c in range(cb):
        o_ref[0:half, c] = y[c * half:(c + 1) * half, 0:h1]
        o_ref[half:nb, c] = y[c * half:(c + 1) * half, h1:2 * h1]


def _hyena_lat(t_arr, prm, hr, hi, *, nb, L):
    cb = 8
    half = nb // 2
    s = cb * half
    cs = _two_stage_consts(L, s, BF16)
    n1, h1 = cs["n1"], cs["h1"]
    ncb = B_WIDTH // cb
    t5 = t_arr.reshape(nb, T_W, h1, LANES)

    def part(k):
        return pl.BlockSpec((nb, cb, h1, LANES), lambda c: (0, c + k * ncb, 0, 0))

    const = lambda a: pl.BlockSpec(a.shape, lambda c: (0,) * a.ndim)
    return pl.pallas_call(
        _hyena_lat_kernel,
        grid=(ncb,),
        in_specs=[
            pl.BlockSpec(memory_space=pltpu.SMEM),
            part(0), part(1), part(2), part(3),
            pl.BlockSpec((HYENA_ORDER, cb, n1, LANES), lambda c: (0, c, 0, 0)),
            pl.BlockSpec((HYENA_ORDER, cb, n1, LANES), lambda c: (0, c, 0, 0)),
            const(cs["w1b"]), const(cs["w1ib"]), const(cs["twr"]), const(cs["twi"]),
            const(cs["w2"]), const(cs["w2c"]),
        ],
        out_specs=pl.BlockSpec((nb, cb, h1, LANES), lambda c: (0, c, 0, 0)),
        out_shape=jax.ShapeDtypeStruct((nb, B_WIDTH, h1, LANES), F32),
        scratch_shapes=[pltpu.VMEM((s, 2 * h1, LANES), F32)] * 3,
        compiler_params=_cparams(("arbitrary",)),
        name="hyena_lat",
    )(prm.T, t5, t5, t5, t5, hr, hi,
      cs["w1b"], cs["w1ib"], cs["twr"], cs["twi"], cs["w2"], cs["w2c"])


def _merge_kernel(x_ref, ya_ref, ybt_ref, yc_ref, mg_ref, gate_ref, wa_ref, wb_ref, wc_ref, wo_ref,
                  nw_ref, *rest, final):
    if final:
        (o_ref,) = rest
    else:
        sc_ref, sh_ref, o_ref, h_ref = rest
    mg = mg_ref[...]
    tm = mg.shape[0]
    if len(ybt_ref.shape) == 4:
        n_rows = tm // LANES
        row0 = (pl.program_id(0) % (ybt_ref.shape[2] // n_rows)) * n_rows
        yb = jnp.concatenate([ybt_ref[0, :, row0 + r, :].T for r in range(n_rows)], axis=0).astype(BF16)
    else:
        yb = ybt_ref[0].T.astype(BF16)
    m = (_sigmoid(mg[:, :D_MODEL].astype(F32)) * jnp.dot(ya_ref[...], wa_ref[...], preferred_element_type=F32)
         + _sigmoid(mg[:, D_MODEL:2 * D_MODEL].astype(F32)) * jnp.dot(yb, wb_ref[...], preferred_element_type=F32)
         + _sigmoid(mg[:, 2 * D_MODEL:].astype(F32)) * jnp.dot(yc_ref[...], wc_ref[...], preferred_element_type=F32))
    out = jnp.dot(m.astype(BF16), wo_ref[...], preferred_element_type=F32)
    xn = x_ref[...] + gate_ref[0] * out
    if final:
        var = jnp.mean(xn * xn, axis=-1, keepdims=True)
        o_ref[...] = xn * lax.rsqrt(var + EPS) * nw_ref[...]
    else:
        o_ref[...] = xn
        h_ref[...] = _modulated_norm(xn, nw_ref[...], sc_ref[0], sh_ref[0])


def _merge(x2d, ya, ybt, yc, u, gate, wa, wb, wc, wo, nw, next_mod, *, nb, lb, tm):
    rows = nb * lb
    bpb = lb // tm
    per_mod = gate.shape[0] > 1
    final = next_mod is None
    kernel = functools.partial(_merge_kernel, final=final)
    mod_spec = pl.BlockSpec((1, 1, D_MODEL), lambda i: ((i // bpb) if per_mod else 0, 0, 0))
    row_spec = pl.BlockSpec((tm, D_MODEL), lambda i: (i, 0))
    const = lambda a: pl.BlockSpec(a.shape, lambda i: (0,) * a.ndim, pipeline_mode=pl.Buffered(1))
    if ybt.ndim == 4:
        sub = 8 * LANES // tm
        assert tm % LANES == 0 and (8 * LANES) % tm == 0 and bpb % sub == 0
        yb_spec = pl.BlockSpec((1, B_WIDTH, 8, LANES), lambda i: (i // bpb, 0, (i % bpb) // sub, 0))
    else:
        yb_spec = pl.BlockSpec((1, B_WIDTH, tm), lambda i: (i // bpb, 0, i % bpb))
    return pl.pallas_call(
        kernel,
        grid=(rows // tm,),
        in_specs=[
            pl.BlockSpec((tm, D_MODEL), lambda i: (i, 0)),
            pl.BlockSpec((tm, A_WIDTH), lambda i: (i, 0)),
            yb_spec,
            pl.BlockSpec((tm, C_WIDTH), lambda i: (i, 0)),
            pl.BlockSpec((tm, MG_W), lambda i: (i, U_MG // MG_W)),
            mod_spec,
            const(wa), const(wb), const(wc), const(wo),
            pl.BlockSpec((1, D_MODEL), lambda i: (0, 0)),
        ] + ([] if final else [mod_spec, mod_spec]),
        out_specs=row_spec if final else [row_spec, row_spec],
        out_shape=(jax.ShapeDtypeStruct((rows, D_MODEL), F32) if final else
                   [jax.ShapeDtypeStruct((rows, D_MODEL), F32), jax.ShapeDtypeStruct((rows, D_MODEL), BF16)]),
        compiler_params=_cparams(("arbitrary",)),
        name="merge_final" if final else "merge",
    )(x2d, ya.reshape(rows, A_WIDTH), ybt, yc.reshape(rows, C_WIDTH), u, gate, wa, wb, wc, wo,
      nw.reshape(1, D_MODEL), *(() if final else next_mod))


def _rope_tables(L):
    t = np.arange(L)
    row = (t // GRID_W).astype(np.float32)
    col = (t % GRID_W).astype(np.float32)
    nf = HEAD_DIM // 4
    inv = np.power(np.float32(ROPE_BASE), -np.arange(nf, dtype=np.float32) / nf).astype(np.float32)
    ang = np.concatenate([row[:, None] * inv[None], col[:, None] * inv[None]], axis=-1)
    cos, sin = np.cos(ang), np.sin(ang)
    reps = LANES // HEAD_DIM
    cos_t = np.tile(np.concatenate([cos, cos], axis=-1), (1, reps))
    sin_t = np.tile(np.concatenate([-sin, sin], axis=-1), (1, reps))
    return jnp.asarray(cos_t, F32), jnp.asarray(sin_t, F32)


def _perm_heads(w, base, axis):
    return [lax.slice_in_dim(w, base + h * HEAD_DIM, base + (h + 1) * HEAD_DIM, axis=axis) for h in A_PERM]


def _prep_w_in_kernel(w_ref, nat_ref, t_ref):
    def put(dst, src, width):
        nat_ref[0, :, dst:dst + width] = w_ref[0, :, src:src + width].astype(BF16)

    put(U_MG, IN_MG, MG_W)
    for dst, k in ((U_CQ, 0), (U_CK, 1), (U_CV, 2), (U_CG, 3)):
        put(dst, IN_CQ + k * C_WIDTH, C_WIDTH)
    for p, h in enumerate(A_PERM):
        put(U_AQ + p * HEAD_DIM, IN_AQ + h * HEAD_DIM, HEAD_DIM)
        put(U_AG + p * HEAD_DIM, IN_AG + h * HEAD_DIM, HEAD_DIM)
    put(U_AK, IN_AK, A_KV_WIDTH)
    put(U_AV, IN_AV, A_KV_WIDTH)
    t_ref[0] = w_ref[0, :, IN_BU:IN_BU + T_W].T.astype(BF16)


def _prep_w_in(w_in):
    depth, _, in_w = w_in.shape
    rb = 128
    return pl.pallas_call(
        _prep_w_in_kernel,
        grid=(depth, D_MODEL // rb),
        in_specs=[pl.BlockSpec((1, rb, in_w), lambda l, i: (l, i, 0))],
        out_specs=[pl.BlockSpec((1, rb, NAT_W), lambda l, i: (l, i, 0)),
                   pl.BlockSpec((1, T_W, rb), lambda l, i: (l, 0, i))],
        out_shape=[jax.ShapeDtypeStruct((depth, D_MODEL, NAT_W), BF16),
                   jax.ShapeDtypeStruct((depth, T_W, D_MODEL), BF16)],
        compiler_params=_cparams(("arbitrary", "arbitrary")),
        name="prep_w_in",
    )(w_in)


def kernel(x_prompt, x_sample, c, cache_a_k, cache_a_v, cache_c_k, cache_c_v, c_ctx, norm_w, w_ada, b_ada, w_in, a_sink, hy_conv_w, hy_conv_b, hy_w1, hy_b1, hy_w2, hy_b2, hy_freq, hy_w3, hy_decay, hy_skip, c_rpb, w_up_a, w_up_b, w_up_c, w_out, final_norm_w):
    nbc, S, _ = x_prompt.shape
    nbl, L, _ = x_sample.shape
    depth = w_in.shape[0]
    n_ctx = cache_a_k.shape[2]
    assert nbc % 2 == 0 and nbl % 2 == 0 and L % (GRID_W * LANES // 2) == 0

    pad = (-(nbl + 1)) % 8
    cond = jnp.concatenate([c, c_ctx[None], jnp.zeros((pad, D_MODEL), F32)], axis=0)
    mod = _adaln(cond, w_ada, b_ada)

    cos_t, sin_t = _rope_tables(L)
    tm_ctx = 1024 if (nbc * S) % 1024 == 0 else S
    zeros_t = jnp.zeros((tm_ctx, LANES), F32)
    wf_c, wi_c = _dense_dft_consts(S)
    ca_k = cache_a_k.reshape(nbl, depth, n_ctx, A_KV_WIDTH)
    ca_v = cache_a_v.reshape(nbl, depth, n_ctx, A_KV_WIDTH)
    cc_k = cache_c_k.reshape(nbl, depth, n_ctx, C_WIDTH)
    cc_v = cache_c_v.reshape(nbl, depth, n_ctx, C_WIDTH)

    w_nat, w_t = _prep_w_in(w_in)

    def mod_parts(l, rows):
        m = mod[l, rows][:, None, :]
        return 1.0 + m[..., D_MODEL:2 * D_MODEL], m[..., :D_MODEL], m[..., 2 * D_MODEL:]

    ctx_rows, lat_rows = slice(nbl, nbl + 1), slice(0, nbl)
    xp = x_prompt.reshape(nbc * S, D_MODEL)
    xs = x_sample.reshape(nbl * L, D_MODEL)
    hp = _norm_mod(xp, *mod_parts(0, ctx_rows)[:2], norm_w[0], nb=nbc, lb=S)
    hs = _norm_mod(xs, *mod_parts(0, lat_rows)[:2], norm_w[0], nb=nbl, lb=L)
    aks, avs, cks, cvs = [], [], [], []
    for l in range(depth):
        final = l == depth - 1
        next_nw = final_norm_w if final else norm_w[l + 1]
        wa = jnp.concatenate(_perm_heads(w_up_a[l], 0, 0), axis=0).astype(BF16)
        wb, wc, wo = (w.astype(BF16) for w in (w_up_b[l], w_up_c[l], w_out[l]))
        prm = _short_conv_params(hy_conv_w[l], hy_conv_b[l])
        filt_args = (hy_w1[l], hy_b1[l], hy_w2[l], hy_b2[l], hy_freq[l], hy_w3[l], hy_decay[l])
        t2 = _nbr_bias_table(c_rpb[l])

        gate = mod_parts(l, ctx_rows)[2]
        u, t_arr, kv = _inproj(hp, w_nat, w_t, zeros_t, zeros_t,
                               layer=l, nb=nbc, lb=S, tm=tm_ctx, rope=False, kv_f32=True)
        cks.append(kv[:, :U_CV - U_CK].reshape(nbc, S, C_HEADS, HEAD_DIM))
        cvs.append(kv[:, U_CV - U_CK:U_AK - U_CK].reshape(nbc, S, C_HEADS, HEAD_DIM))
        aks.append(kv[:, U_AK - U_CK:U_AV - U_CK].reshape(nbc, S, A_KV_HEADS, HEAD_DIM))
        avs.append(kv[:, U_AV - U_CK:].reshape(nbc, S, A_KV_HEADS, HEAD_DIM))
        ya, yc = _ctx_attn(a_sink[l], u, nb=nbc, lb=S)
        hr, hi = _ctx_spectrum(_hyena_filters(S, *filt_args), hy_skip[l], wf_c, S)
        ybt = _hyena_ctx(t_arr, prm, hr, hi, wf_c, wi_c, nb=nbc, S=S)
        res = _merge(xp, ya, ybt, yc, u, gate, wa, wb, wc, wo, next_nw,
                     None if final else mod_parts(l + 1, ctx_rows)[:2], nb=nbc, lb=S, tm=S)
        xp, hp = (res, None) if final else res

        gate = mod_parts(l, lat_rows)[2]
        u, t_arr = _inproj(hs, w_nat, w_t, cos_t, sin_t,
                           layer=l, nb=nbl, lb=L, tm=1024, rope=True, kv_f32=False)
        ya = _lat_win_attn(a_sink[l], u, ca_k, ca_v, l, nb=nbl, lb=L)
        yc = _lat_nbr_attn(u, cc_k, cc_v, l, t2, nb=nbl, lb=L)
        hr, hi = _lat_spectrum(_hyena_filters(L, *filt_args), hy_skip[l], L)
        ybt = _hyena_lat(t_arr, prm, hr, hi, nb=nbl, L=L)
        res = _merge(xs, ya, ybt, yc, u, gate, wa, wb, wc, wo, next_nw,
                     None if final else mod_parts(l + 1, lat_rows)[:2], nb=nbl, lb=L, tm=256)
        xs, hs = (res, None) if final else res

    y_prompt = xp.reshape(nbc, S, D_MODEL)
    y_sample = xs.reshape(nbl, L, D_MODEL)
    return (y_prompt, y_sample, jnp.stack(aks, axis=1), jnp.stack(avs, axis=1),
            jnp.stack(cks, axis=1), jnp.stack(cvs, axis=1))
```

```python
import functools

import numpy as np
import jax
import jax.numpy as jnp
from jax import lax
from jax.experimental import pallas as pl
from jax.experimental.pallas import tpu as pltpu

F32 = jnp.float32
BF16 = jnp.bfloat16
HIGHEST = lax.Precision.HIGHEST

D_MODEL = 2048
HEAD_DIM = 64
A_HEADS = 12
A_KV_HEADS = 4
A_GROUP = A_HEADS // A_KV_HEADS
A_WIDTH = A_HEADS * HEAD_DIM
A_KV_WIDTH = A_KV_HEADS * HEAD_DIM
A_WINDOW = 128
B_WIDTH = 512
HYENA_ORDER = 2
HYENA_BANDS = 16
C_HEADS = 12
C_WIDTH = C_HEADS * HEAD_DIM
GRID_W = 64
NA_ROWS = 8
NA_COLS = 16
ROPE_BASE = 10000.0
EPS = 1e-6
NEG_INF = -1e30
ATTN_SCALE = HEAD_DIM ** -0.5
LOG2E = 1.4426950408889634
QK_SCALE = ATTN_SCALE * LOG2E

LANES = 128
VMEM_LIMIT = 56 * 1024 * 1024

IN_AQ, IN_AK, IN_AV, IN_AG = 0, 768, 1024, 1280
IN_BU, IN_CQ, IN_MG = 2048, 4096, 7168
PROJ_TN = 1024
MG_W = 3 * D_MODEL
U_MG = 0
U_CQ = U_MG + MG_W
U_CG = U_CQ + C_WIDTH
U_AQ = U_CG + C_WIDTH
U_AG = U_AQ + A_WIDTH
U_CK = U_AG + A_WIDTH
U_CV = U_CK + C_WIDTH
U_AK = U_CV + C_WIDTH
U_AV = U_AK + A_KV_WIDTH
NAT_W = U_AV + A_KV_WIDTH
U_KV_W = NAT_W - U_CK
T_W = 4 * B_WIDTH

A_PERM = tuple(2 * A_GROUP * m + A_GROUP * half + j
               for m in range(A_KV_HEADS // 2) for j in range(A_GROUP) for half in range(2))


def _cparams(sem):
    return pltpu.CompilerParams(dimension_semantics=sem, vmem_limit_bytes=VMEM_LIMIT)


def _sigmoid(x):
    return 1.0 / (1.0 + jnp.exp(-x))


def _lane_chunks(off, width):
    assert off % LANES == 0 and width % LANES == 0
    per_blk = PROJ_TN // LANES
    return [divmod(g, per_blk) for g in range(off // LANES, (off + width) // LANES)]


def _adaln_kernel(cond_ref, w_ref, b_ref, o_ref):
    c = cond_ref[...]
    s = (c * _sigmoid(c)).astype(BF16)
    acc = jnp.dot(s, w_ref[0].astype(BF16), preferred_element_type=F32)
    o_ref[0] = acc + b_ref[0]


def _adaln(cond, w_ada, b_ada):
    depth = w_ada.shape[0]
    rows = cond.shape[0]
    tn = 1024
    return pl.pallas_call(
        _adaln_kernel,
        grid=(depth, 3 * D_MODEL // tn),
        in_specs=[
            pl.BlockSpec((rows, D_MODEL), lambda l, j: (0, 0)),
            pl.BlockSpec((1, D_MODEL, tn), lambda l, j: (l, 0, j)),
            pl.BlockSpec((1, 1, tn), lambda l, j: (l, 0, j)),
        ],
        out_specs=pl.BlockSpec((1, rows, tn), lambda l, j: (l, 0, j)),
        out_shape=jax.ShapeDtypeStruct((depth, rows, 3 * D_MODEL), F32),
        compiler_params=_cparams(("arbitrary", "arbitrary")),
        name="adaln",
    )(cond, w_ada, b_ada.reshape(depth, 1, 3 * D_MODEL))


def _modulated_norm(x, nw, sc, sh):
    var = jnp.mean(x * x, axis=-1, keepdims=True)
    return ((x * lax.rsqrt(var + EPS) * nw) * sc + sh).astype(BF16)


def _norm_mod_kernel(x_ref, sc_ref, sh_ref, nw_ref, h_ref):
    h_ref[...] = _modulated_norm(x_ref[...], nw_ref[...], sc_ref[0], sh_ref[0])


def _norm_mod(x2d, sc, sh, norm_w, *, nb, lb):
    per_mod = sc.shape[0] > 1
    tm = 1024 if (lb if per_mod else nb * lb) % 1024 == 0 else lb
    bpb = max(lb // tm, 1)
    mod_spec = pl.BlockSpec((1, 1, D_MODEL), lambda i: ((i // bpb) if per_mod else 0, 0, 0))
    row_spec = pl.BlockSpec((tm, D_MODEL), lambda i: (i, 0))
    return pl.pallas_call(
        _norm_mod_kernel,
        grid=(nb * lb // tm,),
        in_specs=[row_spec, mod_spec, mod_spec, pl.BlockSpec((1, D_MODEL), lambda i: (0, 0))],
        out_specs=row_spec,
        out_shape=jax.ShapeDtypeStruct((nb * lb, D_MODEL), BF16),
        compiler_params=_cparams(("arbitrary",)),
        name="norm_mod",
    )(x2d, sc, sh, norm_w.reshape(1, D_MODEL))


def _inproj_kernel(h_ref, w_ref, wt_ref, cos_ref, sin_ref, u_ref, t_ref, *maybe_kv_ref, rope, tm, lb):
    j = pl.program_id(1)
    tn = PROJ_TN
    n_nat = NAT_W // tn
    kv_blk0 = U_CK // tn

    def rotate(a):
        lane = lax.broadcasted_iota(jnp.int32, (tm, LANES), 1)
        first = (lane % HEAD_DIM) < (HEAD_DIM // 2)
        partner = jnp.where(first, pltpu.roll(a, LANES - HEAD_DIM // 2, 1), pltpu.roll(a, HEAD_DIM // 2, 1))
        return a * cos_ref[...] + partner * sin_ref[...]

    def nat_step(chunk_ops, keep_f32):
        acc = jnp.dot(h_ref[...], w_ref[...], preferred_element_type=F32)
        if keep_f32:
            maybe_kv_ref[0][...] = acc
        if not chunk_ops:
            u_ref[...] = acc.astype(u_ref.dtype)
            return
        for ci in range(tn // LANES):
            a = acc[:, ci * LANES:(ci + 1) * LANES]
            for op in chunk_ops.get(ci, ()):
                a = rotate(a) if op == "rope" else a * QK_SCALE
            u_ref[:, ci * LANES:(ci + 1) * LANES] = a.astype(u_ref.dtype)

    special = {}
    for off, width, ops in ((U_CQ, C_WIDTH, ("scale",)),
                            (U_AQ, A_WIDTH, ("rope", "scale") if rope else ("scale",)),
                            (U_AK, A_KV_WIDTH, ("rope",) if rope else ())):
        for blk, ci in (_lane_chunks(off, width) if ops else ()):
            special.setdefault(blk, {})[ci] = ops
    if maybe_kv_ref:
        assert not rope
        for blk in range(kv_blk0, n_nat):
            special.setdefault(blk, {})

    plain = j < n_nat
    for blk, chunk_ops in special.items():
        plain = plain & (j != blk)
        pl.when(j == blk)(functools.partial(nat_step, chunk_ops, bool(maybe_kv_ref) and blk >= kv_blk0))
    pl.when(plain)(functools.partial(nat_step, None, False))

    @pl.when(j >= n_nat)
    def _():
        acc_t = lax.dot_general(wt_ref[...], h_ref[...], (((1,), (1,)), ((), ())),
                                preferred_element_type=F32)
        if tm <= lb:
            t_ref[0] = acc_t
        else:
            for k in range(tm // lb):
                t_ref[k] = acc_t[:, k * lb:(k + 1) * lb]


def _inproj(h2d, w_nat, w_t, cos_t, sin_t, *, layer, nb, lb, tm, rope, kv_f32):
    rows = nb * lb
    tn = PROJ_TN
    n_nat, n_t = NAT_W // tn, T_W // tn
    kv_blk0, n_kv = U_CK // tn, U_KV_W // tn
    assert U_CK % tn == 0 and kv_blk0 + n_kv == n_nat
    bpb = max(lb // tm, 1)

    if tm <= lb:
        t_spec = pl.BlockSpec((1, tn, tm), lambda i, j: (i // bpb, jnp.clip(j - n_nat, 0, n_t - 1), i % bpb))
    else:
        t_spec = pl.BlockSpec((tm // lb, tn, lb), lambda i, j: (i, jnp.clip(j - n_nat, 0, n_t - 1), 0))

    out_specs = [pl.BlockSpec((tm, tn), lambda i, j: (i, jnp.minimum(j, n_nat - 1))), t_spec]
    out_shape = [jax.ShapeDtypeStruct((rows, NAT_W), BF16), jax.ShapeDtypeStruct((nb, T_W, lb), F32)]
    if kv_f32:
        out_specs.append(pl.BlockSpec((tm, tn), lambda i, j: (i, jnp.clip(j - kv_blk0, 0, n_kv - 1))))
        out_shape.append(jax.ShapeDtypeStruct((rows, U_KV_W), F32))

    kernel = functools.partial(_inproj_kernel, rope=rope, tm=tm, lb=lb)
    return pl.pallas_call(
        kernel,
        grid=(rows // tm, n_nat + n_t),
        in_specs=[
            pl.BlockSpec((tm, D_MODEL), lambda i, j: (i, 0)),
            pl.BlockSpec((None, D_MODEL, tn), lambda i, j: (layer, 0, jnp.minimum(j, n_nat - 1))),
            pl.BlockSpec((None, tn, D_MODEL), lambda i, j: (layer, jnp.maximum(j - n_nat, 0), 0)),
            pl.BlockSpec((tm, LANES), lambda i, j: (i % bpb, 0)),
            pl.BlockSpec((tm, LANES), lambda i, j: (i % bpb, 0)),
        ],
        out_specs=out_specs,
        out_shape=out_shape,
        compiler_params=_cparams(("arbitrary", "arbitrary")),
        name="inproj_rope" if rope else "inproj",
    )(h2d, w_nat, w_t, cos_t, sin_t)


def _softmax_pv(s, v, sink_col):
    m = jnp.max(s, axis=-1, keepdims=True)
    if sink_col is not None:
        m = jnp.maximum(m, sink_col)
    e = jnp.exp2(s - m)
    den = jnp.sum(e, axis=-1, keepdims=True)
    if sink_col is not None:
        den = den + jnp.exp2(sink_col - m)
    o = jnp.dot(e.astype(BF16), v, preferred_element_type=F32)
    return o * (1.0 / den)


def _sink_column(sink_ref, heads, rows_per_head):
    n = len(heads)
    row_head = lax.broadcasted_iota(jnp.int32, (n * rows_per_head, 1), 0) // rows_per_head
    col = jnp.full((n * rows_per_head, 1), sink_ref[heads[-1]], F32)
    for idx in range(n - 1):
        col = jnp.where(row_head == idx, sink_ref[heads[idx]], col)
    return col * LOG2E


def _ctx_attn_kernel(sink_ref, qg_ref, kv0_ref, kv1_ref, ya_ref, yc_ref):
    qg = qg_ref[0]
    kv = jnp.concatenate([kv0_ref[0], kv1_ref[0]], axis=1)
    s_len = qg.shape[0]
    nt = (((1,), (1,)), ((), ()))
    o_cg, o_aq, o_ag = U_CG - U_CQ, U_AQ - U_CQ, U_AG - U_CQ
    o_cv, o_ak, o_av = U_CV - U_CK, U_AK - U_CK, U_AV - U_CK
    head = lambda x, off, h: x[:, off + h * HEAD_DIM:off + (h + 1) * HEAD_DIM]
    a_heads = lambda g: [A_GROUP * g + hh for hh in range(A_GROUP)]

    def scores(job):
        if job < A_KV_HEADS:
            q = jnp.concatenate([head(qg, o_aq, A_PERM.index(h)) for h in a_heads(job)], axis=0)
            k = head(kv, o_ak, job)
        else:
            q = head(qg, 0, job - A_KV_HEADS)
            k = head(kv, 0, job - A_KV_HEADS)
        return lax.dot_general(q, k, nt, preferred_element_type=F32)

    def finish(job, s):
        if job < A_KV_HEADS:
            return _softmax_pv(s, head(kv, o_av, job), _sink_column(sink_ref, a_heads(job), s_len))
        return _softmax_pv(s, head(kv, o_cv, job - A_KV_HEADS), None)

    res = _software_pipeline(A_KV_HEADS + C_HEADS, scores, finish)
    pieces = [None] * A_HEADS
    for g in range(A_KV_HEADS):
        for hh, h in enumerate(a_heads(g)):
            pieces[A_PERM.index(h)] = res[g][hh * s_len:(hh + 1) * s_len]
    ya = jnp.concatenate(pieces, axis=1)
    ag = qg[:, o_ag:o_ag + A_WIDTH].astype(F32)
    ya_ref[0] = (ya * (ag * _sigmoid(ag))).astype(BF16)
    yc = jnp.concatenate(res[A_KV_HEADS:], axis=1)
    cg = qg[:, o_cg:o_cg + C_WIDTH].astype(F32)
    yc_ref[0] = (yc * (cg * _sigmoid(cg))).astype(BF16)


def _ctx_attn(sink, u, *, nb, lb):
    u3 = u.reshape(nb, lb, NAT_W)
    qg_w = U_CK - U_CQ
    assert U_CQ % qg_w == 0 and U_CK % PROJ_TN == 0 and U_KV_W == 2 * PROJ_TN
    return pl.pallas_call(
        _ctx_attn_kernel,
        grid=(nb,),
        in_specs=[
            pl.BlockSpec(memory_space=pltpu.SMEM),
            pl.BlockSpec((1, lb, qg_w), lambda b: (b, 0, U_CQ // qg_w)),
            pl.BlockSpec((1, lb, PROJ_TN), lambda b: (b, 0, U_CK // PROJ_TN)),
            pl.BlockSpec((1, lb, PROJ_TN), lambda b: (b, 0, U_CK // PROJ_TN + 1)),
        ],
        out_specs=[
            pl.BlockSpec((1, lb, A_WIDTH), lambda b: (b, 0, 0)),
            pl.BlockSpec((1, lb, C_WIDTH), lambda b: (b, 0, 0)),
        ],
        out_shape=[
            jax.ShapeDtypeStruct((nb, lb, A_WIDTH), BF16),
            jax.ShapeDtypeStruct((nb, lb, C_WIDTH), BF16),
        ],
        compiler_params=_cparams(("arbitrary",)),
        name="ctx_attn",
    )(sink, u3, u3, u3)


def _own_half(v, half):
    lane = lax.broadcasted_iota(jnp.int32, v.shape, v.ndim - 1) % LANES
    keep = (lane < HEAD_DIM) if half == 0 else (lane >= HEAD_DIM)
    return jnp.where(keep, v, jnp.ones_like(v))


def _stage_request(k_ref, kt_scr, v_ref, v_scr, ck_ref, ckt_scr, cv_ref, cv_scr, chunk):
    n_tok = k_ref.shape[1]

    def body(c, carry):
        start = pl.multiple_of(c * chunk, chunk)
        blk = k_ref[0, pl.ds(start, chunk), :].astype(F32)
        kt_scr[:, pl.ds(start, chunk)] = blk.T.astype(BF16)
        v_blk = v_ref[0, pl.ds(start, chunk), :]
        for half in range(2):
            v_scr[half, pl.ds(start, chunk), :] = _own_half(v_blk, half)
        return carry

    lax.fori_loop(0, n_tok // chunk, body, 0)
    ckt_scr[...] = ck_ref[0, 0].T.astype(BF16)
    ctx_v = cv_ref[0, 0].astype(BF16)
    for half in range(2):
        cv_scr[half] = _own_half(ctx_v, half)


PIPELINE_DEPTH = 2


def _software_pipeline(n, first_stage, second_stage):
    pending = [first_stage(h) for h in range(min(PIPELINE_DEPTH, n))]
    res = []
    for h in range(n):
        if h + PIPELINE_DEPTH < n:
            pending.append(first_stage(h + PIPELINE_DEPTH))
        res.append(second_stage(h, pending.pop(0)))
    return res


def _pair_scores(q_half, kt_loc, bias_fn, kt_ctx):
    s_loc = bias_fn(jnp.dot(q_half, kt_loc, preferred_element_type=F32))
    s_ctx = jnp.dot(q_half, kt_ctx, preferred_element_type=F32)
    return s_loc, s_ctx


def _pair_finish(scores, v_loc, v_ctx, sink_col):
    s_loc, s_ctx = scores
    m = jnp.maximum(jnp.max(s_loc, axis=-1, keepdims=True), jnp.max(s_ctx, axis=-1, keepdims=True))
    if sink_col is not None:
        m = jnp.maximum(m, sink_col)
    e_loc = jnp.exp2(s_loc - m).astype(BF16)
    e_ctx = jnp.exp2(s_ctx - m).astype(BF16)
    o = (jnp.dot(e_loc, v_loc, preferred_element_type=F32)
         + jnp.dot(e_ctx, v_ctx, preferred_element_type=F32))
    den = pltpu.roll(o, HEAD_DIM, 1)
    if sink_col is not None:
        den = den + jnp.exp2(sink_col - m)
    return o * (1.0 / den)


WIN_QBLOCKS = 4


def _lat_win_kernel(sink_ref, q_ref, k_ref, v_ref, ag_ref, ck_ref, cv_ref, o_ref,
                    kt_scr, v_scr, ckt_scr, cv_scr):
    i = pl.program_id(1)
    nblk = pl.num_programs(1) * WIN_QBLOCKS
    blk = A_WINDOW
    span = 3 * blk

    @pl.when(i == 0)
    def _():
        _stage_request(k_ref, kt_scr, v_ref, v_scr, ck_ref, ckt_scr, cv_ref, cv_scr, blk)

    lane = lax.broadcasted_iota(jnp.int32, (blk, LANES), 1)
    low = lane < HEAD_DIM
    starts, masks = [], []
    for sb in range(WIN_QBLOCKS):
        ib = WIN_QBLOCKS * i + sb
        start = pl.multiple_of(jnp.clip(ib - 1, 0, nblk - 3) * blk, blk)
        q_pos = ib * blk + lax.broadcasted_iota(jnp.int32, (blk, span), 0)
        k_pos = start + lax.broadcasted_iota(jnp.int32, (blk, span), 1)
        valid = jnp.abs(q_pos - k_pos) <= A_WINDOW
        starts.append(start)
        masks.append(functools.partial(lambda s, valid: jnp.where(valid, s, NEG_INF), valid=valid))

    def scores(job):
        sb, head = divmod(job, A_HEADS)
        pair, half = divmod(head, 2)
        rows_m = slice(pair // A_GROUP * LANES, (pair // A_GROUP + 1) * LANES)
        qp = q_ref[0, sb * blk:(sb + 1) * blk, pair * LANES:(pair + 1) * LANES]
        q_half = jnp.where(low if half == 0 else ~low, qp, jnp.zeros_like(qp))
        return _pair_scores(q_half, kt_scr[rows_m, pl.ds(starts[sb], span)], masks[sb], ckt_scr[rows_m, :])

    def finish(job, sc):
        sb, head = divmod(job, A_HEADS)
        half = head % 2
        rows_m = slice(head // (2 * A_GROUP) * LANES, (head // (2 * A_GROUP) + 1) * LANES)
        return _pair_finish(sc, v_scr[half, pl.ds(starts[sb], span), rows_m], cv_scr[half, :, rows_m],
                            sink_ref[A_PERM[head]] * LOG2E)

    res = _software_pipeline(WIN_QBLOCKS * A_HEADS, scores, finish)
    for sb in range(WIN_QBLOCKS):
        r = res[sb * A_HEADS:(sb + 1) * A_HEADS]
        ya = jnp.concatenate([jnp.where(low, r[2 * p], r[2 * p + 1]) for p in range(A_HEADS // 2)], axis=1)
        ag = ag_ref[0, sb * blk:(sb + 1) * blk, :].astype(F32)
        o_ref[0, sb * blk:(sb + 1) * blk, :] = (ya * (ag * _sigmoid(ag))).astype(BF16)


def _lat_win_attn(sink, u, cache_k, cache_v, layer, *, nb, lb):
    blk = WIN_QBLOCKS * A_WINDOW
    nblk = lb // blk
    assert lb % blk == 0 and lb // A_WINDOW >= 3
    n_ctx = cache_k.shape[2]
    u3 = u.reshape(nb, lb, NAT_W)
    ctx_spec = pl.BlockSpec((1, 1, n_ctx, A_KV_WIDTH), lambda b, i: (b, layer, 0, 0))
    once = pl.Buffered(1)
    return pl.pallas_call(
        _lat_win_kernel,
        grid=(nb, nblk),
        in_specs=[
            pl.BlockSpec(memory_space=pltpu.SMEM),
            pl.BlockSpec((1, blk, A_WIDTH), lambda b, i: (b, i, U_AQ // A_WIDTH)),
            pl.BlockSpec((1, lb, A_KV_WIDTH), lambda b, i: (b, 0, U_AK // A_KV_WIDTH), pipeline_mode=once),
            pl.BlockSpec((1, lb, A_KV_WIDTH), lambda b, i: (b, 0, U_AV // A_KV_WIDTH), pipeline_mode=once),
            pl.BlockSpec((1, blk, A_WIDTH), lambda b, i: (b, i, U_AG // A_WIDTH)),
            ctx_spec, ctx_spec,
        ],
        out_specs=pl.BlockSpec((1, blk, A_WIDTH), lambda b, i: (b, i, 0)),
        out_shape=jax.ShapeDtypeStruct((nb, lb, A_WIDTH), BF16),
        scratch_shapes=[pltpu.VMEM((A_KV_WIDTH, lb), BF16),
                        pltpu.VMEM((2, lb, A_KV_WIDTH), BF16),
                        pltpu.VMEM((A_KV_WIDTH, n_ctx), BF16),
                        pltpu.VMEM((2, n_ctx, A_KV_WIDTH), BF16)],
        compiler_params=_cparams(("arbitrary", "arbitrary")),
        name="lat_win_attn",
    )(sink, u3, u3, u3, u3, cache_k, cache_v)


NBR_QROWS = 2
NBR_BAND = NA_ROWS + NBR_QROWS
NBR_QBLOCKS = 4


def _lat_nbr_kernel(q_ref, k_ref, v_ref, cg_ref, ck_ref, cv_ref, t2_ref, o_ref,
                    kt_scr, v_scr, ckt_scr, cv_scr, *, grid_rows):
    i = pl.program_id(1)

    @pl.when(i == 0)
    def _():
        _stage_request(k_ref, kt_scr, v_ref, v_scr, ck_ref, ckt_scr, cv_ref, cv_scr, LANES)

    nq = NBR_QROWS * GRID_W
    nk = NBR_BAND * GRID_W
    lane = lax.broadcasted_iota(jnp.int32, (nq, LANES), 1)
    low = lane < HEAD_DIM
    r0s, bss, starts, valids = [], [], [], []
    for sb in range(NBR_QBLOCKS):
        r0 = NBR_QROWS * (NBR_QBLOCKS * i + sb)
        bs = jnp.clip(r0 - NA_ROWS // 2, 0, grid_rows - NBR_BAND)
        q_row = r0 + lax.broadcasted_iota(jnp.int32, (nq, nk), 0) // GRID_W
        rstart = jnp.clip(q_row - NA_ROWS // 2, 0, grid_rows - NA_ROWS)
        k_row = bs + lax.broadcasted_iota(jnp.int32, (nq, nk), 1) // GRID_W
        r0s.append(r0)
        bss.append(bs)
        starts.append(pl.multiple_of(bs * GRID_W, LANES))
        valids.append((k_row >= rstart) & (k_row < rstart + NA_ROWS))

    def scores(job):
        sb, h = divmod(job, C_HEADS)
        m, half = divmod(h, 2)
        rows_m = slice(m * LANES, (m + 1) * LANES)
        bias_rows = []
        for a in range(NBR_QROWS):
            tiles = []
            for p in range(NBR_BAND // 2):
                d = (bss[sb] + 2 * p) - (r0s[sb] + a) + (NA_ROWS - 1)
                idx = jnp.clip(d, -1, 2 * NA_ROWS - 2) + 1
                tiles.append(t2_ref[h, idx])
            bias_rows.append(jnp.concatenate(tiles, axis=1))
        bias = jnp.concatenate(bias_rows, axis=0)
        valid = valids[sb]
        qp = q_ref[0, sb * nq:(sb + 1) * nq, rows_m]
        q_half = jnp.where(low if half == 0 else ~low, qp, jnp.zeros_like(qp))
        return _pair_scores(q_half, kt_scr[rows_m, pl.ds(starts[sb], nk)],
                            lambda s: jnp.where(valid, s + bias, NEG_INF), ckt_scr[rows_m, :])

    def finish(job, sc):
        sb, h = divmod(job, C_HEADS)
        m, half = divmod(h, 2)
        rows_m = slice(m * LANES, (m + 1) * LANES)
        return _pair_finish(sc, v_scr[half, pl.ds(starts[sb], nk), rows_m], cv_scr[half, :, rows_m], None)

    res = _software_pipeline(NBR_QBLOCKS * C_HEADS, scores, finish)
    for sb in range(NBR_QBLOCKS):
        r = res[sb * C_HEADS:(sb + 1) * C_HEADS]
        yc = jnp.concatenate([jnp.where(low, r[2 * m], r[2 * m + 1]) for m in range(C_HEADS // 2)], axis=1)
        cg = cg_ref[0, sb * nq:(sb + 1) * nq, :].astype(F32)
        o_ref[0, sb * nq:(sb + 1) * nq, :] = (yc * (cg * _sigmoid(cg))).astype(BF16)


def _nbr_bias_table(rpb):
    w = np.arange(GRID_W)
    cstart = np.clip(w - NA_COLS // 2, 0, GRID_W - NA_COLS)
    j = np.arange(GRID_W)
    in_win = (j[None, :] >= cstart[:, None]) & (j[None, :] < cstart[:, None] + NA_COLS)
    n_dcol = 2 * NA_COLS - 1
    n_drow = 2 * NA_ROWS - 1
    dcol = j[None, :] - w[:, None] + NA_COLS - 1
    onehot = (np.arange(n_dcol)[:, None, None] == dcol[None]) & in_win[None]
    rp = jnp.pad(rpb.astype(F32) * LOG2E, ((0, 0), (1, 1), (0, 0)))
    r2 = jnp.concatenate([rp[:, :-1], rp[:, 1:]], axis=-1)
    sel = np.zeros((2, n_dcol, GRID_W, 2, GRID_W), np.float32)
    for half in range(2):
        sel[half, :, :, half, :] = onehot
    t = jnp.einsum('hek,kwsj->hewsj', r2, jnp.asarray(sel.reshape(2 * n_dcol, GRID_W, 2, GRID_W)),
                   precision=HIGHEST)
    d = np.arange(2 * NA_ROWS)[:, None] - 1 + np.arange(2)[None, :]
    ok = ((d >= 0) & (d < n_drow))[:, None, :, None] & in_win[None, :, None, :]
    t = jnp.where(jnp.asarray(ok)[None], t, NEG_INF)
    return t.reshape(rpb.shape[0], 2 * NA_ROWS, GRID_W, 2 * GRID_W)


def _lat_nbr_attn(u, cache_k, cache_v, layer, t2, *, nb, lb):
    grid_rows = lb // GRID_W
    step_rows = NBR_QBLOCKS * NBR_QROWS
    assert grid_rows >= NBR_BAND and grid_rows % step_rows == 0 and NBR_QROWS * GRID_W == LANES
    nq = step_rows * GRID_W
    n_ctx = cache_k.shape[2]
    u3 = u.reshape(nb, lb, NAT_W)
    ctx_spec = pl.BlockSpec((1, 1, n_ctx, C_WIDTH), lambda b, i: (b, layer, 0, 0))
    kernel = functools.partial(_lat_nbr_kernel, grid_rows=grid_rows)
    once = pl.Buffered(1)
    return pl.pallas_call(
        kernel,
        grid=(nb, grid_rows // step_rows),
        in_specs=[
            pl.BlockSpec((1, nq, C_WIDTH), lambda b, i: (b, i, U_CQ // C_WIDTH)),
            pl.BlockSpec((1, lb, C_WIDTH), lambda b, i: (b, 0, U_CK // C_WIDTH), pipeline_mode=once),
            pl.BlockSpec((1, lb, C_WIDTH), lambda b, i: (b, 0, U_CV // C_WIDTH), pipeline_mode=once),
            pl.BlockSpec((1, nq, C_WIDTH), lambda b, i: (b, i, U_CG // C_WIDTH)),
            ctx_spec, ctx_spec,
            pl.BlockSpec(t2.shape, lambda b, i: (0, 0, 0, 0), pipeline_mode=once),
        ],
        out_specs=pl.BlockSpec((1, nq, C_WIDTH), lambda b, i: (b, i, 0)),
        out_shape=jax.ShapeDtypeStruct((nb, lb, C_WIDTH), BF16),
        scratch_shapes=[pltpu.VMEM((C_WIDTH, lb), BF16),
                        pltpu.VMEM((2, lb, C_WIDTH), BF16),
                        pltpu.VMEM((C_WIDTH, n_ctx), BF16),
                        pltpu.VMEM((2, n_ctx, C_WIDTH), BF16)],
        compiler_params=_cparams(("arbitrary", "arbitrary")),
        name="lat_nbr_attn",
    )(u3, u3, u3, u3, cache_k, cache_v, t2)


HYENA_EMB_PAD = 40


def _filter_kernel(w1t_ref, b1_ref, fr_ref, w2t_ref, b2_ref, w3t_ref, decay_ref, o_ref, z_scr):
    L = o_ref.shape[1]

    @pl.when(pl.program_id(0) == 0)
    def _():
        t = lax.broadcasted_iota(jnp.int32, (HYENA_EMB_PAD, L), 1).astype(F32) / L
        r = lax.broadcasted_iota(jnp.int32, (HYENA_EMB_PAD, L), 0)
        band = ((r - 1) % HYENA_BANDS + 1).astype(F32) * (2.0 * np.pi)
        ang = t * band
        feats = jnp.where(r == 0, t, jnp.where(r <= HYENA_BANDS, jnp.sin(ang), jnp.cos(ang)))
        z = jnp.dot(w1t_ref[...], feats, precision=HIGHEST, preferred_element_type=F32) + b1_ref[...]
        z = jnp.sin(fr_ref[0] * z)
        z = jnp.dot(w2t_ref[...], z, precision=HIGHEST, preferred_element_type=F32) + b2_ref[...]
        z_scr[...] = jnp.sin(fr_ref[1] * z)

    h = jnp.dot(w3t_ref[...], z_scr[...], precision=HIGHEST, preferred_element_type=F32)
    t_row = lax.broadcasted_iota(jnp.int32, (1, L), 1).astype(F32) / L
    o_ref[...] = h * jnp.exp(-jnp.abs(decay_ref[...]) * t_row)


def _hyena_filters(L, hy_w1, hy_b1, hy_w2, hy_b2, hy_freq, hy_w3, hy_decay):
    hidden = hy_w1.shape[1]
    n_out = hy_w3.shape[1]
    rb = 512
    w1t = jnp.pad(hy_w1.T, ((0, 0), (0, HYENA_EMB_PAD - hy_w1.shape[0])))
    const = lambda a: pl.BlockSpec(a.shape, lambda i: (0,) * a.ndim)
    args = (w1t, hy_b1.reshape(hidden, 1), hy_freq.reshape(2, hidden, 1), hy_w2.T, hy_b2.reshape(hidden, 1))
    out = pl.pallas_call(
        _filter_kernel,
        grid=(n_out // rb,),
        in_specs=[const(a) for a in args] + [
            pl.BlockSpec((rb, hidden), lambda i: (i, 0)),
            pl.BlockSpec((rb, 1), lambda i: (i, 0)),
        ],
        out_specs=pl.BlockSpec((rb, L), lambda i: (i, 0)),
        out_shape=jax.ShapeDtypeStruct((n_out, L), F32),
        scratch_shapes=[pltpu.VMEM((hidden, L), F32)],
        compiler_params=_cparams(("arbitrary",)),
        name="hyena_filter",
    )(*args, hy_w3.T, hy_decay.reshape(n_out, 1))
    return out.reshape(HYENA_ORDER, 2, B_WIDTH, L)


def _short_conv_params(hy_conv_w, hy_conv_b):
    return jnp.concatenate([hy_conv_w.T, hy_conv_b[:, None]], axis=1)


def _dense_dft_consts(S):
    n = 2 * S
    t = np.arange(S)[:, None]
    k = np.arange(n)[None, :]
    ang = -2.0 * np.pi * t * k / n
    cr, ci = np.cos(ang), np.sin(ang)
    wf = np.block([[cr, ci], [-ci, cr]])
    er, ei = cr.T, -ci.T
    wi = np.block([[er, ei], [-ei, er]])
    return jnp.asarray(wf, F32), jnp.asarray(wi, F32)


def _ctx_spec_kernel(f_ref, skip_ref, wf_ref, hr_ref, hi_ref, *, S):
    n = 2 * S
    fwd = f_ref[0, 0]
    bwd = f_ref[0, 1]
    lane = lax.broadcasted_iota(jnp.int32, bwd.shape, 1)
    bwd = jnp.where(lane == 0, 0.0, bwd)
    w = wf_ref[0:S, :]
    ff = jnp.dot(fwd, w, precision=HIGHEST, preferred_element_type=F32)
    fb = jnp.dot(bwd, w, precision=HIGHEST, preferred_element_type=F32)
    skip = skip_ref[0]
    hr_ref[0] = (ff[:, :n] + fb[:, :n] + skip) * (1.0 / n)
    hi_ref[0] = (ff[:, n:] - fb[:, n:]) * (1.0 / n)


def _ctx_spectrum(filt, skip, wf, S):
    n = 2 * S
    kernel = functools.partial(_ctx_spec_kernel, S=S)
    return pl.pallas_call(
        kernel,
        grid=(HYENA_ORDER,),
        in_specs=[
            pl.BlockSpec((1, 2, B_WIDTH, S), lambda o: (o, 0, 0, 0)),
            pl.BlockSpec((1, B_WIDTH, 1), lambda o: (o, 0, 0)),
            pl.BlockSpec(wf.shape, lambda o: (0, 0)),
        ],
        out_specs=[pl.BlockSpec((1, B_WIDTH, n), lambda o: (o, 0, 0))] * 2,
        out_shape=[jax.ShapeDtypeStruct((HYENA_ORDER, B_WIDTH, n), F32)] * 2,
        compiler_params=_cparams(("arbitrary",)),
        name="hyena_ctx_spectrum",
    )(filt, skip.reshape(HYENA_ORDER, B_WIDTH, 1), wf)


def _hyena_ctx_kernel(prm_ref, v_ref, x1_ref, x2_ref, bg_ref, hr_ref, hi_ref, wf_ref, wi_ref, o_ref, *, S):
    nb, cb, _ = v_ref.shape
    half = nb // 2
    n = 2 * S
    lane = lax.broadcasted_iota(jnp.int32, (nb, cb, S), 2)

    def short_conv(u, p):
        prev = jnp.where(lane == 0, 0.0, pltpu.roll(u, 1, 2))
        nxt = jnp.where(lane == S - 1, 0.0, pltpu.roll(u, S - 1, 2))
        return p[:, 3:4] + prev * p[:, 0:1] + u * p[:, 1:2] + nxt * p[:, 2:3]

    def stack(u):
        return jnp.concatenate([u[:half], u[half:]], axis=-1)

    def conv(xs, o):
        spec = jnp.dot(xs.reshape(half * cb, 2 * S).astype(BF16), wf_ref[...],
                       preferred_element_type=F32).reshape(half, cb, 2 * n)
        xr, xi = spec[..., :n], spec[..., n:]
        hr, hi = hr_ref[o], hi_ref[o]
        y = jnp.concatenate([xr * hr - xi * hi, xr * hi + xi * hr], axis=-1)
        return jnp.dot(y.reshape(half * cb, 2 * n).astype(BF16), wi_ref[...],
                       preferred_element_type=F32).reshape(half, cb, 2 * S)

    v = stack(short_conv(v_ref[...], prm_ref[0]))
    x1 = stack(short_conv(x1_ref[...], prm_ref[1]))
    x2 = stack(short_conv(x2_ref[...], prm_ref[2]))
    z = x1 * conv(v, 0)
    y = x2 * conv(z, 1)
    bg = stack(bg_ref[...])
    y = y * (bg * _sigmoid(bg))
    o_ref[0:half] = y[..., :S]
    o_ref[half:nb] = y[..., S:]


def _hyena_ctx(t_arr, prm, hr, hi, wf, wi, *, nb, S):
    cb = 16
    n = 2 * S
    ncb = B_WIDTH // cb
    kernel = functools.partial(_hyena_ctx_kernel, S=S)

    def part(k):
        return pl.BlockSpec((nb, cb, S), lambda c: (0, c + k * ncb, 0))

    return pl.pallas_call(
        kernel,
        grid=(ncb,),
        in_specs=[
            pl.BlockSpec((3, cb, 4), lambda c: (0, c, 0)),
            part(0), part(1), part(2), part(3),
            pl.BlockSpec((HYENA_ORDER, cb, n), lambda c: (0, c, 0)),
            pl.BlockSpec((HYENA_ORDER, cb, n), lambda c: (0, c, 0)),
            pl.BlockSpec(wf.shape, lambda c: (0, 0)),
            pl.BlockSpec(wi.shape, lambda c: (0, 0)),
        ],
        out_specs=pl.BlockSpec((nb, cb, S), lambda c: (0, c, 0)),
        out_shape=jax.ShapeDtypeStruct((nb, B_WIDTH, S), F32),
        compiler_params=_cparams(("arbitrary",)),
        name="hyena_ctx",
    )(prm.reshape(3, B_WIDTH, 4), t_arr, t_arr, t_arr, t_arr, hr, hi, wf.astype(BF16), wi.astype(BF16))


def _two_stage_consts(L, nseq, dtype):
    n2 = LANES
    n = 2 * L
    n1 = n // n2
    h1 = n1 // 2
    k1 = np.arange(n1)[:, None]
    a = -2.0 * np.pi * k1 * np.arange(h1)[None, :] / n1
    w1r, w1i = np.cos(a), np.sin(a)
    w1big = np.block([[w1r, -w1i], [w1i, w1r]])
    vr, vi = w1r.T, -w1i.T
    w1inv = np.block([[vr, -vi], [vi, vr]])
    a = -2.0 * np.pi * k1 * np.arange(n2)[None, :] / n
    twr, twi = np.cos(a), np.sin(a)
    a = -2.0 * np.pi * np.arange(n2)[:, None] * np.arange(n2)[None, :] / n2
    w2r, w2i = np.cos(a), np.sin(a)
    w2big = np.block([[w2r, w2i], [-w2i, w2r]])
    w2conj = np.block([[w2r, -w2i], [w2i, w2r]])
    f = lambda x: jnp.asarray(x, F32)
    m = lambda x: jnp.asarray(x, F32).astype(dtype)
    return dict(
        w1b=jnp.broadcast_to(m(w1big)[None], (nseq,) + w1big.shape),
        w1ib=jnp.broadcast_to(m(w1inv)[None], (nseq,) + w1inv.shape),
        w2=m(w2big), w2c=m(w2conj), twr=f(twr), twi=f(twi), n1=n1, h1=h1)


def _dft_dot(spec, a, b):
    if a.dtype == BF16 or b.dtype == BF16:
        return jnp.einsum(spec, a.astype(BF16), b.astype(BF16), preferred_element_type=F32)
    a_hi, b_hi = a.astype(BF16), b.astype(BF16)
    a_lo = (a - a_hi.astype(F32)).astype(BF16)
    b_lo = (b - b_hi.astype(F32)).astype(BF16)
    mm = lambda x, y: jnp.einsum(spec, x, y, preferred_element_type=F32)
    return mm(a_hi, b_hi) + (mm(a_hi, b_lo) + mm(a_lo, b_hi))


def _cmul(ar, ai, br, bi):
    return ar * br - ai * bi, ar * bi + ai * br


def _stage_fwd(xs, w1b, twr, twi, w2):
    s = xs.shape[0]
    n1 = twr.shape[0]
    a = _dft_dot('smk,skn->smn', w1b, xs)
    p = jnp.concatenate(_cmul(a[:, :n1], a[:, n1:], twr, twi), axis=-1)
    return _dft_dot('mk,kn->mn', p.reshape(s * n1, 2 * LANES), w2)


def _stage_inv(cm, w1ib, twr, twi, w2c, s):
    n1 = twr.shape[0]
    dm = _dft_dot('mk,kn->mn', cm, w2c).reshape(s, n1, 2 * LANES)
    r = jnp.concatenate(_cmul(dm[..., :LANES], dm[..., LANES:], twr, -twi), axis=1)
    return _dft_dot('smk,skn->smn', w1ib, r)


def _lat_spec_kernel(skip_ref, f_ref, w1b_ref, twr_ref, twi_ref, w2_ref, hr_ref, hi_ref, *, n_fft):
    cb, h1 = f_ref.shape[2], f_ref.shape[3]
    n1 = twr_ref.shape[0]
    fwd = f_ref[0, 0]
    bwd = f_ref[0, 1]
    first = (lax.broadcasted_iota(jnp.int32, bwd.shape, 1) == 0) & \
            (lax.broadcasted_iota(jnp.int32, bwd.shape, 2) == 0)
    bwd = jnp.where(first, 0.0, bwd)
    xs = jnp.concatenate([fwd, bwd], axis=0)
    w1_real = w1b_ref[:, :, 0:h1]
    sp = _stage_fwd(xs, w1_real, twr_ref[...], twi_ref[...], w2_ref[...]).reshape(2, cb, n1, 2 * LANES)
    inv = 1.0 / n_fft
    for c in range(cb):
        skip = skip_ref[pl.program_id(0), pl.program_id(1) * cb + c]
        hr_ref[0, c] = (sp[0, c, :, :LANES] + sp[1, c, :, :LANES] + skip) * inv
        hi_ref[0, c] = (sp[0, c, :, LANES:] - sp[1, c, :, LANES:]) * inv


def _lat_spectrum(filt, skip, L):
    cb = 16
    cs = _two_stage_consts(L, 2 * cb, F32)
    n1, h1 = cs["n1"], cs["h1"]
    filt5 = filt.reshape(HYENA_ORDER, 2, B_WIDTH, h1, LANES)
    kernel = functools.partial(_lat_spec_kernel, n_fft=2 * L)
    const = lambda a: pl.BlockSpec(a.shape, lambda o, c: (0,) * a.ndim)
    return pl.pallas_call(
        kernel,
        grid=(HYENA_ORDER, B_WIDTH // cb),
        in_specs=[
            pl.BlockSpec(memory_space=pltpu.SMEM),
            pl.BlockSpec((1, 2, cb, h1, LANES), lambda o, c: (o, 0, c, 0, 0)),
            const(cs["w1b"]), const(cs["twr"]), const(cs["twi"]), const(cs["w2"]),
        ],
        out_specs=[pl.BlockSpec((1, cb, n1, LANES), lambda o, c: (o, c, 0, 0))] * 2,
        out_shape=[jax.ShapeDtypeStruct((HYENA_ORDER, B_WIDTH, n1, LANES), F32)] * 2,
        compiler_params=_cparams(("arbitrary", "arbitrary")),
        name="hyena_lat_spectrum",
    )(skip, filt5, cs["w1b"], cs["twr"], cs["twi"], cs["w2"])


def _hyena_lat_kernel(prm_ref, v_ref, x1_ref, x2_ref, bg_ref, hr_ref, hi_ref,
                      w1b_ref, w1ib_ref, twr_ref, twi_ref, w2_ref, w2c_ref, o_ref,
                      vs_ref, x1s_ref, x2s_ref):
    nb, cb, h1, _ = v_ref.shape
    half = nb // 2
    s = cb * half
    n1 = twr_ref.shape[0]
    c0 = pl.program_id(0) * cb
    row = lax.broadcasted_iota(jnp.int32, (nb, h1, LANES), 1)
    lane = lax.broadcasted_iota(jnp.int32, (nb, h1, LANES), 2)

    def short_conv(u, part, c):
        r = pltpu.roll(u, 1, 2)
        prev = jnp.where(lane == 0, pltpu.roll(r, 1, 1), r)
        prev = jnp.where((lane == 0) & (row == 0), 0.0, prev)
        r = pltpu.roll(u, LANES - 1, 2)
        nxt = jnp.where(lane == LANES - 1, pltpu.roll(r, h1 - 1, 1), r)
        nxt = jnp.where((lane == LANES - 1) & (row == h1 - 1), 0.0, nxt)
        ch = part * B_WIDTH + c0 + c
        return prm_ref[3, ch] + prev * prm_ref[0, ch] + u * prm_ref[1, ch] + nxt * prm_ref[2, ch]

    def stack_into(dst_ref, u, c):
        dst_ref[c * half:(c + 1) * half, 0:h1, :] = u[:half]
        dst_ref[c * half:(c + 1) * half, h1:2 * h1, :] = u[half:]

    for c in range(cb):
        stack_into(vs_ref, short_conv(v_ref[:, c], 0, c), c)
        stack_into(x1s_ref, short_conv(x1_ref[:, c], 1, c), c)
        bg = bg_ref[:, c]
        stack_into(x2s_ref, short_conv(x2_ref[:, c], 2, c) * (bg * _sigmoid(bg)), c)

    twr, twi = twr_ref[...], twi_ref[...]

    def conv(xs, o):
        sp = _stage_fwd(xs, w1b_ref[...], twr, twi, w2_ref[...]).reshape(cb, half, n1, 2 * LANES)
        hr = hr_ref[o][:, None]
        hi = hi_ref[o][:, None]
        cm = jnp.concatenate(_cmul(sp[..., :LANES], sp[..., LANES:], hr, hi), axis=-1)
        return _stage_inv(cm.reshape(s * n1, 2 * LANES), w1ib_ref[...], twr, twi, w2c_ref[...], s)

    z = x1s_ref[...] * conv(vs_ref[...], 0)
    y = x2s_ref[...] * conv(z, 1)
    for c in range(cb):
        o_ref[0:half, c] = y[c * half:(c + 1) * half, 0:h1]
        o_ref[half:nb, c] = y[c * half:(c + 1) * half, h1:2 * h1]


def _hyena_lat(t_arr, prm, hr, hi, *, nb, L):
    cb = 8
    half = nb // 2
    s = cb * half
    cs = _two_stage_consts(L, s, BF16)
    n1, h1 = cs["n1"], cs["h1"]
    ncb = B_WIDTH // cb
    t5 = t_arr.reshape(nb, T_W, h1, LANES)

    def part(k):
        return pl.BlockSpec((nb, cb, h1, LANES), lambda c: (0, c + k * ncb, 0, 0))

    const = lambda a: pl.BlockSpec(a.shape, lambda c: (0,) * a.ndim)
    return pl.pallas_call(
        _hyena_lat_kernel,
        grid=(ncb,),
        in_specs=[
            pl.BlockSpec(memory_space=pltpu.SMEM),
            part(0), part(1), part(2), part(3),
            pl.BlockSpec((HYENA_ORDER, cb, n1, LANES), lambda c: (0, c, 0, 0)),
            pl.BlockSpec((HYENA_ORDER, cb, n1, LANES), lambda c: (0, c, 0, 0)),
            const(cs["w1b"]), const(cs["w1ib"]), const(cs["twr"]), const(cs["twi"]),
            const(cs["w2"]), const(cs["w2c"]),
        ],
        out_specs=pl.BlockSpec((nb, cb, h1, LANES), lambda c: (0, c, 0, 0)),
        out_shape=jax.ShapeDtypeStruct((nb, B_WIDTH, h1, LANES), F32),
        scratch_shapes=[pltpu.VMEM((s, 2 * h1, LANES), F32)] * 3,
        compiler_params=_cparams(("arbitrary",)),
        name="hyena_lat",
    )(prm.T, t5, t5, t5, t5, hr, hi,
      cs["w1b"], cs["w1ib"], cs["twr"], cs["twi"], cs["w2"], cs["w2c"])


def _merge_kernel(x_ref, ya_ref, ybt_ref, yc_ref, mg_ref, gate_ref, wa_ref, wb_ref, wc_ref, wo_ref,
                  nw_ref, *rest, final):
    if final:
        (o_ref,) = rest
    else:
        sc_ref, sh_ref, o_ref, h_ref = rest
    mg = mg_ref[...]
    tm = mg.shape[0]
    if len(ybt_ref.shape) == 4:
        n_rows = tm // LANES
        row0 = (pl.program_id(0) % (ybt_ref.shape[2] // n_rows)) * n_rows
        yb = jnp.concatenate([ybt_ref[0, :, row0 + r, :].T for r in range(n_rows)], axis=0).astype(BF16)
    else:
        yb = ybt_ref[0].T.astype(BF16)
    m = (_sigmoid(mg[:, :D_MODEL].astype(F32)) * jnp.dot(ya_ref[...], wa_ref[...], preferred_element_type=F32)
         + _sigmoid(mg[:, D_MODEL:2 * D_MODEL].astype(F32)) * jnp.dot(yb, wb_ref[...], preferred_element_type=F32)
         + _sigmoid(mg[:, 2 * D_MODEL:].astype(F32)) * jnp.dot(yc_ref[...], wc_ref[...], preferred_element_type=F32))
    out = jnp.dot(m.astype(BF16), wo_ref[...], preferred_element_type=F32)
    xn = x_ref[...] + gate_ref[0] * out
    if final:
        var = jnp.mean(xn * xn, axis=-1, keepdims=True)
        o_ref[...] = xn * lax.rsqrt(var + EPS) * nw_ref[...]
    else:
        o_ref[...] = xn
        h_ref[...] = _modulated_norm(xn, nw_ref[...], sc_ref[0], sh_ref[0])


def _merge(x2d, ya, ybt, yc, u, gate, wa, wb, wc, wo, nw, next_mod, *, nb, lb, tm):
    rows = nb * lb
    bpb = lb // tm
    per_mod = gate.shape[0] > 1
    final = next_mod is None
    kernel = functools.partial(_merge_kernel, final=final)
    mod_spec = pl.BlockSpec((1, 1, D_MODEL), lambda i: ((i // bpb) if per_mod else 0, 0, 0))
    row_spec = pl.BlockSpec((tm, D_MODEL), lambda i: (i, 0))
    const = lambda a: pl.BlockSpec(a.shape, lambda i: (0,) * a.ndim, pipeline_mode=pl.Buffered(1))
    if ybt.ndim == 4:
        sub = 8 * LANES // tm
        assert tm % LANES == 0 and (8 * LANES) % tm == 0 and bpb % sub == 0
        yb_spec = pl.BlockSpec((1, B_WIDTH, 8, LANES), lambda i: (i // bpb, 0, (i % bpb) // sub, 0))
    else:
        yb_spec = pl.BlockSpec((1, B_WIDTH, tm), lambda i: (i // bpb, 0, i % bpb))
    return pl.pallas_call(
        kernel,
        grid=(rows // tm,),
        in_specs=[
            pl.BlockSpec((tm, D_MODEL), lambda i: (i, 0)),
            pl.BlockSpec((tm, A_WIDTH), lambda i: (i, 0)),
            yb_spec,
            pl.BlockSpec((tm, C_WIDTH), lambda i: (i, 0)),
            pl.BlockSpec((tm, MG_W), lambda i: (i, U_MG // MG_W)),
            mod_spec,
            const(wa), const(wb), const(wc), const(wo),
            pl.BlockSpec((1, D_MODEL), lambda i: (0, 0)),
        ] + ([] if final else [mod_spec, mod_spec]),
        out_specs=row_spec if final else [row_spec, row_spec],
        out_shape=(jax.ShapeDtypeStruct((rows, D_MODEL), F32) if final else
                   [jax.ShapeDtypeStruct((rows, D_MODEL), F32), jax.ShapeDtypeStruct((rows, D_MODEL), BF16)]),
        compiler_params=_cparams(("arbitrary",)),
        name="merge_final" if final else "merge",
    )(x2d, ya.reshape(rows, A_WIDTH), ybt, yc.reshape(rows, C_WIDTH), u, gate, wa, wb, wc, wo,
      nw.reshape(1, D_MODEL), *(() if final else next_mod))


def _rope_tables(L):
    t = np.arange(L)
    row = (t // GRID_W).astype(np.float32)
    col = (t % GRID_W).astype(np.float32)
    nf = HEAD_DIM // 4
    inv = np.power(np.float32(ROPE_BASE), -np.arange(nf, dtype=np.float32) / nf).astype(np.float32)
    ang = np.concatenate([row[:, None] * inv[None], col[:, None] * inv[None]], axis=-1)
    cos, sin = np.cos(ang), np.sin(ang)
    reps = LANES // HEAD_DIM
    cos_t = np.tile(np.concatenate([cos, cos], axis=-1), (1, reps))
    sin_t = np.tile(np.concatenate([-sin, sin], axis=-1), (1, reps))
    return jnp.asarray(cos_t, F32), jnp.asarray(sin_t, F32)


def _perm_heads(w, base, axis):
    return [lax.slice_in_dim(w, base + h * HEAD_DIM, base + (h + 1) * HEAD_DIM, axis=axis) for h in A_PERM]


def _prep_w_in_kernel(w_ref, nat_ref, t_ref):
    def put(dst, src, width):
        nat_ref[0, :, dst:dst + width] = w_ref[0, :, src:src + width].astype(BF16)

    put(U_MG, IN_MG, MG_W)
    for dst, k in ((U_CQ, 0), (U_CK, 1), (U_CV, 2), (U_CG, 3)):
        put(dst, IN_CQ + k * C_WIDTH, C_WIDTH)
    for p, h in enumerate(A_PERM):
        put(U_AQ + p * HEAD_DIM, IN_AQ + h * HEAD_DIM, HEAD_DIM)
        put(U_AG + p * HEAD_DIM, IN_AG + h * HEAD_DIM, HEAD_DIM)
    put(U_AK, IN_AK, A_KV_WIDTH)
    put(U_AV, IN_AV, A_KV_WIDTH)
    t_ref[0] = w_ref[0, :, IN_BU:IN_BU + T_W].T.astype(BF16)


def _prep_w_in(w_in):
    depth, _, in_w = w_in.shape
    rb = 128
    return pl.pallas_call(
        _prep_w_in_kernel,
        grid=(depth, D_MODEL // rb),
        in_specs=[pl.BlockSpec((1, rb, in_w), lambda l, i: (l, i, 0))],
        out_specs=[pl.BlockSpec((1, rb, NAT_W), lambda l, i: (l, i, 0)),
                   pl.BlockSpec((1, T_W, rb), lambda l, i: (l, 0, i))],
        out_shape=[jax.ShapeDtypeStruct((depth, D_MODEL, NAT_W), BF16),
                   jax.ShapeDtypeStruct((depth, T_W, D_MODEL), BF16)],
        compiler_params=_cparams(("arbitrary", "arbitrary")),
        name="prep_w_in",
    )(w_in)


def kernel(x_prompt, x_sample, c, cache_a_k, cache_a_v, cache_c_k, cache_c_v, c_ctx, norm_w, w_ada, b_ada, w_in, a_sink, hy_conv_w, hy_conv_b, hy_w1, hy_b1, hy_w2, hy_b2, hy_freq, hy_w3, hy_decay, hy_skip, c_rpb, w_up_a, w_up_b, w_up_c, w_out, final_norm_w):
    nbc, S, _ = x_prompt.shape
    nbl, L, _ = x_sample.shape
    depth = w_in.shape[0]
    n_ctx = cache_a_k.shape[2]
    assert nbc % 2 == 0 and nbl % 2 == 0 and L % (GRID_W * LANES // 2) == 0

    pad = (-(nbl + 1)) % 8
    cond = jnp.concatenate([c, c_ctx[None], jnp.zeros((pad, D_MODEL), F32)], axis=0)
    mod = _adaln(cond, w_ada, b_ada)

    cos_t, sin_t = _rope_tables(L)
    tm_ctx = 1024 if (nbc * S) % 1024 == 0 else S
    zeros_t = jnp.zeros((tm_ctx, LANES), F32)
    wf_c, wi_c = _dense_dft_consts(S)
    ca_k = cache_a_k.reshape(nbl, depth, n_ctx, A_KV_WIDTH)
    ca_v = cache_a_v.reshape(nbl, depth, n_ctx, A_KV_WIDTH)
    cc_k = cache_c_k.reshape(nbl, depth, n_ctx, C_WIDTH)
    cc_v = cache_c_v.reshape(nbl, depth, n_ctx, C_WIDTH)

    w_nat, w_t = _prep_w_in(w_in)

    def mod_parts(l, rows):
        m = mod[l, rows][:, None, :]
        return 1.0 + m[..., D_MODEL:2 * D_MODEL], m[..., :D_MODEL], m[..., 2 * D_MODEL:]

    ctx_rows, lat_rows = slice(nbl, nbl + 1), slice(0, nbl)
    xp = x_prompt.reshape(nbc * S, D_MODEL)
    xs = x_sample.reshape(nbl * L, D_MODEL)
    hp = _norm_mod(xp, *mod_parts(0, ctx_rows)[:2], norm_w[0], nb=nbc, lb=S)
    hs = _norm_mod(xs, *mod_parts(0, lat_rows)[:2], norm_w[0], nb=nbl, lb=L)
    aks, avs, cks, cvs = [], [], [], []
    for l in range(depth):
        final = l == depth - 1
        next_nw = final_norm_w if final else norm_w[l + 1]
        wa = jnp.concatenate(_perm_heads(w_up_a[l], 0, 0), axis=0).astype(BF16)
        wb, wc, wo = (w.astype(BF16) for w in (w_up_b[l], w_up_c[l], w_out[l]))
        prm = _short_conv_params(hy_conv_w[l], hy_conv_b[l])
        filt_args = (hy_w1[l], hy_b1[l], hy_w2[l], hy_b2[l], hy_freq[l], hy_w3[l], hy_decay[l])
        t2 = _nbr_bias_table(c_rpb[l])

        gate = mod_parts(l, ctx_rows)[2]
        u, t_arr, kv = _inproj(hp, w_nat, w_t, zeros_t, zeros_t,
                               layer=l, nb=nbc, lb=S, tm=tm_ctx, rope=False, kv_f32=True)
        cks.append(kv[:, :U_CV - U_CK].reshape(nbc, S, C_HEADS, HEAD_DIM))
        cvs.append(kv[:, U_CV - U_CK:U_AK - U_CK].reshape(nbc, S, C_HEADS, HEAD_DIM))
        aks.append(kv[:, U_AK - U_CK:U_AV - U_CK].reshape(nbc, S, A_KV_HEADS, HEAD_DIM))
        avs.append(kv[:, U_AV - U_CK:].reshape(nbc, S, A_KV_HEADS, HEAD_DIM))
        ya, yc = _ctx_attn(a_sink[l], u, nb=nbc, lb=S)
        hr, hi = _ctx_spectrum(_hyena_filters(S, *filt_args), hy_skip[l], wf_c, S)
        ybt = _hyena_ctx(t_arr, prm, hr, hi, wf_c, wi_c, nb=nbc, S=S)
        res = _merge(xp, ya, ybt, yc, u, gate, wa, wb, wc, wo, next_nw,
                     None if final else mod_parts(l + 1, ctx_rows)[:2], nb=nbc, lb=S, tm=S)
        xp, hp = (res, None) if final else res

        gate = mod_parts(l, lat_rows)[2]
        u, t_arr = _inproj(hs, w_nat, w_t, cos_t, sin_t,
                           layer=l, nb=nbl, lb=L, tm=1024, rope=True, kv_f32=False)
        ya = _lat_win_attn(a_sink[l], u, ca_k, ca_v, l, nb=nbl, lb=L)
        yc = _lat_nbr_attn(u, cc_k, cc_v, l, t2, nb=nbl, lb=L)
        hr, hi = _lat_spectrum(_hyena_filters(L, *filt_args), hy_skip[l], L)
        ybt = _hyena_lat(t_arr, prm, hr, hi, nb=nbl, L=L)
        res = _merge(xs, ya, ybt, yc, u, gate, wa, wb, wc, wo, next_nw,
                     None if final else mod_parts(l + 1, lat_rows)[:2], nb=nbl, lb=L, tm=256)
        xs, hs = (res, None) if final else res

    y_prompt = xp.reshape(nbc, S, D_MODEL)
    y_sample = xs.reshape(nbl, L, D_MODEL)
    return (y_prompt, y_sample, jnp.stack(aks, axis=1), jnp.stack(avs, axis=1),
            jnp.stack(cks, axis=1), jnp.stack(cvs, axis=1))
```

```python
import functools

import numpy as np
import jax
import jax.numpy as jnp
from jax import lax
from jax.experimental import pallas as pl
from jax.experimental.pallas import tpu as pltpu

F32 = jnp.float32
BF16 = jnp.bfloat16
HIGHEST = lax.Precision.HIGHEST

D_MODEL = 2048
HEAD_DIM = 64
A_HEADS = 12
A_KV_HEADS = 4
A_GROUP = A_HEADS // A_KV_HEADS
A_WIDTH = A_HEADS * HEAD_DIM
A_KV_WIDTH = A_KV_HEADS * HEAD_DIM
A_WINDOW = 128
B_WIDTH = 512
HYENA_ORDER = 2
HYENA_BANDS = 16
C_HEADS = 12
C_WIDTH = C_HEADS * HEAD_DIM
GRID_W = 64
NA_ROWS = 8
NA_COLS = 16
ROPE_BASE = 10000.0
EPS = 1e-6
NEG_INF = -1e30
ATTN_SCALE = HEAD_DIM ** -0.5
LOG2E = 1.4426950408889634
QK_SCALE = ATTN_SCALE * LOG2E

LANES = 128
VMEM_LIMIT = 56 * 1024 * 1024

IN_AQ, IN_AK, IN_AV, IN_AG = 0, 768, 1024, 1280
IN_BU, IN_CQ, IN_MG = 2048, 4096, 7168
PROJ_TN = 1024
MG_W = 3 * D_MODEL
U_MG = 0
U_CQ = U_MG + MG_W
U_CG = U_CQ + C_WIDTH
U_AQ = U_CG + C_WIDTH
U_AG = U_AQ + A_WIDTH
U_CK = U_AG + A_WIDTH
U_CV = U_CK + C_WIDTH
U_AK = U_CV + C_WIDTH
U_AV = U_AK + A_KV_WIDTH
NAT_W = U_AV + A_KV_WIDTH
U_KV_W = NAT_W - U_CK
T_W = 4 * B_WIDTH

A_PERM = tuple(2 * A_GROUP * m + A_GROUP * half + j
               for m in range(A_KV_HEADS // 2) for j in range(A_GROUP) for half in range(2))


def _cparams(sem):
    return pltpu.CompilerParams(dimension_semantics=sem, vmem_limit_bytes=VMEM_LIMIT)


def _sigmoid(x):
    return 1.0 / (1.0 + jnp.exp(-x))


def _lane_chunks(off, width):
    assert off % LANES == 0 and width % LANES == 0
    per_blk = PROJ_TN // LANES
    return [divmod(g, per_blk) for g in range(off // LANES, (off + width) // LANES)]


def _adaln_kernel(cond_ref, w_ref, b_ref, o_ref):
    c = cond_ref[...]
    s = (c * _sigmoid(c)).astype(BF16)
    acc = jnp.dot(s, w_ref[0].astype(BF16), preferred_element_type=F32)
    o_ref[0] = acc + b_ref[0]


def _adaln(cond, w_ada, b_ada):
    depth = w_ada.shape[0]
    rows = cond.shape[0]
    tn = 1024
    return pl.pallas_call(
        _adaln_kernel,
        grid=(depth, 3 * D_MODEL // tn),
        in_specs=[
            pl.BlockSpec((rows, D_MODEL), lambda l, j: (0, 0)),
            pl.BlockSpec((1, D_MODEL, tn), lambda l, j: (l, 0, j)),
            pl.BlockSpec((1, 1, tn), lambda l, j: (l, 0, j)),
        ],
        out_specs=pl.BlockSpec((1, rows, tn), lambda l, j: (l, 0, j)),
        out_shape=jax.ShapeDtypeStruct((depth, rows, 3 * D_MODEL), F32),
        compiler_params=_cparams(("arbitrary", "arbitrary")),
        name="adaln",
    )(cond, w_ada, b_ada.reshape(depth, 1, 3 * D_MODEL))


def _modulated_norm(x, nw, sc, sh):
    var = jnp.mean(x * x, axis=-1, keepdims=True)
    return ((x * lax.rsqrt(var + EPS) * nw) * sc + sh).astype(BF16)


def _norm_mod_kernel(x_ref, sc_ref, sh_ref, nw_ref, h_ref):
    h_ref[...] = _modulated_norm(x_ref[...], nw_ref[...], sc_ref[0], sh_ref[0])


def _norm_mod(x2d, sc, sh, norm_w, *, nb, lb):
    per_mod = sc.shape[0] > 1
    tm = 1024 if (lb if per_mod else nb * lb) % 1024 == 0 else lb
    bpb = max(lb // tm, 1)
    mod_spec = pl.BlockSpec((1, 1, D_MODEL), lambda i: ((i // bpb) if per_mod else 0, 0, 0))
    row_spec = pl.BlockSpec((tm, D_MODEL), lambda i: (i, 0))
    return pl.pallas_call(
        _norm_mod_kernel,
        grid=(nb * lb // tm,),
        in_specs=[row_spec, mod_spec, mod_spec, pl.BlockSpec((1, D_MODEL), lambda i: (0, 0))],
        out_specs=row_spec,
        out_shape=jax.ShapeDtypeStruct((nb * lb, D_MODEL), BF16),
        compiler_params=_cparams(("arbitrary",)),
        name="norm_mod",
    )(x2d, sc, sh, norm_w.reshape(1, D_MODEL))


def _inproj_kernel(h_ref, w_ref, wt_ref, cos_ref, sin_ref, u_ref, t_ref, *maybe_kv_ref, rope, tm, lb):
    j = pl.program_id(1)
    tn = PROJ_TN
    n_nat = NAT_W // tn
    kv_blk0 = U_CK // tn

    def rotate(a):
        lane = lax.broadcasted_iota(jnp.int32, (tm, LANES), 1)
        first = (lane % HEAD_DIM) < (HEAD_DIM // 2)
        partner = jnp.where(first, pltpu.roll(a, LANES - HEAD_DIM // 2, 1), pltpu.roll(a, HEAD_DIM // 2, 1))
        return a * cos_ref[...] + partner * sin_ref[...]

    def nat_step(chunk_ops, keep_f32):
        acc = jnp.dot(h_ref[...], w_ref[...], preferred_element_type=F32)
        if keep_f32:
            maybe_kv_ref[0][...] = acc
        if not chunk_ops:
            u_ref[...] = acc.astype(u_ref.dtype)
            return
        for ci in range(tn // LANES):
            a = acc[:, ci * LANES:(ci + 1) * LANES]
            for op in chunk_ops.get(ci, ()):
                a = rotate(a) if op == "rope" else a * QK_SCALE
            u_ref[:, ci * LANES:(ci + 1) * LANES] = a.astype(u_ref.dtype)

    special = {}
    for off, width, ops in ((U_CQ, C_WIDTH, ("scale",)),
                            (U_AQ, A_WIDTH, ("rope", "scale") if rope else ("scale",)),
                            (U_AK, A_KV_WIDTH, ("rope",) if rope else ())):
        for blk, ci in (_lane_chunks(off, width) if ops else ()):
            special.setdefault(blk, {})[ci] = ops
    if maybe_kv_ref:
        assert not rope
        for blk in range(kv_blk0, n_nat):
            special.setdefault(blk, {})

    plain = j < n_nat
    for blk, chunk_ops in special.items():
        plain = plain & (j != blk)
        pl.when(j == blk)(functools.partial(nat_step, chunk_ops, bool(maybe_kv_ref) and blk >= kv_blk0))
    pl.when(plain)(functools.partial(nat_step, None, False))

    @pl.when(j >= n_nat)
    def _():
        acc_t = lax.dot_general(wt_ref[...], h_ref[...], (((1,), (1,)), ((), ())),
                                preferred_element_type=F32)
        if tm <= lb:
            t_ref[0] = acc_t
        else:
            for k in range(tm // lb):
                t_ref[k] = acc_t[:, k * lb:(k + 1) * lb]


def _inproj(h2d, w_nat, w_t, cos_t, sin_t, *, layer, nb, lb, tm, rope, kv_f32):
    rows = nb * lb
    tn = PROJ_TN
    n_nat, n_t = NAT_W // tn, T_W // tn
    kv_blk0, n_kv = U_CK // tn, U_KV_W // tn
    assert U_CK % tn == 0 and kv_blk0 + n_kv == n_nat
    bpb = max(lb // tm, 1)

    if tm <= lb:
        t_spec = pl.BlockSpec((1, tn, tm), lambda i, j: (i // bpb, jnp.clip(j - n_nat, 0, n_t - 1), i % bpb))
    else:
        t_spec = pl.BlockSpec((tm // lb, tn, lb), lambda i, j: (i, jnp.clip(j - n_nat, 0, n_t - 1), 0))

    out_specs = [pl.BlockSpec((tm, tn), lambda i, j: (i, jnp.minimum(j, n_nat - 1))), t_spec]
    out_shape = [jax.ShapeDtypeStruct((rows, NAT_W), BF16), jax.ShapeDtypeStruct((nb, T_W, lb), F32)]
    if kv_f32:
        out_specs.append(pl.BlockSpec((tm, tn), lambda i, j: (i, jnp.clip(j - kv_blk0, 0, n_kv - 1))))
        out_shape.append(jax.ShapeDtypeStruct((rows, U_KV_W), F32))

    kernel = functools.partial(_inproj_kernel, rope=rope, tm=tm, lb=lb)
    return pl.pallas_call(
        kernel,
        grid=(rows // tm, n_nat + n_t),
        in_specs=[
            pl.BlockSpec((tm, D_MODEL), lambda i, j: (i, 0)),
            pl.BlockSpec((None, D_MODEL, tn), lambda i, j: (layer, 0, jnp.minimum(j, n_nat - 1))),
            pl.BlockSpec((None, tn, D_MODEL), lambda i, j: (layer, jnp.maximum(j - n_nat, 0), 0)),
            pl.BlockSpec((tm, LANES), lambda i, j: (i % bpb, 0)),
            pl.BlockSpec((tm, LANES), lambda i, j: (i % bpb, 0)),
        ],
        out_specs=out_specs,
        out_shape=out_shape,
        compiler_params=_cparams(("arbitrary", "arbitrary")),
        name="inproj_rope" if rope else "inproj",
    )(h2d, w_nat, w_t, cos_t, sin_t)


def _softmax_pv(s, v, sink_col):
    m = jnp.max(s, axis=-1, keepdims=True)
    if sink_col is not None:
        m = jnp.maximum(m, sink_col)
    e = jnp.exp2(s - m)
    den = jnp.sum(e, axis=-1, keepdims=True)
    if sink_col is not None:
        den = den + jnp.exp2(sink_col - m)
    o = jnp.dot(e.astype(BF16), v, preferred_element_type=F32)
    return o * (1.0 / den)


def _sink_column(sink_ref, heads, rows_per_head):
    n = len(heads)
    row_head = lax.broadcasted_iota(jnp.int32, (n * rows_per_head, 1), 0) // rows_per_head
    col = jnp.full((n * rows_per_head, 1), sink_ref[heads[-1]], F32)
    for idx in range(n - 1):
        col = jnp.where(row_head == idx, sink_ref[heads[idx]], col)
    return col * LOG2E


def _ctx_attn_kernel(sink_ref, qg_ref, kv0_ref, kv1_ref, ya_ref, yc_ref):
    qg = qg_ref[0]
    kv = jnp.concatenate([kv0_ref[0], kv1_ref[0]], axis=1)
    s_len = qg.shape[0]
    nt = (((1,), (1,)), ((), ()))
    o_cg, o_aq, o_ag = U_CG - U_CQ, U_AQ - U_CQ, U_AG - U_CQ
    o_cv, o_ak, o_av = U_CV - U_CK, U_AK - U_CK, U_AV - U_CK
    head = lambda x, off, h: x[:, off + h * HEAD_DIM:off + (h + 1) * HEAD_DIM]
    a_heads = lambda g: [A_GROUP * g + hh for hh in range(A_GROUP)]

    def scores(job):
        if job < A_KV_HEADS:
            q = jnp.concatenate([head(qg, o_aq, A_PERM.index(h)) for h in a_heads(job)], axis=0)
            k = head(kv, o_ak, job)
        else:
            q = head(qg, 0, job - A_KV_HEADS)
            k = head(kv, 0, job - A_KV_HEADS)
        return lax.dot_general(q, k, nt, preferred_element_type=F32)

    def finish(job, s):
        if job < A_KV_HEADS:
            return _softmax_pv(s, head(kv, o_av, job), _sink_column(sink_ref, a_heads(job), s_len))
        return _softmax_pv(s, head(kv, o_cv, job - A_KV_HEADS), None)

    res = _software_pipeline(A_KV_HEADS + C_HEADS, scores, finish)
    pieces = [None] * A_HEADS
    for g in range(A_KV_HEADS):
        for hh, h in enumerate(a_heads(g)):
            pieces[A_PERM.index(h)] = res[g][hh * s_len:(hh + 1) * s_len]
    ya = jnp.concatenate(pieces, axis=1)
    ag = qg[:, o_ag:o_ag + A_WIDTH].astype(F32)
    ya_ref[0] = (ya * (ag * _sigmoid(ag))).astype(BF16)
    yc = jnp.concatenate(res[A_KV_HEADS:], axis=1)
    cg = qg[:, o_cg:o_cg + C_WIDTH].astype(F32)
    yc_ref[0] = (yc * (cg * _sigmoid(cg))).astype(BF16)


def _ctx_attn(sink, u, *, nb, lb):
    u3 = u.reshape(nb, lb, NAT_W)
    qg_w = U_CK - U_CQ
    assert U_CQ % qg_w == 0 and U_CK % PROJ_TN == 0 and U_KV_W == 2 * PROJ_TN
    return pl.pallas_call(
        _ctx_attn_kernel,
        grid=(nb,),
        in_specs=[
            pl.BlockSpec(memory_space=pltpu.SMEM),
            pl.BlockSpec((1, lb, qg_w), lambda b: (b, 0, U_CQ // qg_w)),
            pl.BlockSpec((1, lb, PROJ_TN), lambda b: (b, 0, U_CK // PROJ_TN)),
            pl.BlockSpec((1, lb, PROJ_TN), lambda b: (b, 0, U_CK // PROJ_TN + 1)),
        ],
        out_specs=[
            pl.BlockSpec((1, lb, A_WIDTH), lambda b: (b, 0, 0)),
            pl.BlockSpec((1, lb, C_WIDTH), lambda b: (b, 0, 0)),
        ],
        out_shape=[
            jax.ShapeDtypeStruct((nb, lb, A_WIDTH), BF16),
            jax.ShapeDtypeStruct((nb, lb, C_WIDTH), BF16),
        ],
        compiler_params=_cparams(("arbitrary",)),
        name="ctx_attn",
    )(sink, u3, u3, u3)


def _own_half(v, half):
    lane = lax.broadcasted_iota(jnp.int32, v.shape, v.ndim - 1) % LANES
    keep = (lane < HEAD_DIM) if half == 0 else (lane >= HEAD_DIM)
    return jnp.where(keep, v, jnp.ones_like(v))


def _stage_request(k_ref, kt_scr, v_ref, v_scr, ck_ref, ckt_scr, cv_ref, cv_scr, chunk):
    n_tok = k_ref.shape[1]

    def body(c, carry):
        start = pl.multiple_of(c * chunk, chunk)
        blk = k_ref[0, pl.ds(start, chunk), :].astype(F32)
        kt_scr[:, pl.ds(start, chunk)] = blk.T.astype(BF16)
        v_blk = v_ref[0, pl.ds(start, chunk), :]
        for half in range(2):
            v_scr[half, pl.ds(start, chunk), :] = _own_half(v_blk, half)
        return carry

    lax.fori_loop(0, n_tok // chunk, body, 0)
    ckt_scr[...] = ck_ref[0, 0].T.astype(BF16)
    ctx_v = cv_ref[0, 0].astype(BF16)
    for half in range(2):
        cv_scr[half] = _own_half(ctx_v, half)


PIPELINE_DEPTH = 2


def _software_pipeline(n, first_stage, second_stage):
    pending = [first_stage(h) for h in range(min(PIPELINE_DEPTH, n))]
    res = []
    for h in range(n):
        if h + PIPELINE_DEPTH < n:
            pending.append(first_stage(h + PIPELINE_DEPTH))
        res.append(second_stage(h, pending.pop(0)))
    return res


def _pair_scores(q_half, kt_loc, bias_fn, kt_ctx):
    s_loc = bias_fn(jnp.dot(q_half, kt_loc, preferred_element_type=F32))
    s_ctx = jnp.dot(q_half, kt_ctx, preferred_element_type=F32)
    return s_loc, s_ctx


def _pair_finish(scores, v_loc, v_ctx, sink_col):
    s_loc, s_ctx = scores
    m = jnp.maximum(jnp.max(s_loc, axis=-1, keepdims=True), jnp.max(s_ctx, axis=-1, keepdims=True))
    if sink_col is not None:
        m = jnp.maximum(m, sink_col)
    e_loc = jnp.exp2(s_loc - m).astype(BF16)
    e_ctx = jnp.exp2(s_ctx - m).astype(BF16)
    o = (jnp.dot(e_loc, v_loc, preferred_element_type=F32)
         + jnp.dot(e_ctx, v_ctx, preferred_element_type=F32))
    den = pltpu.roll(o, HEAD_DIM, 1)
    if sink_col is not None:
        den = den + jnp.exp2(sink_col - m)
    return o * (1.0 / den)


WIN_QBLOCKS = 4


def _lat_win_kernel(sink_ref, q_ref, k_ref, v_ref, ag_ref, ck_ref, cv_ref, o_ref,
                    kt_scr, v_scr, ckt_scr, cv_scr):
    i = pl.program_id(1)
    nblk = pl.num_programs(1) * WIN_QBLOCKS
    blk = A_WINDOW
    span = 3 * blk

    @pl.when(i == 0)
    def _():
        _stage_request(k_ref, kt_scr, v_ref, v_scr, ck_ref, ckt_scr, cv_ref, cv_scr, blk)

    lane = lax.broadcasted_iota(jnp.int32, (blk, LANES), 1)
    low = lane < HEAD_DIM
    starts, masks = [], []
    for sb in range(WIN_QBLOCKS):
        ib = WIN_QBLOCKS * i + sb
        start = pl.multiple_of(jnp.clip(ib - 1, 0, nblk - 3) * blk, blk)
        q_pos = ib * blk + lax.broadcasted_iota(jnp.int32, (blk, span), 0)
        k_pos = start + lax.broadcasted_iota(jnp.int32, (blk, span), 1)
        valid = jnp.abs(q_pos - k_pos) <= A_WINDOW
        starts.append(start)
        masks.append(functools.partial(lambda s, valid: jnp.where(valid, s, NEG_INF), valid=valid))

    def scores(job):
        sb, head = divmod(job, A_HEADS)
        pair, half = divmod(head, 2)
        rows_m = slice(pair // A_GROUP * LANES, (pair // A_GROUP + 1) * LANES)
        qp = q_ref[0, sb * blk:(sb + 1) * blk, pair * LANES:(pair + 1) * LANES]
        q_half = jnp.where(low if half == 0 else ~low, qp, jnp.zeros_like(qp))
        return _pair_scores(q_half, kt_scr[rows_m, pl.ds(starts[sb], span)], masks[sb], ckt_scr[rows_m, :])

    def finish(job, sc):
        sb, head = divmod(job, A_HEADS)
        half = head % 2
        rows_m = slice(head // (2 * A_GROUP) * LANES, (head // (2 * A_GROUP) + 1) * LANES)
        return _pair_finish(sc, v_scr[half, pl.ds(starts[sb], span), rows_m], cv_scr[half, :, rows_m],
                            sink_ref[A_PERM[head]] * LOG2E)

    res = _software_pipeline(WIN_QBLOCKS * A_HEADS, scores, finish)
    for sb in range(WIN_QBLOCKS):
        r = res[sb * A_HEADS:(sb + 1) * A_HEADS]
        ya = jnp.concatenate([jnp.where(low, r[2 * p], r[2 * p + 1]) for p in range(A_HEADS // 2)], axis=1)
        ag = ag_ref[0, sb * blk:(sb + 1) * blk, :].astype(F32)
        o_ref[0, sb * blk:(sb + 1) * blk, :] = (ya * (ag * _sigmoid(ag))).astype(BF16)


def _lat_win_attn(sink, u, cache_k, cache_v, layer, *, nb, lb):
    blk = WIN_QBLOCKS * A_WINDOW
    nblk = lb // blk
    assert lb % blk == 0 and lb // A_WINDOW >= 3
    n_ctx = cache_k.shape[2]
    u3 = u.reshape(nb, lb, NAT_W)
    ctx_spec = pl.BlockSpec((1, 1, n_ctx, A_KV_WIDTH), lambda b, i: (b, layer, 0, 0))
    once = pl.Buffered(1)
    return pl.pallas_call(
        _lat_win_kernel,
        grid=(nb, nblk),
        in_specs=[
            pl.BlockSpec(memory_space=pltpu.SMEM),
            pl.BlockSpec((1, blk, A_WIDTH), lambda b, i: (b, i, U_AQ // A_WIDTH)),
            pl.BlockSpec((1, lb, A_KV_WIDTH), lambda b, i: (b, 0, U_AK // A_KV_WIDTH), pipeline_mode=once),
            pl.BlockSpec((1, lb, A_KV_WIDTH), lambda b, i: (b, 0, U_AV // A_KV_WIDTH), pipeline_mode=once),
            pl.BlockSpec((1, blk, A_WIDTH), lambda b, i: (b, i, U_AG // A_WIDTH)),
            ctx_spec, ctx_spec,
        ],
        out_specs=pl.BlockSpec((1, blk, A_WIDTH), lambda b, i: (b, i, 0)),
        out_shape=jax.ShapeDtypeStruct((nb, lb, A_WIDTH), BF16),
        scratch_shapes=[pltpu.VMEM((A_KV_WIDTH, lb), BF16),
                        pltpu.VMEM((2, lb, A_KV_WIDTH), BF16),
                        pltpu.VMEM((A_KV_WIDTH, n_ctx), BF16),
                        pltpu.VMEM((2, n_ctx, A_KV_WIDTH), BF16)],
        compiler_params=_cparams(("arbitrary", "arbitrary")),
        name="lat_win_attn",
    )(sink, u3, u3, u3, u3, cache_k, cache_v)


NBR_QROWS = 2
NBR_BAND = NA_ROWS + NBR_QROWS
NBR_QBLOCKS = 4


def _lat_nbr_kernel(q_ref, k_ref, v_ref, cg_ref, ck_ref, cv_ref, t2_ref, o_ref,
                    kt_scr, v_scr, ckt_scr, cv_scr, *, grid_rows):
    i = pl.program_id(1)

    @pl.when(i == 0)
    def _():
        _stage_request(k_ref, kt_scr, v_ref, v_scr, ck_ref, ckt_scr, cv_ref, cv_scr, LANES)

    nq = NBR_QROWS * GRID_W
    nk = NBR_BAND * GRID_W
    lane = lax.broadcasted_iota(jnp.int32, (nq, LANES), 1)
    low = lane < HEAD_DIM
    r0s, bss, starts, valids = [], [], [], []
    for sb in range(NBR_QBLOCKS):
        r0 = NBR_QROWS * (NBR_QBLOCKS * i + sb)
        bs = jnp.clip(r0 - NA_ROWS // 2, 0, grid_rows - NBR_BAND)
        q_row = r0 + lax.broadcasted_iota(jnp.int32, (nq, nk), 0) // GRID_W
        rstart = jnp.clip(q_row - NA_ROWS // 2, 0, grid_rows - NA_ROWS)
        k_row = bs + lax.broadcasted_iota(jnp.int32, (nq, nk), 1) // GRID_W
        r0s.append(r0)
        bss.append(bs)
        starts.append(pl.multiple_of(bs * GRID_W, LANES))
        valids.append((k_row >= rstart) & (k_row < rstart + NA_ROWS))

    def scores(job):
        sb, h = divmod(job, C_HEADS)
        m, half = divmod(h, 2)
        rows_m = slice(m * LANES, (m + 1) * LANES)
        bias_rows = []
        for a in range(NBR_QROWS):
            tiles = []
            for p in range(NBR_BAND // 2):
                d = (bss[sb] + 2 * p) - (r0s[sb] + a) + (NA_ROWS - 1)
                idx = jnp.clip(d, -1, 2 * NA_ROWS - 2) + 1
                tiles.append(t2_ref[h, idx])
            bias_rows.append(jnp.concatenate(tiles, axis=1))
        bias = jnp.concatenate(bias_rows, axis=0)
        valid = valids[sb]
        qp = q_ref[0, sb * nq:(sb + 1) * nq, rows_m]
        q_half = jnp.where(low if half == 0 else ~low, qp, jnp.zeros_like(qp))
        return _pair_scores(q_half, kt_scr[rows_m, pl.ds(starts[sb], nk)],
                            lambda s: jnp.where(valid, s + bias, NEG_INF), ckt_scr[rows_m, :])

    def finish(job, sc):
        sb, h = divmod(job, C_HEADS)
        m, half = divmod(h, 2)
        rows_m = slice(m * LANES, (m + 1) * LANES)
        return _pair_finish(sc, v_scr[half, pl.ds(starts[sb], nk), rows_m], cv_scr[half, :, rows_m], None)

    res = _software_pipeline(NBR_QBLOCKS * C_HEADS, scores, finish)
    for sb in range(NBR_QBLOCKS):
        r = res[sb * C_HEADS:(sb + 1) * C_HEADS]
        yc = jnp.concatenate([jnp.where(low, r[2 * m], r[2 * m + 1]) for m in range(C_HEADS // 2)], axis=1)
        cg = cg_ref[0, sb * nq:(sb + 1) * nq, :].astype(F32)
        o_ref[0, sb * nq:(sb + 1) * nq, :] = (yc * (cg * _sigmoid(cg))).astype(BF16)


def _nbr_bias_table(rpb):
    w = np.arange(GRID_W)
    cstart = np.clip(w - NA_COLS // 2, 0, GRID_W - NA_COLS)
    j = np.arange(GRID_W)
    in_win = (j[None, :] >= cstart[:, None]) & (j[None, :] < cstart[:, None] + NA_COLS)
    n_dcol = 2 * NA_COLS - 1
    n_drow = 2 * NA_ROWS - 1
    dcol = j[None, :] - w[:, None] + NA_COLS - 1
    onehot = (np.arange(n_dcol)[:, None, None] == dcol[None]) & in_win[None]
    rp = jnp.pad(rpb.astype(F32) * LOG2E, ((0, 0), (1, 1), (0, 0)))
    r2 = jnp.concatenate([rp[:, :-1], rp[:, 1:]], axis=-1)
    sel = np.zeros((2, n_dcol, GRID_W, 2, GRID_W), np.float32)
    for half in range(2):
        sel[half, :, :, half, :] = onehot
    t = jnp.einsum('hek,kwsj->hewsj', r2, jnp.asarray(sel.reshape(2 * n_dcol, GRID_W, 2, GRID_W)),
                   precision=HIGHEST)
    d = np.arange(2 * NA_ROWS)[:, None] - 1 + np.arange(2)[None, :]
    ok = ((d >= 0) & (d < n_drow))[:, None, :, None] & in_win[None, :, None, :]
    t = jnp.where(jnp.asarray(ok)[None], t, NEG_INF)
    return t.reshape(rpb.shape[0], 2 * NA_ROWS, GRID_W, 2 * GRID_W)


def _lat_nbr_attn(u, cache_k, cache_v, layer, t2, *, nb, lb):
    grid_rows = lb // GRID_W
    step_rows = NBR_QBLOCKS * NBR_QROWS
    assert grid_rows >= NBR_BAND and grid_rows % step_rows == 0 and NBR_QROWS * GRID_W == LANES
    nq = step_rows * GRID_W
    n_ctx = cache_k.shape[2]
    u3 = u.reshape(nb, lb, NAT_W)
    ctx_spec = pl.BlockSpec((1, 1, n_ctx, C_WIDTH), lambda b, i: (b, layer, 0, 0))
    kernel = functools.partial(_lat_nbr_kernel, grid_rows=grid_rows)
    once = pl.Buffered(1)
    return pl.pallas_call(
        kernel,
        grid=(nb, grid_rows // step_rows),
        in_specs=[
            pl.BlockSpec((1, nq, C_WIDTH), lambda b, i: (b, i, U_CQ // C_WIDTH)),
            pl.BlockSpec((1, lb, C_WIDTH), lambda b, i: (b, 0, U_CK // C_WIDTH), pipeline_mode=once),
            pl.BlockSpec((1, lb, C_WIDTH), lambda b, i: (b, 0, U_CV // C_WIDTH), pipeline_mode=once),
            pl.BlockSpec((1, nq, C_WIDTH), lambda b, i: (b, i, U_CG // C_WIDTH)),
            ctx_spec, ctx_spec,
            pl.BlockSpec(t2.shape, lambda b, i: (0, 0, 0, 0), pipeline_mode=once),
        ],
        out_specs=pl.BlockSpec((1, nq, C_WIDTH), lambda b, i: (b, i, 0)),
        out_shape=jax.ShapeDtypeStruct((nb, lb, C_WIDTH), BF16),
        scratch_shapes=[pltpu.VMEM((C_WIDTH, lb), BF16),
                        pltpu.VMEM((2, lb, C_WIDTH), BF16),
                        pltpu.VMEM((C_WIDTH, n_ctx), BF16),
                        pltpu.VMEM((2, n_ctx, C_WIDTH), BF16)],
        compiler_params=_cparams(("arbitrary", "arbitrary")),
        name="lat_nbr_attn",
    )(u3, u3, u3, u3, cache_k, cache_v, t2)


HYENA_EMB_PAD = 40


def _filter_kernel(w1t_ref, b1_ref, fr_ref, w2t_ref, b2_ref, w3t_ref, decay_ref, o_ref, z_scr):
    L = o_ref.shape[1]

    @pl.when(pl.program_id(0) == 0)
    def _():
        t = lax.broadcasted_iota(jnp.int32, (HYENA_EMB_PAD, L), 1).astype(F32) / L
        r = lax.broadcasted_iota(jnp.int32, (HYENA_EMB_PAD, L), 0)
        band = ((r - 1) % HYENA_BANDS + 1).astype(F32) * (2.0 * np.pi)
        ang = t * band
        feats = jnp.where(r == 0, t, jnp.where(r <= HYENA_BANDS, jnp.sin(ang), jnp.cos(ang)))
        z = jnp.dot(w1t_ref[...], feats, precision=HIGHEST, preferred_element_type=F32) + b1_ref[...]
        z = jnp.sin(fr_ref[0] * z)
        z = jnp.dot(w2t_ref[...], z, precision=HIGHEST, preferred_element_type=F32) + b2_ref[...]
        z_scr[...] = jnp.sin(fr_ref[1] * z)

    h = jnp.dot(w3t_ref[...], z_scr[...], precision=HIGHEST, preferred_element_type=F32)
    t_row = lax.broadcasted_iota(jnp.int32, (1, L), 1).astype(F32) / L
    o_ref[...] = h * jnp.exp(-jnp.abs(decay_ref[...]) * t_row)


def _hyena_filters(L, hy_w1, hy_b1, hy_w2, hy_b2, hy_freq, hy_w3, hy_decay):
    hidden = hy_w1.shape[1]
    n_out = hy_w3.shape[1]
    rb = 512
    w1t = jnp.pad(hy_w1.T, ((0, 0), (0, HYENA_EMB_PAD - hy_w1.shape[0])))
    const = lambda a: pl.BlockSpec(a.shape, lambda i: (0,) * a.ndim)
    args = (w1t, hy_b1.reshape(hidden, 1), hy_freq.reshape(2, hidden, 1), hy_w2.T, hy_b2.reshape(hidden, 1))
    out = pl.pallas_call(
        _filter_kernel,
        grid=(n_out // rb,),
        in_specs=[const(a) for a in args] + [
            pl.BlockSpec((rb, hidden), lambda i: (i, 0)),
            pl.BlockSpec((rb, 1), lambda i: (i, 0)),
        ],
        out_specs=pl.BlockSpec((rb, L), lambda i: (i, 0)),
        out_shape=jax.ShapeDtypeStruct((n_out, L), F32),
        scratch_shapes=[pltpu.VMEM((hidden, L), F32)],
        compiler_params=_cparams(("arbitrary",)),
        name="hyena_filter",
    )(*args, hy_w3.T, hy_decay.reshape(n_out, 1))
    return out.reshape(HYENA_ORDER, 2, B_WIDTH, L)


def _short_conv_params(hy_conv_w, hy_conv_b):
    return jnp.concatenate([hy_conv_w.T, hy_conv_b[:, None]], axis=1)


def _dense_dft_consts(S):
    n = 2 * S
    t = np.arange(S)[:, None]
    k = np.arange(n)[None, :]
    ang = -2.0 * np.pi * t * k / n
    cr, ci = np.cos(ang), np.sin(ang)
    wf = np.block([[cr, ci], [-ci, cr]])
    er, ei = cr.T, -ci.T
    wi = np.block([[er, ei], [-ei, er]])
    return jnp.asarray(wf, F32), jnp.asarray(wi, F32)


def _ctx_spec_kernel(f_ref, skip_ref, wf_ref, hr_ref, hi_ref, *, S):
    n = 2 * S
    fwd = f_ref[0, 0]
    bwd = f_ref[0, 1]
    lane = lax.broadcasted_iota(jnp.int32, bwd.shape, 1)
    bwd = jnp.where(lane == 0, 0.0, bwd)
    w = wf_ref[0:S, :]
    ff = jnp.dot(fwd, w, precision=HIGHEST, preferred_element_type=F32)
    fb = jnp.dot(bwd, w, precision=HIGHEST, preferred_element_type=F32)
    skip = skip_ref[0]
    hr_ref[0] = (ff[:, :n] + fb[:, :n] + skip) * (1.0 / n)
    hi_ref[0] = (ff[:, n:] - fb[:, n:]) * (1.0 / n)


def _ctx_spectrum(filt, skip, wf, S):
    n = 2 * S
    kernel = functools.partial(_ctx_spec_kernel, S=S)
    return pl.pallas_call(
        kernel,
        grid=(HYENA_ORDER,),
        in_specs=[
            pl.BlockSpec((1, 2, B_WIDTH, S), lambda o: (o, 0, 0, 0)),
            pl.BlockSpec((1, B_WIDTH, 1), lambda o: (o, 0, 0)),
            pl.BlockSpec(wf.shape, lambda o: (0, 0)),
        ],
        out_specs=[pl.BlockSpec((1, B_WIDTH, n), lambda o: (o, 0, 0))] * 2,
        out_shape=[jax.ShapeDtypeStruct((HYENA_ORDER, B_WIDTH, n), F32)] * 2,
        compiler_params=_cparams(("arbitrary",)),
        name="hyena_ctx_spectrum",
    )(filt, skip.reshape(HYENA_ORDER, B_WIDTH, 1), wf)


def _hyena_ctx_kernel(prm_ref, v_ref, x1_ref, x2_ref, bg_ref, hr_ref, hi_ref, wf_ref, wi_ref, o_ref, *, S):
    nb, cb, _ = v_ref.shape
    half = nb // 2
    n = 2 * S
    lane = lax.broadcasted_iota(jnp.int32, (nb, cb, S), 2)

    def short_conv(u, p):
        prev = jnp.where(lane == 0, 0.0, pltpu.roll(u, 1, 2))
        nxt = jnp.where(lane == S - 1, 0.0, pltpu.roll(u, S - 1, 2))
        return p[:, 3:4] + prev * p[:, 0:1] + u * p[:, 1:2] + nxt * p[:, 2:3]

    def stack(u):
        return jnp.concatenate([u[:half], u[half:]], axis=-1)

    def conv(xs, o):
        spec = jnp.dot(xs.reshape(half * cb, 2 * S).astype(BF16), wf_ref[...],
                       preferred_element_type=F32).reshape(half, cb, 2 * n)
        xr, xi = spec[..., :n], spec[..., n:]
        hr, hi = hr_ref[o], hi_ref[o]
        y = jnp.concatenate([xr * hr - xi * hi, xr * hi + xi * hr], axis=-1)
        return jnp.dot(y.reshape(half * cb, 2 * n).astype(BF16), wi_ref[...],
                       preferred_element_type=F32).reshape(half, cb, 2 * S)

    v = stack(short_conv(v_ref[...], prm_ref[0]))
    x1 = stack(short_conv(x1_ref[...], prm_ref[1]))
    x2 = stack(short_conv(x2_ref[...], prm_ref[2]))
    z = x1 * conv(v, 0)
    y = x2 * conv(z, 1)
    bg = stack(bg_ref[...])
    y = y * (bg * _sigmoid(bg))
    o_ref[0:half] = y[..., :S]
    o_ref[half:nb] = y[..., S:]


def _hyena_ctx(t_arr, prm, hr, hi, wf, wi, *, nb, S):
    cb = 16
    n = 2 * S
    ncb = B_WIDTH // cb
    kernel = functools.partial(_hyena_ctx_kernel, S=S)

    def part(k):
        return pl.BlockSpec((nb, cb, S), lambda c: (0, c + k * ncb, 0))

    return pl.pallas_call(
        kernel,
        grid=(ncb,),
        in_specs=[
            pl.BlockSpec((3, cb, 4), lambda c: (0, c, 0)),
            part(0), part(1), part(2), part(3),
            pl.BlockSpec((HYENA_ORDER, cb, n), lambda c: (0, c, 0)),
            pl.BlockSpec((HYENA_ORDER, cb, n), lambda c: (0, c, 0)),
            pl.BlockSpec(wf.shape, lambda c: (0, 0)),
            pl.BlockSpec(wi.shape, lambda c: (0, 0)),
        ],
        out_specs=pl.BlockSpec((nb, cb, S), lambda c: (0, c, 0)),
        out_shape=jax.ShapeDtypeStruct((nb, B_WIDTH, S), F32),
        compiler_params=_cparams(("arbitrary",)),
        name="hyena_ctx",
    )(prm.reshape(3, B_WIDTH, 4), t_arr, t_arr, t_arr, t_arr, hr, hi, wf.astype(BF16), wi.astype(BF16))


def _two_stage_consts(L, nseq, dtype):
    n2 = LANES
    n = 2 * L
    n1 = n // n2
    h1 = n1 // 2
    k1 = np.arange(n1)[:, None]
    a = -2.0 * np.pi * k1 * np.arange(h1)[None, :] / n1
    w1r, w1i = np.cos(a), np.sin(a)
    w1big = np.block([[w1r, -w1i], [w1i, w1r]])
    vr, vi = w1r.T, -w1i.T
    w1inv = np.block([[vr, -vi], [vi, vr]])
    a = -2.0 * np.pi * k1 * np.arange(n2)[None, :] / n
    twr, twi = np.cos(a), np.sin(a)
    a = -2.0 * np.pi * np.arange(n2)[:, None] * np.arange(n2)[None, :] / n2
    w2r, w2i = np.cos(a), np.sin(a)
    w2big = np.block([[w2r, w2i], [-w2i, w2r]])
    w2conj = np.block([[w2r, -w2i], [w2i, w2r]])
    f = lambda x: jnp.asarray(x, F32)
    m = lambda x: jnp.asarray(x, F32).astype(dtype)
    return dict(
        w1b=jnp.broadcast_to(m(w1big)[None], (nseq,) + w1big.shape),
        w1ib=jnp.broadcast_to(m(w1inv)[None], (nseq,) + w1inv.shape),
        w2=m(w2big), w2c=m(w2conj), twr=f(twr), twi=f(twi), n1=n1, h1=h1)


def _dft_dot(spec, a, b):
    if a.dtype == BF16 or b.dtype == BF16:
        return jnp.einsum(spec, a.astype(BF16), b.astype(BF16), preferred_element_type=F32)
    a_hi, b_hi = a.astype(BF16), b.astype(BF16)
    a_lo = (a - a_hi.astype(F32)).astype(BF16)
    b_lo = (b - b_hi.astype(F32)).astype(BF16)
    mm = lambda x, y: jnp.einsum(spec, x, y, preferred_element_type=F32)
    return mm(a_hi, b_hi) + (mm(a_hi, b_lo) + mm(a_lo, b_hi))


def _cmul(ar, ai, br, bi):
    return ar * br - ai * bi, ar * bi + ai * br


def _stage_fwd(xs, w1b, twr, twi, w2):
    s = xs.shape[0]
    n1 = twr.shape[0]
    a = _dft_dot('smk,skn->smn', w1b, xs)
    p = jnp.concatenate(_cmul(a[:, :n1], a[:, n1:], twr, twi), axis=-1)
    return _dft_dot('mk,kn->mn', p.reshape(s * n1, 2 * LANES), w2)


def _stage_inv(cm, w1ib, twr, twi, w2c, s):
    n1 = twr.shape[0]
    dm = _dft_dot('mk,kn->mn', cm, w2c).reshape(s, n1, 2 * LANES)
    r = jnp.concatenate(_cmul(dm[..., :LANES], dm[..., LANES:], twr, -twi), axis=1)
    return _dft_dot('smk,skn->smn', w1ib, r)


def _lat_spec_kernel(skip_ref, f_ref, w1b_ref, twr_ref, twi_ref, w2_ref, hr_ref, hi_ref, *, n_fft):
    cb, h1 = f_ref.shape[2], f_ref.shape[3]
    n1 = twr_ref.shape[0]
    fwd = f_ref[0, 0]
    bwd = f_ref[0, 1]
    first = (lax.broadcasted_iota(jnp.int32, bwd.shape, 1) == 0) & \
            (lax.broadcasted_iota(jnp.int32, bwd.shape, 2) == 0)
    bwd = jnp.where(first, 0.0, bwd)
    xs = jnp.concatenate([fwd, bwd], axis=0)
    w1_real = w1b_ref[:, :, 0:h1]
    sp = _stage_fwd(xs, w1_real, twr_ref[...], twi_ref[...], w2_ref[...]).reshape(2, cb, n1, 2 * LANES)
    inv = 1.0 / n_fft
    for c in range(cb):
        skip = skip_ref[pl.program_id(0), pl.program_id(1) * cb + c]
        hr_ref[0, c] = (sp[0, c, :, :LANES] + sp[1, c, :, :LANES] + skip) * inv
        hi_ref[0, c] = (sp[0, c, :, LANES:] - sp[1, c, :, LANES:]) * inv


def _lat_spectrum(filt, skip, L):
    cb = 16
    cs = _two_stage_consts(L, 2 * cb, F32)
    n1, h1 = cs["n1"], cs["h1"]
    filt5 = filt.reshape(HYENA_ORDER, 2, B_WIDTH, h1, LANES)
    kernel = functools.partial(_lat_spec_kernel, n_fft=2 * L)
    const = lambda a: pl.BlockSpec(a.shape, lambda o, c: (0,) * a.ndim)
    return pl.pallas_call(
        kernel,
        grid=(HYENA_ORDER, B_WIDTH // cb),
        in_specs=[
            pl.BlockSpec(memory_space=pltpu.SMEM),
            pl.BlockSpec((1, 2, cb, h1, LANES), lambda o, c: (o, 0, c, 0, 0)),
            const(cs["w1b"]), const(cs["twr"]), const(cs["twi"]), const(cs["w2"]),
        ],
        out_specs=[pl.BlockSpec((1, cb, n1, LANES), lambda o, c: (o, c, 0, 0))] * 2,
        out_shape=[jax.ShapeDtypeStruct((HYENA_ORDER, B_WIDTH, n1, LANES), F32)] * 2,
        compiler_params=_cparams(("arbitrary", "arbitrary")),
        name="hyena_lat_spectrum",
    )(skip, filt5, cs["w1b"], cs["twr"], cs["twi"], cs["w2"])


def _hyena_lat_kernel(prm_ref, v_ref, x1_ref, x2_ref, bg_ref, hr_ref, hi_ref,
                      w1b_ref, w1ib_ref, twr_ref, twi_ref, w2_ref, w2c_ref, o_ref,
                      vs_ref, x1s_ref, x2s_ref):
    nb, cb, h1, _ = v_ref.shape
    half = nb // 2
    s = cb * half
    n1 = twr_ref.shape[0]
    c0 = pl.program_id(0) * cb
    row = lax.broadcasted_iota(jnp.int32, (nb, h1, LANES), 1)
    lane = lax.broadcasted_iota(jnp.int32, (nb, h1, LANES), 2)

    def short_conv(u, part, c):
        r = pltpu.roll(u, 1, 2)
        prev = jnp.where(lane == 0, pltpu.roll(r, 1, 1), r)
        prev = jnp.where((lane == 0) & (row == 0), 0.0, prev)
        r = pltpu.roll(u, LANES - 1, 2)
        nxt = jnp.where(lane == LANES - 1, pltpu.roll(r, h1 - 1, 1), r)
        nxt = jnp.where((lane == LANES - 1) & (row == h1 - 1), 0.0, nxt)
        ch = part * B_WIDTH + c0 + c
        return prm_ref[3, ch] + prev * prm_ref[0, ch] + u * prm_ref[1, ch] + nxt * prm_ref[2, ch]

    def stack_into(dst_ref, u, c):
        dst_ref[c * half:(c + 1) * half, 0:h1, :] = u[:half]
        dst_ref[c * half:(c + 1) * half, h1:2 * h1, :] = u[half:]

    for c in range(cb):
        stack_into(vs_ref, short_conv(v_ref[:, c], 0, c), c)
        stack_into(x1s_ref, short_conv(x1_ref[:, c], 1, c), c)
        bg = bg_ref[:, c]
        stack_into(x2s_ref, short_conv(x2_ref[:, c], 2, c) * (bg * _sigmoid(bg)), c)

    twr, twi = twr_ref[...], twi_ref[...]

    def conv(xs, o):
        sp = _stage_fwd(xs, w1b_ref[...], twr, twi, w2_ref[...]).reshape(cb, half, n1, 2 * LANES)
        hr = hr_ref[o][:, None]
        hi = hi_ref[o][:, None]
        cm = jnp.concatenate(_cmul(sp[..., :LANES], sp[..., LANES:], hr, hi), axis=-1)
        return _stage_inv(cm.reshape(s * n1, 2 * LANES), w1ib_ref[...], twr, twi, w2c_ref[...], s)

    z = x1s_ref[...] * conv(vs_ref[...], 0)
    y = x2s_ref[...] * conv(z, 1)
    for c in range(cb):
        o_ref[0:half, c] = y[c * half:(c + 1) * half, 0:h1]
        o_ref[half:nb, c] = y[c * half:(c + 1) * half, h1:2 * h1]


def _hyena_lat(t_arr, prm, hr, hi, *, nb, L):
    cb = 8
    half = nb // 2
    s = cb * half
    cs = _two_stage_consts(L, s, BF16)
    n1, h1 = cs["n1"], cs["h1"]
    ncb = B_WIDTH // cb
    t5 = t_arr.reshape(nb, T_W, h1, LANES)

    def part(k):
        return pl.BlockSpec((nb, cb, h1, LANES), lambda c: (0, c + k * ncb, 0, 0))

    const = lambda a: pl.BlockSpec(a.shape, lambda c: (0,) * a.ndim)
    return pl.pallas_call(
        _hyena_lat_kernel,
        grid=(ncb,),
        in_specs=[
            pl.BlockSpec(memory_space=pltpu.SMEM),
            part(0), part(1), part(2), part(3),
            pl.BlockSpec((HYENA_ORDER, cb, n1, LANES), lambda c: (0, c, 0, 0)),
            pl.BlockSpec((HYENA_ORDER, cb, n1, LANES), lambda c: (0, c, 0, 0)),
            const(cs["w1b"]), const(cs["w1ib"]), const(cs["twr"]), const(cs["twi"]),
            const(cs["w2"]), const(cs["w2c"]),
        ],
        out_specs=pl.BlockSpec((nb, cb, h1, LANES), lambda c: (0, c, 0, 0)),
        out_shape=jax.ShapeDtypeStruct((nb, B_WIDTH, h1, LANES), F32),
        scratch_shapes=[pltpu.VMEM((s, 2 * h1, LANES), F32)] * 3,
        compiler_params=_cparams(("arbitrary",)),
        name="hyena_lat",
    )(prm.T, t5, t5, t5, t5, hr, hi,
      cs["w1b"], cs["w1ib"], cs["twr"], cs["twi"], cs["w2"], cs["w2c"])


MERGE_SPLITS = 4


def _merge_kernel(x_ref, ya_ref, ybt_ref, yc_ref, mg_ref, gate_ref, wa_ref, wb_ref, wc_ref, wo_ref,
                  nw_ref, *rest, final):
    if final:
        (o_ref,) = rest
    else:
        sc_ref, sh_ref, o_ref, h_ref = rest
    mg = mg_ref[...]
    tm = mg.shape[0]
    if len(ybt_ref.shape) == 4:
        n_rows = tm // LANES
        row0 = (pl.program_id(0) % (ybt_ref.shape[2] // n_rows)) * n_rows
        yb = jnp.concatenate([ybt_ref[0, :, row0 + r, :].T for r in range(n_rows)], axis=0).astype(BF16)
    else:
        yb = ybt_ref[0].T.astype(BF16)
    gw = D_MODEL // MERGE_SPLITS
    branches = ((ya_ref[...], wa_ref), (yb, wb_ref), (yc_ref[...], wc_ref))

    def up_proj(c):
        return [jnp.dot(y, w_ref[:, c * gw:(c + 1) * gw], preferred_element_type=F32) for y, w_ref in branches]

    def gate_out(c, ups):
        m = sum(_sigmoid(mg[:, k * D_MODEL + c * gw:k * D_MODEL + (c + 1) * gw].astype(F32)) * ups[k]
                for k in range(len(branches)))
        return jnp.dot(m.astype(BF16), wo_ref[c * gw:(c + 1) * gw, :], preferred_element_type=F32)

    ups = [up_proj(c) for c in range(MERGE_SPLITS)]
    out = sum(gate_out(c, ups[c]) for c in range(MERGE_SPLITS))
    xn = x_ref[...] + gate_ref[0] * out
    if final:
        var = jnp.mean(xn * xn, axis=-1, keepdims=True)
        o_ref[...] = xn * lax.rsqrt(var + EPS) * nw_ref[...]
    else:
        o_ref[...] = xn
        h_ref[...] = _modulated_norm(xn, nw_ref[...], sc_ref[0], sh_ref[0])


def _merge(x2d, ya, ybt, yc, u, gate, wa, wb, wc, wo, nw, next_mod, *, nb, lb, tm):
    rows = nb * lb
    bpb = lb // tm
    per_mod = gate.shape[0] > 1
    final = next_mod is None
    kernel = functools.partial(_merge_kernel, final=final)
    mod_spec = pl.BlockSpec((1, 1, D_MODEL), lambda i: ((i // bpb) if per_mod else 0, 0, 0))
    row_spec = pl.BlockSpec((tm, D_MODEL), lambda i: (i, 0))
    const = lambda a: pl.BlockSpec(a.shape, lambda i: (0,) * a.ndim, pipeline_mode=pl.Buffered(1))
    if ybt.ndim == 4:
        sub = 8 * LANES // tm
        assert tm % LANES == 0 and (8 * LANES) % tm == 0 and bpb % sub == 0
        yb_spec = pl.BlockSpec((1, B_WIDTH, 8, LANES), lambda i: (i // bpb, 0, (i % bpb) // sub, 0))
    else:
        yb_spec = pl.BlockSpec((1, B_WIDTH, tm), lambda i: (i // bpb, 0, i % bpb))
    return pl.pallas_call(
        kernel,
        grid=(rows // tm,),
        in_specs=[
            pl.BlockSpec((tm, D_MODEL), lambda i: (i, 0)),
            pl.BlockSpec((tm, A_WIDTH), lambda i: (i, 0)),
            yb_spec,
            pl.BlockSpec((tm, C_WIDTH), lambda i: (i, 0)),
            pl.BlockSpec((tm, MG_W), lambda i: (i, U_MG // MG_W)),
            mod_spec,
            const(wa), const(wb), const(wc), const(wo),
            pl.BlockSpec((1, D_MODEL), lambda i: (0, 0)),
        ] + ([] if final else [mod_spec, mod_spec]),
        out_specs=row_spec if final else [row_spec, row_spec],
        out_shape=(jax.ShapeDtypeStruct((rows, D_MODEL), F32) if final else
                   [jax.ShapeDtypeStruct((rows, D_MODEL), F32), jax.ShapeDtypeStruct((rows, D_MODEL), BF16)]),
        compiler_params=_cparams(("arbitrary",)),
        name="merge_final" if final else "merge",
    )(x2d, ya.reshape(rows, A_WIDTH), ybt, yc.reshape(rows, C_WIDTH), u, gate, wa, wb, wc, wo,
      nw.reshape(1, D_MODEL), *(() if final else next_mod))


def _rope_tables(L):
    t = np.arange(L)
    row = (t // GRID_W).astype(np.float32)
    col = (t % GRID_W).astype(np.float32)
    nf = HEAD_DIM // 4
    inv = np.power(np.float32(ROPE_BASE), -np.arange(nf, dtype=np.float32) / nf).astype(np.float32)
    ang = np.concatenate([row[:, None] * inv[None], col[:, None] * inv[None]], axis=-1)
    cos, sin = np.cos(ang), np.sin(ang)
    reps = LANES // HEAD_DIM
    cos_t = np.tile(np.concatenate([cos, cos], axis=-1), (1, reps))
    sin_t = np.tile(np.concatenate([-sin, sin], axis=-1), (1, reps))
    return jnp.asarray(cos_t, F32), jnp.asarray(sin_t, F32)


def _perm_heads(w, base, axis):
    return [lax.slice_in_dim(w, base + h * HEAD_DIM, base + (h + 1) * HEAD_DIM, axis=axis) for h in A_PERM]


def _prep_w_in_kernel(w_ref, nat_ref, t_ref):
    def put(dst, src, width):
        nat_ref[0, :, dst:dst + width] = w_ref[0, :, src:src + width].astype(BF16)

    put(U_MG, IN_MG, MG_W)
    for dst, k in ((U_CQ, 0), (U_CK, 1), (U_CV, 2), (U_CG, 3)):
        put(dst, IN_CQ + k * C_WIDTH, C_WIDTH)
    for p, h in enumerate(A_PERM):
        put(U_AQ + p * HEAD_DIM, IN_AQ + h * HEAD_DIM, HEAD_DIM)
        put(U_AG + p * HEAD_DIM, IN_AG + h * HEAD_DIM, HEAD_DIM)
    put(U_AK, IN_AK, A_KV_WIDTH)
    put(U_AV, IN_AV, A_KV_WIDTH)
    t_ref[0] = w_ref[0, :, IN_BU:IN_BU + T_W].T.astype(BF16)


def _prep_w_in(w_in):
    depth, _, in_w = w_in.shape
    rb = 128
    return pl.pallas_call(
        _prep_w_in_kernel,
        grid=(depth, D_MODEL // rb),
        in_specs=[pl.BlockSpec((1, rb, in_w), lambda l, i: (l, i, 0))],
        out_specs=[pl.BlockSpec((1, rb, NAT_W), lambda l, i: (l, i, 0)),
                   pl.BlockSpec((1, T_W, rb), lambda l, i: (l, 0, i))],
        out_shape=[jax.ShapeDtypeStruct((depth, D_MODEL, NAT_W), BF16),
                   jax.ShapeDtypeStruct((depth, T_W, D_MODEL), BF16)],
        compiler_params=_cparams(("arbitrary", "arbitrary")),
        name="prep_w_in",
    )(w_in)


def kernel(x_prompt, x_sample, c, cache_a_k, cache_a_v, cache_c_k, cache_c_v, c_ctx, norm_w, w_ada, b_ada, w_in, a_sink, hy_conv_w, hy_conv_b, hy_w1, hy_b1, hy_w2, hy_b2, hy_freq, hy_w3, hy_decay, hy_skip, c_rpb, w_up_a, w_up_b, w_up_c, w_out, final_norm_w):
    nbc, S, _ = x_prompt.shape
    nbl, L, _ = x_sample.shape
    depth = w_in.shape[0]
    n_ctx = cache_a_k.shape[2]
    assert nbc % 2 == 0 and nbl % 2 == 0 and L % (GRID_W * LANES // 2) == 0

    pad = (-(nbl + 1)) % 8
    cond = jnp.concatenate([c, c_ctx[None], jnp.zeros((pad, D_MODEL), F32)], axis=0)
    mod = _adaln(cond, w_ada, b_ada)

    cos_t, sin_t = _rope_tables(L)
    tm_ctx = 1024 if (nbc * S) % 1024 == 0 else S
    zeros_t = jnp.zeros((tm_ctx, LANES), F32)
    wf_c, wi_c = _dense_dft_consts(S)
    ca_k = cache_a_k.reshape(nbl, depth, n_ctx, A_KV_WIDTH)
    ca_v = cache_a_v.reshape(nbl, depth, n_ctx, A_KV_WIDTH)
    cc_k = cache_c_k.reshape(nbl, depth, n_ctx, C_WIDTH)
    cc_v = cache_c_v.reshape(nbl, depth, n_ctx, C_WIDTH)

    w_nat, w_t = _prep_w_in(w_in)

    def mod_parts(l, rows):
        m = mod[l, rows][:, None, :]
        return 1.0 + m[..., D_MODEL:2 * D_MODEL], m[..., :D_MODEL], m[..., 2 * D_MODEL:]

    ctx_rows, lat_rows = slice(nbl, nbl + 1), slice(0, nbl)
    xp = x_prompt.reshape(nbc * S, D_MODEL)
    xs = x_sample.reshape(nbl * L, D_MODEL)
    hp = _norm_mod(xp, *mod_parts(0, ctx_rows)[:2], norm_w[0], nb=nbc, lb=S)
    hs = _norm_mod(xs, *mod_parts(0, lat_rows)[:2], norm_w[0], nb=nbl, lb=L)
    aks, avs, cks, cvs = [], [], [], []
    for l in range(depth):
        final = l == depth - 1
        next_nw = final_norm_w if final else norm_w[l + 1]
        wa = jnp.concatenate(_perm_heads(w_up_a[l], 0, 0), axis=0).astype(BF16)
        wb, wc, wo = (w.astype(BF16) for w in (w_up_b[l], w_up_c[l], w_out[l]))
        prm = _short_conv_params(hy_conv_w[l], hy_conv_b[l])
        filt_args = (hy_w1[l], hy_b1[l], hy_w2[l], hy_b2[l], hy_freq[l], hy_w3[l], hy_decay[l])
        t2 = _nbr_bias_table(c_rpb[l])

        gate = mod_parts(l, ctx_rows)[2]
        u, t_arr, kv = _inproj(hp, w_nat, w_t, zeros_t, zeros_t,
                               layer=l, nb=nbc, lb=S, tm=tm_ctx, rope=False, kv_f32=True)
        cks.append(kv[:, :U_CV - U_CK].reshape(nbc, S, C_HEADS, HEAD_DIM))
        cvs.append(kv[:, U_CV - U_CK:U_AK - U_CK].reshape(nbc, S, C_HEADS, HEAD_DIM))
        aks.append(kv[:, U_AK - U_CK:U_AV - U_CK].reshape(nbc, S, A_KV_HEADS, HEAD_DIM))
        avs.append(kv[:, U_AV - U_CK:].reshape(nbc, S, A_KV_HEADS, HEAD_DIM))
        ya, yc = _ctx_attn(a_sink[l], u, nb=nbc, lb=S)
        hr, hi = _ctx_spectrum(_hyena_filters(S, *filt_args), hy_skip[l], wf_c, S)
        ybt = _hyena_ctx(t_arr, prm, hr, hi, wf_c, wi_c, nb=nbc, S=S)
        res = _merge(xp, ya, ybt, yc, u, gate, wa, wb, wc, wo, next_nw,
                     None if final else mod_parts(l + 1, ctx_rows)[:2], nb=nbc, lb=S, tm=S)
        xp, hp = (res, None) if final else res

        gate = mod_parts(l, lat_rows)[2]
        u, t_arr = _inproj(hs, w_nat, w_t, cos_t, sin_t,
                           layer=l, nb=nbl, lb=L, tm=1024, rope=True, kv_f32=False)
        ya = _lat_win_attn(a_sink[l], u, ca_k, ca_v, l, nb=nbl, lb=L)
        yc = _lat_nbr_attn(u, cc_k, cc_v, l, t2, nb=nbl, lb=L)
        hr, hi = _lat_spectrum(_hyena_filters(L, *filt_args), hy_skip[l], L)
        ybt = _hyena_lat(t_arr, prm, hr, hi, nb=nbl, L=L)
        res = _merge(xs, ya, ybt, yc, u, gate, wa, wb, wc, wo, next_nw,
                     None if final else mod_parts(l + 1, lat_rows)[:2], nb=nbl, lb=L, tm=256)
        xs, hs = (res, None) if final else res

    y_prompt = xp.reshape(nbc, S, D_MODEL)
    y_sample = xs.reshape(nbl, L, D_MODEL)
    return (y_prompt, y_sample, jnp.stack(aks, axis=1), jnp.stack(avs, axis=1),
            jnp.stack(cks, axis=1), jnp.stack(cvs, axis=1))
```
